```python
import math
import jax
import jax.numpy as jnp
from jax import lax
import numpy as np

D_MODEL = 1024
BATCH = 1
SEQ = 16384
DEPTH = 1
DEC_BATCH = 128
DEC_SEQ = 4
PAST_LEN = 8192
PAGE_SIZE = 128

HEAD_DIM = 64
N_HEADS_A = 8
PATTERNS = ((128, 1), (512, 4), (2048, 16))
MAX_WINDOW = 2048
N_HEADS_B = 4
DK_B = 64
DV_B = 64
CONV_B = 4
CHUNK_B = 64
N_HEADS_C = 4
N_MEM = 256
D_FF = 2816
CONV_FFN = 3
EPS = 1e-6
WIDTH_A = N_HEADS_A * HEAD_DIM
WIDTH_B = N_HEADS_B * DV_B
WIDTH_C = N_HEADS_C * HEAD_DIM
D_MIX = WIDTH_A + WIDTH_B + WIDTH_C
QKV_B = N_HEADS_B * (2 * DK_B + DV_B)
IN_SPLITS = (WIDTH_A, WIDTH_A, WIDTH_A, QKV_B, WIDTH_B, N_HEADS_B, N_HEADS_B, WIDTH_C)
N_IN = sum(IN_SPLITS)

kernel_name = 'hybrid_dilated_gdn_memory_decoder_step'

F32 = jnp.float32


def rms_norm(x, g):
    xf = x.astype(F32)
    y = xf * lax.rsqrt(jnp.mean(xf * xf, axis=-1, keepdims=True) + EPS)
    return (y * g.astype(F32)).astype(x.dtype)


def l2_normalize(x):
    xf = x.astype(F32)
    return xf * lax.rsqrt(jnp.sum(xf * xf, axis=-1, keepdims=True) + EPS)


def split_heads(a, n):
    return a.reshape(*a.shape[:-1], n, a.shape[-1] // n)


def causal_dwconv(x_ext, w):
    return lax.conv_general_dilated(x_ext, w[:, None, :].astype(x_ext.dtype), window_strides=(1,),
                                    padding='VALID', dimension_numbers=('NWC', 'WIO', 'NWC'),
                                    feature_group_count=x_ext.shape[-1])


def alibi_slopes(n):
    return jnp.exp2(-8.0 * jnp.arange(1, n + 1, dtype=F32) / n)


def dilated_band_prompt(q, k, v, slopes, window, dil):
    b, t, h, e = q.shape
    nband = window // dil
    span = dil * nband
    t_pad = -(-t // span) * span
    sub_len = t_pad // dil
    nblk = sub_len // nband

    def to_sub(a):
        a = jnp.pad(a, ((0, 0), (0, t_pad - t), (0, 0), (0, 0)))
        a = jnp.moveaxis(a.reshape(b, sub_len, dil, h, e), 1, 3)
        return a.reshape(b, dil, h, nblk, nband, e)

    def with_prev(a):
        prev = jnp.pad(a, ((0, 0), (0, 0), (0, 0), (1, 0), (0, 0), (0, 0)))[:, :, :, :-1]
        return jnp.concatenate([prev, a], axis=4)

    def from_sub(a):
        rest = a.shape[5:]
        a = a.reshape(b, dil, h, sub_len, *rest)
        return jnp.moveaxis(a, 3, 1).reshape(b, t_pad, h, *rest)[:, :t]

    qs = to_sub(q)
    ks = with_prev(to_sub(k))
    vs = with_prev(to_sub(v))
    s = jnp.einsum('brhnqe,brhnke->brhnqk', qs, ks, preferred_element_type=F32) * (e ** -0.5)
    qi = jnp.arange(nband)[:, None]
    kj = jnp.arange(2 * nband)[None, :]
    delta = qi + nband - kj
    in_band = (delta >= 0) & (delta <= nband)
    exists = (jnp.arange(nblk)[:, None, None] > 0) | (kj >= nband)[None]
    valid = in_band[None] & exists
    s = s - slopes[:, None, None, None] * (delta * dil).astype(F32)
    s = jnp.where(valid, s, -jnp.inf)
    mx = jnp.max(s, axis=-1)
    p = jnp.exp(s - mx[..., None])
    den = jnp.sum(p, axis=-1)
    num = jnp.einsum('brhnqk,brhnke->brhnqe', p, vs.astype(F32))
    return from_sub(num), from_sub(den), from_sub(mx)


def dilated_gather_sample(q, k_all, v_all, slopes, window, dil):
    b, s_len, h, e = q.shape
    past = k_all.shape[1] - s_len
    steps = jnp.arange(window // dil + 1)
    idx = past + jnp.arange(s_len)[:, None] - steps[None, :] * dil
    valid = idx >= 0
    idx = jnp.maximum(idx, 0)
    kg = k_all[:, idx]
    vg = v_all[:, idx]
    s = jnp.einsum('bshe,bsjhe->bshj', q, kg, preferred_element_type=F32) * (e ** -0.5)
    s = s - slopes[:, None] * (steps * dil).astype(F32)[None, :]
    s = jnp.where(valid[:, None, :], s, -jnp.inf)
    mx = jnp.max(s, axis=-1)
    p = jnp.exp(s - mx[..., None])
    den = jnp.sum(p, axis=-1)
    num = jnp.einsum('bshj,bsjhe->bshe', p, vg.astype(F32))
    return num, den, mx


def combine_dilations(parts):
    mx = jnp.max(jnp.stack([m for _, _, m in parts]), axis=0)
    num = sum(n * jnp.exp(m - mx)[..., None] for n, _, m in parts)
    den = sum(d * jnp.exp(m - mx) for _, d, m in parts)
    return num / den[..., None]


def gated_delta_chunked(q, k, v, g, beta, s0, chunk):
    b, t, h, _ = q.shape
    dv = v.shape[-1]
    n = t // chunk

    def blocks(a):
        a = a.reshape(b, n, chunk, h, *a.shape[3:])
        return jnp.swapaxes(jnp.moveaxis(a, 3, 2), 0, 1)

    qc, kc, vc = blocks(q), blocks(k), blocks(v)
    gc = jnp.cumsum(blocks(g), axis=-1)
    bc = blocks(beta)
    causal = jnp.tril(jnp.ones((chunk, chunk), dtype=bool))
    strict = jnp.tril(jnp.ones((chunk, chunk), dtype=bool), -1)
    decay = jnp.exp(jnp.where(causal, gc[..., :, None] - gc[..., None, :], -jnp.inf))
    kb = kc * bc[..., None]
    m = jnp.where(strict, jnp.einsum('nbhid,nbhjd->nbhij', kb, kc) * decay, 0.0)
    eye = jnp.eye(chunk, dtype=F32)
    tinv = lax.linalg.triangular_solve(eye + m, jnp.broadcast_to(eye, m.shape), left_side=True,
                                       lower=True, unit_diagonal=True)
    u = tinv @ (vc * bc[..., None])
    w = tinv @ (kb * jnp.exp(gc)[..., None])
    attn = jnp.einsum('nbhid,nbhjd->nbhij', qc, kc) * decay
    q_dec = qc * jnp.exp(gc)[..., None]
    g_last = gc[..., -1]
    k_dec = kc * jnp.exp(g_last[..., None] - gc)[..., None]

    def step(s, xs):
        u_n, w_n, attn_n, qd_n, kd_n, gl_n = xs
        v_new = u_n - w_n @ s
        o = qd_n @ s + attn_n @ v_new
        s = jnp.exp(gl_n)[..., None, None] * s + jnp.swapaxes(kd_n, -1, -2) @ v_new
        return s, o

    s_fin, o = lax.scan(step, s0, (u, w, attn, q_dec, k_dec, g_last))
    o = jnp.moveaxis(jnp.swapaxes(o, 0, 1), 2, 3).reshape(b, t, h, dv)
    return o, s_fin


def mixer_inputs(x, norm1_g, w_in, q_norm_a, k_norm_a, q_norm_c):
    z = rms_norm(x, norm1_g) @ w_in
    offs = [int(i) for i in np.cumsum(IN_SPLITS)[:-1]]
    qa, ka, va, qkv_b, gate_b, a_b, b_b, qc = jnp.split(z, offs, axis=-1)
    qa = rms_norm(split_heads(qa, N_HEADS_A), q_norm_a)
    ka = rms_norm(split_heads(ka, N_HEADS_A), k_norm_a)
    va = split_heads(va, N_HEADS_A)
    qc = rms_norm(split_heads(qc, N_HEADS_C), q_norm_c)
    return qa, ka, va, qkv_b, gate_b, a_b, b_b, qc


def gated_delta_mixer(qkv_ext, gate_b, a_b, b_b, conv_w, a_log, dt_bias, out_norm, s0):
    qkv = jax.nn.silu(causal_dwconv(qkv_ext, conv_w).astype(F32))
    q, k, v = jnp.split(qkv, [N_HEADS_B * DK_B, 2 * N_HEADS_B * DK_B], axis=-1)
    q = l2_normalize(split_heads(q, N_HEADS_B)) * (DK_B ** -0.5)
    k = l2_normalize(split_heads(k, N_HEADS_B))
    v = split_heads(v, N_HEADS_B)
    beta = jax.nn.sigmoid(b_b.astype(F32))
    g = -jnp.exp(a_log.astype(F32)) * jax.nn.softplus(a_b.astype(F32) + dt_bias.astype(F32))
    t = q.shape[1]
    chunk = CHUNK_B if t % CHUNK_B == 0 else t
    o, s_new = gated_delta_chunked(q, k, v, g, beta, s0.astype(F32), chunk)
    o = rms_norm(o, out_norm) * jax.nn.silu(split_heads(gate_b, N_HEADS_B).astype(F32))
    return o.reshape(*o.shape[:2], WIDTH_B), s_new


def memory_kv(mem, mem_norm_g, w_mem_kv, k_norm_c):
    mk, mv = jnp.split(rms_norm(mem, mem_norm_g) @ w_mem_kv, 2, axis=-1)
    return rms_norm(split_heads(mk, N_HEADS_C), k_norm_c), split_heads(mv, N_HEADS_C)


def memory_attend(q, mk, mv):
    s = jnp.einsum('bthe,bmhe->bhtm', q, mk, preferred_element_type=F32) * (q.shape[-1] ** -0.5)
    p = jax.nn.softmax(s, axis=-1)
    return jnp.einsum('bhtm,bmhe->bthe', p, mv.astype(F32))


def merge_groups(x, oa, ob, oc, w_out):
    b, t = x.shape[:2]
    mix = jnp.concatenate([oa.reshape(b, t, WIDTH_A), ob, oc.reshape(b, t, WIDTH_C)], axis=-1)
    return x + mix.astype(x.dtype) @ w_out


def conv_ffn(up_ext, conv_w, w_down):
    g, u = jnp.split(causal_dwconv(up_ext, conv_w), 2, axis=-1)
    return (jax.nn.silu(g) * u) @ w_down


def setup_inputs(seed: int = 0) -> dict:
    key = jax.random.key(seed)
    keys = iter(jax.random.split(key, 40))

    def nrm(shape, scale=1.0):
        return jax.random.normal(next(keys), shape, F32) * scale

    def gain(shape):
        return 1.0 + nrm(shape, 0.05)

    L = DEPTH
    wbuf = min(MAX_WINDOW, PAST_LEN)
    a_log_b = jnp.log(jax.random.uniform(next(keys), (L, N_HEADS_B), F32, 1.0, 16.0))
    dt = jnp.exp(jax.random.uniform(next(keys), (L, N_HEADS_B), F32, math.log(1e-3), math.log(1e-1)))
    dt_bias_b = dt + jnp.log(-jnp.expm1(-dt))
    return {
        'x_prompt': nrm((BATCH, SEQ, D_MODEL)),
        'x_sample': nrm((DEC_BATCH, DEC_SEQ, D_MODEL)),
        'cache_win_k': nrm((L, DEC_BATCH, wbuf, N_HEADS_A, HEAD_DIM)),
        'cache_win_v': nrm((L, DEC_BATCH, wbuf, N_HEADS_A, HEAD_DIM)),
        'state_gdn': nrm((L, DEC_BATCH, N_HEADS_B, DK_B, DV_B), 0.5),
        'state_gdn_conv': nrm((L, DEC_BATCH, CONV_B - 1, QKV_B)),
        'state_ffn_conv': nrm((L, DEC_BATCH, CONV_FFN - 1, 2 * D_FF)),
        'cache_mem_k': nrm((L, DEC_BATCH, N_MEM, N_HEADS_C, HEAD_DIM)),
        'cache_mem_v': nrm((L, DEC_BATCH, N_MEM, N_HEADS_C, HEAD_DIM)),
        'mem_prompt': nrm((BATCH, N_MEM, D_MODEL)),
        'norm1_g': gain((L, D_MODEL)),
        'w_in': nrm((L, D_MODEL, N_IN), D_MODEL ** -0.5),
        'q_norm_a': gain((L, HEAD_DIM)),
        'k_norm_a': gain((L, HEAD_DIM)),
        'conv_b_w': nrm((L, CONV_B, QKV_B), CONV_B ** -0.5),
        'a_log_b': a_log_b,
        'dt_bias_b': dt_bias_b,
        'out_norm_b': gain((L, DV_B)),
        'mem_norm_g': gain((L, D_MODEL)),
        'w_mem_kv': nrm((L, D_MODEL, 2 * WIDTH_C), D_MODEL ** -0.5),
        'q_norm_c': gain((L, HEAD_DIM)),
        'k_norm_c': gain((L, HEAD_DIM)),
        'w_out': nrm((L, D_MIX, D_MODEL), D_MIX ** -0.5),
        'norm2_g': gain((L, D_MODEL)),
        'w_up': nrm((L, D_MODEL, 2 * D_FF), D_MODEL ** -0.5),
        'conv_ffn_w': nrm((L, CONV_FFN, 2 * D_FF), CONV_FFN ** -0.5),
        'w_down': nrm((L, D_FF, D_MODEL), D_FF ** -0.5),
    }


def reference(x_prompt, x_sample, cache_win_k, cache_win_v, state_gdn, state_gdn_conv, state_ffn_conv,
              cache_mem_k, cache_mem_v, mem_prompt, norm1_g, w_in, q_norm_a, k_norm_a, conv_b_w, a_log_b,
              dt_bias_b, out_norm_b, mem_norm_g, w_mem_kv, q_norm_c, k_norm_c, w_out, norm2_g, w_up,
              conv_ffn_w, w_down):
    slopes = alibi_slopes(N_HEADS_A)
    xp, xs = x_prompt, x_sample
    b_p, t_p = xp.shape[:2]
    (win_k_p, win_v_p, win_k_s, win_v_s, gdn_p, gdn_s, gconv_p, gconv_s,
     fconv_p, fconv_s, mem_k_p, mem_v_p) = ([] for _ in range(12))
    for l in range(DEPTH):
        qa, ka, va, qkv_b, gate_b, a_b, b_b, qc = mixer_inputs(xp, norm1_g[l], w_in[l], q_norm_a[l],
                                                              k_norm_a[l], q_norm_c[l])
        oa = combine_dilations([dilated_band_prompt(qa, ka, va, slopes, w, d) for w, d in PATTERNS])
        n_keep = min(MAX_WINDOW, t_p)
        win_k_p.append(ka[:, t_p - n_keep:])
        win_v_p.append(va[:, t_p - n_keep:])
        qkv_ext = jnp.pad(qkv_b, ((0, 0), (CONV_B - 1, 0), (0, 0)))
        gconv_p.append(qkv_ext[:, -(CONV_B - 1):])
        ob, s_fin = gated_delta_mixer(qkv_ext, gate_b, a_b, b_b, conv_b_w[l], a_log_b[l], dt_bias_b[l],
                                      out_norm_b[l], jnp.zeros((b_p, N_HEADS_B, DK_B, DV_B), F32))
        gdn_p.append(s_fin)
        mk, mv = memory_kv(mem_prompt, mem_norm_g[l], w_mem_kv[l], k_norm_c[l])
        mem_k_p.append(mk)
        mem_v_p.append(mv)
        oc = memory_attend(qc, mk, mv)
        xp = merge_groups(xp, oa, ob, oc, w_out[l])
        up = rms_norm(xp, norm2_g[l]) @ w_up[l]
        up_ext = jnp.pad(up, ((0, 0), (CONV_FFN - 1, 0), (0, 0)))
        fconv_p.append(up_ext[:, -(CONV_FFN - 1):])
        xp = xp + conv_ffn(up_ext, conv_ffn_w[l], w_down[l])

        qa, ka, va, qkv_b, gate_b, a_b, b_b, qc = mixer_inputs(xs, norm1_g[l], w_in[l], q_norm_a[l],
                                                              k_norm_a[l], q_norm_c[l])
        k_all = jnp.concatenate([cache_win_k[l].astype(ka.dtype), ka], axis=1)
        v_all = jnp.concatenate([cache_win_v[l].astype(va.dtype), va], axis=1)
        oa = combine_dilations([dilated_gather_sample(qa, k_all, v_all, slopes, w, d) for w, d in PATTERNS])
        win_k_s.append(ka)
        win_v_s.append(va)
        qkv_ext = jnp.concatenate([state_gdn_conv[l].astype(qkv_b.dtype), qkv_b], axis=1)
        gconv_s.append(qkv_ext[:, -(CONV_B - 1):])
        ob, s_new = gated_delta_mixer(qkv_ext, gate_b, a_b, b_b, conv_b_w[l], a_log_b[l], dt_bias_b[l],
                                      out_norm_b[l], state_gdn[l])
        gdn_s.append(s_new)
        oc = memory_attend(qc, cache_mem_k[l], cache_mem_v[l])
        xs = merge_groups(xs, oa, ob, oc, w_out[l])
        up = rms_norm(xs, norm2_g[l]) @ w_up[l]
        up_ext = jnp.concatenate([state_ffn_conv[l].astype(up.dtype), up], axis=1)
        fconv_s.append(up_ext[:, -(CONV_FFN - 1):])
        xs = xs + conv_ffn(up_ext, conv_ffn_w[l], w_down[l])
    return (xp, xs, jnp.stack(win_k_p), jnp.stack(win_v_p), jnp.stack(win_k_s), jnp.stack(win_v_s),
            jnp.stack(gdn_p), jnp.stack(gdn_s), jnp.stack(gconv_p), jnp.stack(gconv_s),
            jnp.stack(fconv_p), jnp.stack(fconv_s), jnp.stack(mem_k_p), jnp.stack(mem_v_p))
```

```python
import functools
import math

import numpy as np
import jax
import jax.numpy as jnp
from jax import lax
from jax.experimental import pallas as pl
from jax.experimental.pallas import tpu as pltpu

F32 = jnp.float32
BF16 = jnp.bfloat16
EPS = 1e-6
NEG = -1e30

D_MODEL = 1024
HEAD_DIM = 64
N_HEADS_A = 8
PATTERNS = ((128, 1), (512, 4), (2048, 16))
N_HEADS_B = 4
N_HEADS_C = 4
N_MEM = 256
D_FF = 2816
WIDTH_A = N_HEADS_A * HEAD_DIM
WIDTH_B = N_HEADS_B * HEAD_DIM
WIDTH_C = N_HEADS_C * HEAD_DIM
QKV_B = 3 * WIDTH_B
CHUNK = 64
NBAND = 128
ROW_TILE = 512
FF_TILE = 256
VMEM_LIMIT = 56 * 1024 * 1024

SEG_QA, SEG_KA, SEG_VA = (0, 512), (512, 1024), (1024, 1536)
SEG_QKV, SEG_GATE, SEG_QC, SEG_AB = (1536, 2304), (2304, 2560), (2560, 2816), (2816, 2944)
N_IN_PAD = 2944


def _params(*sem):
    return pltpu.CompilerParams(dimension_semantics=sem, vmem_limit_bytes=VMEM_LIMIT)


def _dot(a, b):
    return jnp.dot(a, b, preferred_element_type=F32)


def _dot_nt(a, b):
    return lax.dot_general(a, b, (((1,), (1,)), ((), ())), preferred_element_type=F32)


def _dot_tn(a, b):
    return lax.dot_general(a, b, (((0,), (0,)), ((), ())), preferred_element_type=F32)


def _dot_split(a, b, parts, left=False):
    acc = None
    rem = a
    for _ in range(parts):
        piece = rem.astype(BF16)
        term = _dot(b, piece) if left else _dot(piece, b)
        acc = term if acc is None else acc + term
        rem = rem - piece.astype(F32)
    return acc


def _sigmoid(x):
    return 1.0 / (1.0 + jnp.exp(-x))


def _silu(x):
    return x * _sigmoid(x)


def _softplus(x):
    return jnp.maximum(x, 0.0) + jnp.log1p(jnp.exp(-jnp.abs(x)))


def _group_ones(width, group=HEAD_DIM):
    i = np.arange(width)
    return jnp.asarray((i[:, None] // group) == (i[None, :] // group), BF16)


def _full(shape):
    nd = len(shape)
    return pl.BlockSpec(shape, lambda *_: (0,) * nd)


def _inproj_kernel(x_ref, g1_ref, w_ref, gq_ref, gk_ref, gc_ref, ga_ref, gcc_ref,
                   qa_ref, ka_ref, va_ref, qkv_ref, gate_ref, qc_ref, ab_ref):
    x = x_ref[...]
    ms = jnp.mean(x * x, axis=-1, keepdims=True)
    xn = (x * lax.rsqrt(ms + EPS) * g1_ref[...]).astype(BF16)

    def seg(s):
        return _dot(xn, w_ref[:, s[0]:s[1]])

    def head_norm(z, ones_ref, gain):
        ss = _dot_split(z * z, ones_ref[...], 2) * (1.0 / HEAD_DIM)
        return z * lax.rsqrt(ss + EPS) * gain

    scale = HEAD_DIM ** -0.5
    qa_ref[...] = (head_norm(seg(SEG_QA), ga_ref, gq_ref[...]) * scale).astype(BF16)
    ka_ref[...] = head_norm(seg(SEG_KA), ga_ref, gk_ref[...])
    va_ref[...] = seg(SEG_VA)
    qkv_ref[...] = seg(SEG_QKV)
    gate_ref[...] = seg(SEG_GATE)
    qc_ref[...] = (head_norm(seg(SEG_QC), gcc_ref, gc_ref[...]) * scale).astype(BF16)
    ab_ref[...] = seg(SEG_AB)


def _inproj(x, g1, w_in_p, gq, gk, gc):
    rows = x.shape[0]
    tm = ROW_TILE
    row = lambda w: pl.BlockSpec((tm, w), lambda i: (i, 0))
    outs = [(WIDTH_A, BF16), (WIDTH_A, F32), (WIDTH_A, F32), (QKV_B, F32), (WIDTH_B, F32), (WIDTH_C, BF16), (128, F32)]
    return pl.pallas_call(
        _inproj_kernel,
        grid=(rows // tm,),
        in_specs=[row(D_MODEL), _full((1, D_MODEL)), _full((D_MODEL, N_IN_PAD)), _full((1, WIDTH_A)),
                  _full((1, WIDTH_A)), _full((1, WIDTH_C)), _full((WIDTH_A, WIDTH_A)), _full((WIDTH_C, WIDTH_C))],
        out_specs=[row(w) for w, _ in outs],
        out_shape=[jax.ShapeDtypeStruct((rows, w), dt) for w, dt in outs],
        compiler_params=_params("parallel"),
        name="inproj",
    )(x, g1, w_in_p, gq, gk, gc, _group_ones(WIDTH_A), _group_ones(WIDTH_C))


def _alibi_slopes():
    return np.exp2(-8.0 * np.arange(1, N_HEADS_A + 1, dtype=np.float64) / N_HEADS_A)


def _band_bias(dil):
    qi = np.arange(NBAND)[:, None]
    kj = np.arange(2 * NBAND)[None, :]
    delta = qi + NBAND - kj
    in_band = (delta >= 0) & (delta <= NBAND)
    bias = -_alibi_slopes()[:, None, None] * (delta * dil)[None].astype(np.float64)
    general = np.where(in_band[None], bias, NEG)
    first = np.where((in_band & (kj >= NBAND))[None], bias, NEG)
    return jnp.asarray(np.stack([general, first]), F32)


def _attn_band_kernel(q_ref, kp_ref, kc_ref, vp_ref, vc_ref, bias_ref, num_ref, st_ref):
    first = (pl.program_id(1) == 0).astype(jnp.int32)
    q = q_ref[...]
    k = jnp.concatenate([kp_ref[...], kc_ref[...]], axis=0).astype(BF16)
    v = jnp.concatenate([vp_ref[...], vc_ref[...]], axis=0).astype(BF16)
    lane = lax.broadcasted_iota(jnp.int32, (NBAND, 128), 1)
    stats = jnp.zeros((NBAND, 128), F32)
    for h in range(N_HEADS_A):
        sl = slice(h * HEAD_DIM, (h + 1) * HEAD_DIM)
        s = _dot_nt(q[:, sl], k[:, sl]) + bias_ref[first, h]
        m = jnp.max(s, axis=-1, keepdims=True)
        p = jnp.exp(s - m)
        den = jnp.sum(p, axis=-1, keepdims=True)
        num_ref[:, sl] = _dot(p.astype(BF16), v[:, sl])
        stats = jnp.where(lane == h, den, stats)
        stats = jnp.where(lane == N_HEADS_A + h, m, stats)
    st_ref[...] = stats


def _attn_band(qa, ka, va, dil):
    t = qa.shape[0]
    rows = t // dil
    nblk = rows // NBAND
    view = lambda a: a.reshape(rows, dil * a.shape[1])
    cur = lambda w: pl.BlockSpec((NBAND, w), lambda r, n: (n, r))
    prev = lambda w: pl.BlockSpec((NBAND, w), lambda r, n: (jnp.maximum(n - 1, 0), r))
    num, st = pl.pallas_call(
        _attn_band_kernel,
        grid=(dil, nblk),
        in_specs=[cur(WIDTH_A), prev(WIDTH_A), cur(WIDTH_A), prev(WIDTH_A), cur(WIDTH_A),
                  _full((2, N_HEADS_A, NBAND, 2 * NBAND))],
        out_specs=[cur(WIDTH_A), cur(128)],
        out_shape=[jax.ShapeDtypeStruct((rows, dil * WIDTH_A), F32), jax.ShapeDtypeStruct((rows, dil * 128), F32)],
        compiler_params=_params("parallel", "arbitrary"),
        name=f"attn_band_d{dil}",
    )(view(qa), view(ka), view(ka), view(va), view(va), _band_bias(dil))
    return num.reshape(t, WIDTH_A), st.reshape(t, 128)


def _memkv_kernel(mem_ref, g_ref, w_ref, gk_ref, ones_ref, mk_ref, mv_ref):
    x = mem_ref[...]
    ms = jnp.mean(x * x, axis=-1, keepdims=True)
    xn = (x * lax.rsqrt(ms + EPS) * g_ref[...]).astype(BF16)
    zk = _dot(xn, w_ref[:, 0:WIDTH_C])
    ss = _dot_split(zk * zk, ones_ref[...], 2) * (1.0 / HEAD_DIM)
    mk_ref[...] = zk * lax.rsqrt(ss + EPS) * gk_ref[...]
    mv_ref[...] = _dot(xn, w_ref[:, WIDTH_C:2 * WIDTH_C])


def _memkv(mem, g, w, gk):
    n = mem.shape[0]
    return pl.pallas_call(
        _memkv_kernel,
        out_shape=[jax.ShapeDtypeStruct((n, WIDTH_C), F32)] * 2,
        compiler_params=pltpu.CompilerParams(vmem_limit_bytes=VMEM_LIMIT),
        name="memkv",
    )(mem, g, w, gk, _group_ones(WIDTH_C))


def _combine_mem_kernel(n1_ref, n2_ref, n3_ref, s1_ref, s2_ref, s3_ref, qc_ref, mk_ref, mv_ref, oa_ref, oc_ref):
    nums = (n1_ref, n2_ref, n3_ref)
    stats = [r[...] for r in (s1_ref, s2_ref, s3_ref)]
    dens = [s[:, 0:N_HEADS_A] for s in stats]
    mxs = [s[:, N_HEADS_A:2 * N_HEADS_A] for s in stats]
    m = jnp.maximum(jnp.maximum(mxs[0], mxs[1]), mxs[2])
    es = [jnp.exp(mx - m) for mx in mxs]
    den = dens[0] * es[0] + dens[1] * es[1] + dens[2] * es[2]
    ws = [e / den for e in es]
    for h in range(N_HEADS_A):
        sl = slice(h * HEAD_DIM, (h + 1) * HEAD_DIM)
        acc = nums[0][:, sl] * ws[0][:, h:h + 1]
        acc = acc + nums[1][:, sl] * ws[1][:, h:h + 1]
        acc = acc + nums[2][:, sl] * ws[2][:, h:h + 1]
        oa_ref[:, sl] = acc.astype(BF16)
    q = qc_ref[...]
    mk = mk_ref[...].astype(BF16)
    mv = mv_ref[...].astype(BF16)
    for h in range(N_HEADS_C):
        sl = slice(h * HEAD_DIM, (h + 1) * HEAD_DIM)
        s = _dot_nt(q[:, sl], mk[:, sl])
        mx = jnp.max(s, axis=-1, keepdims=True)
        p = jnp.exp(s - mx)
        den_c = jnp.sum(p, axis=-1, keepdims=True)
        oc_ref[:, sl] = (_dot(p.astype(BF16), mv[:, sl]) / den_c).astype(BF16)


def _combine_mem(nums, stats, qc, mk, mv):
    t = qc.shape[0]
    tm = ROW_TILE
    row = lambda w: pl.BlockSpec((tm, w), lambda i: (i, 0))
    return pl.pallas_call(
        _combine_mem_kernel,
        grid=(t // tm,),
        in_specs=[row(WIDTH_A)] * 3 + [row(128)] * 3 + [row(WIDTH_C), _full((N_MEM, WIDTH_C)), _full((N_MEM, WIDTH_C))],
        out_specs=[row(WIDTH_A), row(WIDTH_C)],
        out_shape=[jax.ShapeDtypeStruct((t, WIDTH_A), BF16), jax.ShapeDtypeStruct((t, WIDTH_C), BF16)],
        compiler_params=_params("parallel"),
        name="combine_mem",
    )(*nums, *stats, qc, mk, mv)


def _head_lane_consts():
    e_g = np.zeros((128, WIDTH_B), np.float32)
    e_b = np.zeros((128, WIDTH_B), np.float32)
    for h in range(N_HEADS_B):
        e_g[h, h * HEAD_DIM:(h + 1) * HEAD_DIM] = 1.0
        e_b[N_HEADS_B + h, h * HEAD_DIM:(h + 1) * HEAD_DIM] = 1.0
    i = np.arange(CHUNK)
    ltri = (i[None, :] <= i[:, None]).astype(np.float32)
    return jnp.asarray(e_g, BF16), jnp.asarray(e_b, BF16), jnp.asarray(ltri, BF16), jnp.ones((CHUNK, CHUNK), BF16)


def _block_diag(x, mask):
    return jnp.where(mask, jnp.concatenate([x] * N_HEADS_B, axis=0), jnp.zeros((), x.dtype))


def _gdn_prep_kernel(x_ref, halo_ref, ab_ref, cw_ref, alog_ref, dtb_ref, ones_ref, eg_ref, eb_ref, ltri_ref, one64_ref,
                     m_ref, attn_ref, qd_ref, kd_ref, vb_ref, kbe_ref, egl_ref, xe_ref):
    tm = x_ref.shape[0]
    halo = jnp.where(pl.program_id(0) > 0, halo_ref[...], 0.0)
    xe_ref[0:8, :] = halo
    xe_ref[8:8 + tm, :] = x_ref[...]
    cw = cw_ref[...]
    y = cw[3:4] * xe_ref[8:8 + tm, :] + cw[2:3] * xe_ref[7:7 + tm, :] + cw[1:2] * xe_ref[6:6 + tm, :] \
        + cw[0:1] * xe_ref[5:5 + tm, :]
    y = _silu(y)

    def l2n(z):
        return z * lax.rsqrt(_dot_split(z * z, ones_ref[...], 2) + EPS)

    q = l2n(y[:, 0:WIDTH_B]) * (HEAD_DIM ** -0.5)
    k = l2n(y[:, WIDTH_B:2 * WIDTH_B])
    v = y[:, 2 * WIDTH_B:3 * WIDTH_B]
    ab = ab_ref[...]
    g = -jnp.exp(alog_ref[...]) * _softplus(ab + dtb_ref[...])
    gx = _dot_split(g, eg_ref[...], 3)
    bx = _dot_split(_sigmoid(ab), eb_ref[...], 2)

    row = lax.broadcasted_iota(jnp.int32, (CHUNK, WIDTH_B), 0)
    col = lax.broadcasted_iota(jnp.int32, (CHUNK, WIDTH_B), 1) % CHUNK
    diag = row == col
    r4 = lax.broadcasted_iota(jnp.int32, (WIDTH_B, WIDTH_B), 0) // HEAD_DIM
    c4 = lax.broadcasted_iota(jnp.int32, (WIDTH_B, WIDTH_B), 1) // HEAD_DIM
    bd_mask = r4 == c4
    for c in range(tm // CHUNK):
        sl = slice(c * CHUNK, (c + 1) * CHUNK)
        gc = _dot_split(gx[sl], ltri_ref[...], 3, left=True)
        gl = gc[CHUNK - 1:CHUNK, :]
        gcj = _dot_split(jnp.where(diag, gc, 0.0), one64_ref[...], 3, left=True)
        dec = jnp.exp(jnp.where(row >= col, gc - gcj, NEG))
        kc, qc, vc, bc = k[sl], q[sl], v[sl], bx[sl]
        kb = kc * bc
        lhs = jnp.concatenate([kb, qc], axis=0).astype(BF16)
        kkqk = _dot_nt(lhs, _block_diag(kc.astype(BF16), bd_mask))
        m_ref[sl, :] = jnp.where(row > col, kkqk[0:CHUNK] * dec, 0.0)
        attn_ref[sl, :] = (kkqk[CHUNK:2 * CHUNK] * dec).astype(BF16)
        egc = jnp.exp(gc)
        qd_ref[sl, :] = (qc * egc).astype(BF16)
        kd_ref[sl, :] = (kc * jnp.exp(gl - gc)).astype(BF16)
        vb_ref[sl, :] = (vc * bc).astype(BF16)
        kbe_ref[sl, :] = (kb * egc).astype(BF16)
        egl_ref[c:c + 1, :] = jnp.exp(gl)


def _gdn_prep(qkv, ab, conv_w, alog_row, dtb_row):
    t = qkv.shape[0]
    tm = ROW_TILE
    nc = tm // CHUNK
    row = lambda w: pl.BlockSpec((tm, w), lambda i: (i, 0))
    e_g, e_b, ltri, one64 = _head_lane_consts()
    outs = [F32, BF16, BF16, BF16, BF16, BF16]
    return pl.pallas_call(
        _gdn_prep_kernel,
        grid=(t // tm,),
        in_specs=[row(QKV_B), pl.BlockSpec((8, QKV_B), lambda i: (jnp.maximum(i * (tm // 8) - 1, 0), 0)), row(128),
                  _full((4, QKV_B)), _full((1, 128)), _full((1, 128)), _full((WIDTH_B, WIDTH_B)),
                  _full((128, WIDTH_B)), _full((128, WIDTH_B)), _full((CHUNK, CHUNK)), _full((CHUNK, CHUNK))],
        out_specs=[row(WIDTH_B)] * 6 + [pl.BlockSpec((nc, WIDTH_B), lambda i: (i, 0))],
        out_shape=[jax.ShapeDtypeStruct((t, WIDTH_B), dt) for dt in outs]
        + [jax.ShapeDtypeStruct((t // CHUNK, WIDTH_B), F32)],
        scratch_shapes=[pltpu.VMEM((tm + 8, QKV_B), F32)],
        compiler_params=_params("parallel"),
        name="gdn_prep",
    )(qkv, qkv, ab, conv_w, alog_row, dtb_row, _group_ones(WIDTH_B), e_g, e_b, ltri, one64)


def _tinv_kernel(m_ref, out_ref, n_ref):
    blk = pl.program_id(0)
    for ii in range(8):
        i = blk * 8 + ii
        for kb in range(8):
            k0 = 8 * kb
            init = tuple(m_ref[ii, k0 + kk] for kk in range(8))

            def body(j, acc, ii=ii, k0=k0):
                mij = m_ref[ii, j]
                return tuple(acc[kk] + mij * n_ref[j, k0 + kk] for kk in range(8))

            acc = lax.fori_loop(k0, jnp.maximum(i, k0), body, init)
            for kk in range(8):
                val = jnp.where(k0 + kk < i, -acc[kk], 0.0)
                n_ref[i, k0 + kk] = val
                out_ref[ii, k0 + kk] = val


def _tinv(mt):
    return pl.pallas_call(
        _tinv_kernel,
        grid=(CHUNK // 8,),
        in_specs=[pl.BlockSpec((8, CHUNK, 8, 128), lambda i: (i, 0, 0, 0))],
        out_specs=pl.BlockSpec((8, CHUNK, 8, 128), lambda i: (i, 0, 0, 0)),
        out_shape=jax.ShapeDtypeStruct(mt.shape, F32),
        scratch_shapes=[pltpu.VMEM(mt.shape, F32)],
        compiler_params=_params("arbitrary"),
        name="gdn_tinv",
    )(mt)


def _gdn_scan_kernel(n_ref, vb_ref, kbe_ref, attn_ref, qd_ref, kd_ref, egl_ref, gate_ref, gn_ref, ones_ref,
                     ob_ref, s_out_ref, s_ref):
    tm = n_ref.shape[0]

    @pl.when(pl.program_id(0) == 0)
    def _():
        s_ref[...] = jnp.zeros_like(s_ref)

    row = lax.broadcasted_iota(jnp.int32, (CHUNK, WIDTH_B), 0)
    col = lax.broadcasted_iota(jnp.int32, (CHUNK, WIDTH_B), 1) % CHUNK
    eye = (row == col).astype(F32)
    r4 = lax.broadcasted_iota(jnp.int32, (WIDTH_B, WIDTH_B), 0) // HEAD_DIM
    c4 = lax.broadcasted_iota(jnp.int32, (WIDTH_B, WIDTH_B), 1) // HEAD_DIM
    bd_mask = r4 == c4
    s = s_ref[...]
    for c in range(tm // CHUNK):
        sl = slice(c * CHUNK, (c + 1) * CHUNK)
        tinv = (n_ref[sl, :] + eye).astype(BF16)
        u = _dot(tinv, _block_diag(vb_ref[sl, :], bd_mask))
        w = _dot(tinv, _block_diag(kbe_ref[sl, :], bd_mask))
        lhs = jnp.concatenate([w.astype(BF16), qd_ref[sl, :]], axis=0)
        ws = _dot(lhs, s.astype(BF16))
        v_new = (u - ws[0:CHUNK]).astype(BF16)
        o = ws[CHUNK:2 * CHUNK] + _dot(attn_ref[sl, :], _block_diag(v_new, bd_mask))
        s = s * egl_ref[c:c + 1, :] + jnp.where(bd_mask, _dot_tn(kd_ref[sl, :], v_new), 0.0)
        ms = _dot_split(o * o, ones_ref[...], 2) * (1.0 / HEAD_DIM)
        ob_ref[sl, :] = (o * lax.rsqrt(ms + EPS) * gn_ref[...] * _silu(gate_ref[sl, :])).astype(BF16)
    s_ref[...] = s
    s_out_ref[...] = s


def _gdn_scan(n_cat, vb, kbe, attn, qd, kd, egl, gate, gn_row):
    t = n_cat.shape[0]
    tm = ROW_TILE
    nc = tm // CHUNK
    row = lambda w: pl.BlockSpec((tm, w), lambda i: (i, 0))
    return pl.pallas_call(
        _gdn_scan_kernel,
        grid=(t // tm,),
        in_specs=[row(WIDTH_B)] * 6 + [pl.BlockSpec((nc, WIDTH_B), lambda i: (i, 0)), row(WIDTH_B),
                                       _full((1, WIDTH_B)), _full((WIDTH_B, WIDTH_B))],
        out_specs=[row(WIDTH_B), _full((WIDTH_B, WIDTH_B))],
        out_shape=[jax.ShapeDtypeStruct((t, WIDTH_B), BF16), jax.ShapeDtypeStruct((WIDTH_B, WIDTH_B), F32)],
        scratch_shapes=[pltpu.VMEM((WIDTH_B, WIDTH_B), F32)],
        compiler_params=_params("arbitrary"),
        name="gdn_scan",
    )(n_cat, vb, kbe, attn, qd, kd, egl, gate, gn_row, _group_ones(WIDTH_B))


def _gdn_prompt(qkv, ab, gate, conv_w, alog_row, dtb_row, gn_row):
    t = qkv.shape[0]
    nchunk = t // CHUNK
    nsys = nchunk * N_HEADS_B
    nsys_pad = -(-nsys // 1024) * 1024
    assert nsys_pad == 1024
    m, attn, qd, kd, vb, kbe, egl = _gdn_prep(qkv, ab, conv_w, alog_row, dtb_row)
    mt = m.reshape(nchunk, CHUNK, N_HEADS_B, CHUNK).transpose(1, 3, 0, 2).reshape(CHUNK, CHUNK, nsys)
    mt = jnp.pad(mt, ((0, 0), (0, 0), (0, nsys_pad - nsys))).reshape(CHUNK, CHUNK, 8, 128)
    nt = _tinv(mt).reshape(CHUNK, CHUNK, nsys_pad)[:, :, :nsys]
    n_cat = nt.reshape(CHUNK, CHUNK, nchunk, N_HEADS_B).transpose(2, 0, 3, 1).reshape(t, WIDTH_B)
    ob, s_bd = _gdn_scan(n_cat, vb, kbe, attn, qd, kd, egl, gate, gn_row)
    s_fin = jnp.stack([s_bd[h * HEAD_DIM:(h + 1) * HEAD_DIM, h * HEAD_DIM:(h + 1) * HEAD_DIM] for h in range(N_HEADS_B)])
    return ob, s_fin


def _outproj_kernel(x_ref, oa_ref, ob_ref, oc_ref, w_ref, y_ref):
    acc = _dot(oa_ref[...], w_ref[0:WIDTH_A, :])
    acc = acc + _dot(ob_ref[...], w_ref[WIDTH_A:WIDTH_A + WIDTH_B, :])
    acc = acc + _dot(oc_ref[...], w_ref[WIDTH_A + WIDTH_B:, :])
    y_ref[...] = x_ref[...] + acc


def _outproj(x, oa, ob, oc, w_out):
    rows = x.shape[0]
    tm = ROW_TILE
    row = lambda w: pl.BlockSpec((tm, w), lambda i: (i, 0))
    return pl.pallas_call(
        _outproj_kernel,
        grid=(rows // tm,),
        in_specs=[row(D_MODEL), row(WIDTH_A), row(WIDTH_B), row(WIDTH_C), _full((D_MODEL, D_MODEL))],
        out_specs=row(D_MODEL),
        out_shape=jax.ShapeDtypeStruct((rows, D_MODEL), F32),
        compiler_params=_params("parallel"),
        name="outproj",
    )(x, oa, ob, oc, w_out)


def _ffn_kernel(shift, carry_rows, x_ref, g_ref, wup_ref, cw_ref, wdn_ref, c0_ref, y_ref, last_ref, carry_ref, ext_ref):
    tm = x_ref.shape[0]

    @pl.when(pl.program_id(0) == 0)
    def _():
        carry_ref[...] = c0_ref[...]

    x = x_ref[...]
    ms = jnp.mean(x * x, axis=-1, keepdims=True)
    xn = (x * lax.rsqrt(ms + EPS) * g_ref[...]).astype(BF16)
    y_ref[...] = x
    tf = FF_TILE
    base = carry_rows
    for j in range(D_FF // tf):
        halves = []
        for half in range(2):
            cols = slice(half * D_FF + j * tf, half * D_FF + (j + 1) * tf)
            up = _dot(xn, wup_ref[:, cols])
            ext_ref[0:base, :] = carry_ref[:, cols]
            ext_ref[base:base + tm, :] = up
            carry_ref[:, cols] = ext_ref[tm:tm + base, :]
            cw = cw_ref[:, cols]
            halves.append(cw[0:1] * ext_ref[base - 2 * shift:base - 2 * shift + tm, :]
                          + cw[1:2] * ext_ref[base - shift:base - shift + tm, :] + cw[2:3] * up)
        hidden = (_silu(halves[0]) * halves[1]).astype(BF16)
        y_ref[...] += _dot(hidden, wdn_ref[j * tf:(j + 1) * tf, :])
    last_ref[...] = carry_ref[...]


def _ffn(x, g, w_up, conv_w, w_down, carry0, shift):
    rows = x.shape[0]
    tm = ROW_TILE
    carry_rows = carry0.shape[0]
    row = lambda w: pl.BlockSpec((tm, w), lambda i: (i, 0))
    once = lambda shape: pl.BlockSpec(shape, lambda i: (0,) * len(shape), pipeline_mode=pl.Buffered(1))
    return pl.pallas_call(
        functools.partial(_ffn_kernel, shift, carry_rows),
        grid=(rows // tm,),
        in_specs=[row(D_MODEL), once((1, D_MODEL)), once((D_MODEL, 2 * D_FF)), once((3, 2 * D_FF)),
                  once((D_FF, D_MODEL)), once((carry_rows, 2 * D_FF))],
        out_specs=[row(D_MODEL), _full((carry_rows, 2 * D_FF))],
        out_shape=[jax.ShapeDtypeStruct((rows, D_MODEL), F32), jax.ShapeDtypeStruct((carry_rows, 2 * D_FF), F32)],
        scratch_shapes=[pltpu.VMEM((carry_rows, 2 * D_FF), F32), pltpu.VMEM((tm + carry_rows, FF_TILE), F32)],
        compiler_params=_params("arbitrary"),
        name=f"ffn_shift{shift}",
    )(x, g, w_up, conv_w, w_down, carry0)


N_STRIDED = 96
DENSE_START = 16 * N_STRIDED
SAMPLE_GROUP = 2


def _sample_bias(wbuf, s_len):
    def table(key_pos):
        q_pos = wbuf + np.arange(s_len)
        dist = q_pos[:, None] - key_pos[None, :]
        mult = np.zeros(dist.shape)
        for window, dil in PATTERNS:
            mult += (dist >= 0) & (dist <= window) & (dist % dil == 0)
        with np.errstate(divide="ignore"):
            logm = np.where(mult > 0, np.log(np.maximum(mult, 1)), NEG)
        bias = -_alibi_slopes()[:, None, None] * np.maximum(dist, 0)[None] + logm[None]
        return np.where(mult[None] > 0, bias, NEG).reshape(N_HEADS_A * s_len, -1)
    strided = np.stack([table(16 * np.arange(N_STRIDED) + c) for c in range(4)])
    dense = table(np.arange(DENSE_START, wbuf))
    new = table(np.concatenate([wbuf + np.arange(s_len), np.full(8 - s_len, 10 ** 6)]))
    return jnp.asarray(strided, F32), jnp.asarray(dense, F32), jnp.asarray(new, F32)


def _sample_attn_kernel(s_len, q_ref, kn_ref, vn_ref, ks_ref, vs_ref, kd_ref, vd_ref, bs_ref, bd_ref, bn_ref,
                        qc_ref, mk_ref, mv_ref, oa_ref, oc_ref):
    def head_mask(n_heads):
        width = n_heads * HEAD_DIM
        r = lax.broadcasted_iota(jnp.int32, (n_heads * s_len, width), 0) // s_len
        c = lax.broadcasted_iota(jnp.int32, (n_heads * s_len, width), 1) // HEAD_DIM
        return r == c

    def fold_heads(o, sel, n_heads):
        o = jnp.where(sel, o, 0.0)
        acc = o[0:s_len]
        for h in range(1, n_heads):
            acc = acc + o[h * s_len:(h + 1) * s_len]
        return acc

    sel_a = head_mask(N_HEADS_A)
    sel_c = head_mask(N_HEADS_C)
    for b in range(q_ref.shape[0]):
        qh = jnp.where(sel_a, q_ref[b], jnp.zeros((), BF16))
        blocks = [(kn_ref[b], vn_ref[b], bn_ref[...]), (kd_ref[b], vd_ref[b], bd_ref[...])]
        for c in range(4):
            cols = slice(c * WIDTH_A, (c + 1) * WIDTH_A)
            blocks.append((ks_ref[b, :, cols], vs_ref[b, :, cols], bs_ref[c]))
        scores = [_dot_nt(qh, kk.astype(BF16)) + bias for kk, _, bias in blocks]
        m = functools.reduce(jnp.maximum, [jnp.max(s, axis=-1, keepdims=True) for s in scores])
        den = 0.0
        num = 0.0
        for s, (_, vv, _) in zip(scores, blocks):
            p = jnp.exp(s - m)
            den = den + jnp.sum(p, axis=-1, keepdims=True)
            num = num + _dot(p.astype(BF16), vv.astype(BF16))
        oa_ref[b] = fold_heads(num / den, sel_a, N_HEADS_A)

        qm = jnp.where(sel_c, qc_ref[b], jnp.zeros((), BF16))
        s = _dot_nt(qm, mk_ref[b].astype(BF16))
        mx = jnp.max(s, axis=-1, keepdims=True)
        p = jnp.exp(s - mx)
        o = _dot(p.astype(BF16), mv_ref[b].astype(BF16)) / jnp.sum(p, axis=-1, keepdims=True)
        oc_ref[b] = fold_heads(o, sel_c, N_HEADS_C)


def _sample_attn(qa, ka, va, cache_k, cache_v, qc, mem_k, mem_v):
    nb, s_len, _ = qa.shape
    wbuf = cache_k.shape[1]
    assert wbuf == 2048 and s_len <= 4
    g = SAMPLE_GROUP
    bs, bd, bn = _sample_bias(wbuf, s_len)
    strided = lambda a: a.reshape(nb, wbuf // 16, 16 * WIDTH_A)
    pad8 = lambda a: jnp.pad(a, ((0, 0), (0, 8 - s_len), (0, 0)))
    per_b = lambda shape: pl.BlockSpec((g,) + shape, lambda i: (i, 0, 0))
    spec_strided = pl.BlockSpec((g, N_STRIDED, 4 * WIDTH_A), lambda i: (i, 0, 0))
    n_dense = wbuf - DENSE_START
    spec_dense = pl.BlockSpec((g, n_dense, WIDTH_A), lambda i: (i, DENSE_START // n_dense, 0))
    oa, oc = pl.pallas_call(
        functools.partial(_sample_attn_kernel, s_len),
        grid=(nb // g,),
        in_specs=[per_b((N_HEADS_A * s_len, WIDTH_A)), per_b((8, WIDTH_A)), per_b((8, WIDTH_A)),
                  spec_strided, spec_strided, spec_dense, spec_dense,
                  _full(bs.shape), _full(bd.shape), _full(bn.shape),
                  per_b((N_HEADS_C * s_len, WIDTH_C)), per_b((N_MEM, WIDTH_C)), per_b((N_MEM, WIDTH_C))],
        out_specs=[per_b((s_len, WIDTH_A)), per_b((s_len, WIDTH_C))],
        out_shape=[jax.ShapeDtypeStruct((nb, s_len, WIDTH_A), F32), jax.ShapeDtypeStruct((nb, s_len, WIDTH_C), F32)],
        compiler_params=_params("parallel"),
        name="sample_attn",
    )(jnp.tile(qa, (1, N_HEADS_A, 1)), pad8(ka), pad8(va), strided(cache_k), strided(cache_v), cache_k, cache_v,
      bs, bd, bn, jnp.tile(qc, (1, N_HEADS_C, 1)), mem_k, mem_v)
    return oa.astype(BF16), oc.astype(BF16)


def _sample_gdn_kernel(xe_ref, cw_ref, a_ref, b_ref, alog_ref, dtb_ref, gate_ref, gn_ref, s_ref, o_ref, s_out_ref,
                       q_scr, k_scr, v_scr):
    s_len = o_ref.shape[0]
    s_out_ref[...] = s_ref[...]
    for t in range(s_len):
        for part, scr in enumerate((q_scr, k_scr, v_scr)):
            y = sum(cw_ref[j, part] * xe_ref[t + j, part] for j in range(4))
            y = _silu(y)
            if part < 2:
                y = y * lax.rsqrt(jnp.sum(y * y, axis=0, keepdims=True) + EPS)
            if part == 0:
                y = y * (HEAD_DIM ** -0.5)
            scr[...] = y
        decay = jnp.exp(-jnp.exp(alog_ref[...]) * _softplus(a_ref[t] + dtb_ref[...]))
        beta = _sigmoid(b_ref[t])

        def read_body(dk, r):
            return r + k_scr[pl.ds(dk, 1), :] * s_out_ref[dk]

        r = lax.fori_loop(0, HEAD_DIM, read_body, jnp.zeros(v_scr.shape, F32), unroll=8)
        v_new = beta * (v_scr[...] - decay * r)

        def write_body(dk, o):
            s_new = decay * s_out_ref[dk] + k_scr[pl.ds(dk, 1), :] * v_new
            s_out_ref[dk] = s_new
            return o + q_scr[pl.ds(dk, 1), :] * s_new

        o = lax.fori_loop(0, HEAD_DIM, write_body, jnp.zeros(v_scr.shape, F32), unroll=8)
        ms = jnp.mean(o * o, axis=0, keepdims=True)
        o_ref[t] = o * lax.rsqrt(ms + EPS) * gn_ref[...] * _silu(gate_ref[t])


def _sample_gdn(xe, conv_w, a_b, b_b, a_log, dt_bias, gate, out_norm, state):
    nb, ext, _ = xe.shape
    s_len = ext - 3
    nh, hd = N_HEADS_B, HEAD_DIM
    xe_t = xe.reshape(nb, ext, 3, nh, hd).transpose(1, 2, 3, 4, 0)
    cw_t = jnp.broadcast_to(conv_w.reshape(4, 3, nh, hd)[..., None], (4, 3, nh, hd, nb))
    a_t = a_b.transpose(1, 2, 0).reshape(s_len, nh, 1, nb)
    b_t = b_b.transpose(1, 2, 0).reshape(s_len, nh, 1, nb)
    alog_t = jnp.broadcast_to(a_log.reshape(nh, 1, 1), (nh, 1, nb))
    dtb_t = jnp.broadcast_to(dt_bias.reshape(nh, 1, 1), (nh, 1, nb))
    gate_t = gate.reshape(nb, s_len, nh, hd).transpose(1, 2, 3, 0)
    gn_t = jnp.broadcast_to(out_norm.reshape(1, hd, 1), (1, hd, nb))
    s_t = state.transpose(1, 2, 3, 0)
    o_t, s_new = pl.pallas_call(
        _sample_gdn_kernel,
        grid=(nh,),
        in_specs=[pl.BlockSpec((ext, 3, None, hd, nb), lambda h: (0, 0, h, 0, 0)),
                  pl.BlockSpec((4, 3, None, hd, nb), lambda h: (0, 0, h, 0, 0)),
                  pl.BlockSpec((s_len, None, 1, nb), lambda h: (0, h, 0, 0)),
                  pl.BlockSpec((s_len, None, 1, nb), lambda h: (0, h, 0, 0)),
                  pl.BlockSpec((None, 1, nb), lambda h: (h, 0, 0)),
                  pl.BlockSpec((None, 1, nb), lambda h: (h, 0, 0)),
                  pl.BlockSpec((s_len, None, hd, nb), lambda h: (0, h, 0, 0)),
                  pl.BlockSpec((None, hd, nb), lambda h: (0, 0, 0)),
                  pl.BlockSpec((None, hd, hd, nb), lambda h: (h, 0, 0, 0))],
        out_specs=[pl.BlockSpec((s_len, None, hd, nb), lambda h: (0, h, 0, 0)),
                   pl.BlockSpec((None, hd, hd, nb), lambda h: (h, 0, 0, 0))],
        out_shape=[jax.ShapeDtypeStruct((s_len, nh, hd, nb), F32), jax.ShapeDtypeStruct((nh, hd, hd, nb), F32)],
        scratch_shapes=[pltpu.VMEM((hd, nb), F32)] * 3,
        compiler_params=_params("parallel"),
        name="sample_gdn",
    )(xe_t, cw_t, a_t, b_t, alog_t, dtb_t, gate_t, gn_t, s_t)
    ob = o_t.transpose(3, 0, 1, 2).reshape(nb * s_len, nh * hd).astype(BF16)
    return ob, s_new.transpose(3, 0, 1, 2)


def _permute_w_in(w_in):
    a_start = 3 * WIDTH_A + QKV_B + WIDTH_B
    qc_start = a_start + 2 * N_HEADS_B
    pad = jnp.zeros((w_in.shape[0], N_IN_PAD - w_in.shape[1]), w_in.dtype)
    return jnp.concatenate([w_in[:, :a_start], w_in[:, qc_start:], w_in[:, a_start:qc_start], pad], axis=1).astype(BF16)


def _lane_row(v, width=128):
    return jnp.zeros((1, width), F32).at[0, :v.shape[0]].set(v)


def kernel(x_prompt, x_sample, cache_win_k, cache_win_v, state_gdn, state_gdn_conv, state_ffn_conv, cache_mem_k,
           cache_mem_v, mem_prompt, norm1_g, w_in, q_norm_a, k_norm_a, conv_b_w, a_log_b, dt_bias_b, out_norm_b,
           mem_norm_g, w_mem_kv, q_norm_c, k_norm_c, w_out, norm2_g, w_up, conv_ffn_w, w_down):
    depth = norm1_g.shape[0]
    assert depth == 1 and x_prompt.shape[0] == 1
    l = 0
    t_p = x_prompt.shape[1]
    nb, s_len = x_sample.shape[:2]
    xp = x_prompt.reshape(t_p, D_MODEL)
    xs = x_sample.reshape(nb * s_len, D_MODEL)

    w_in_p = _permute_w_in(w_in[l])
    w_out_b = w_out[l].astype(BF16)
    w_up_b = w_up[l].astype(BF16)
    w_down_b = w_down[l].astype(BF16)
    g1 = norm1_g[l].reshape(1, D_MODEL)
    g2 = norm2_g[l].reshape(1, D_MODEL)
    gq = jnp.tile(q_norm_a[l], N_HEADS_A).reshape(1, WIDTH_A)
    gk = jnp.tile(k_norm_a[l], N_HEADS_A).reshape(1, WIDTH_A)
    gqc = jnp.tile(q_norm_c[l], N_HEADS_C).reshape(1, WIDTH_C)
    gkc = jnp.tile(k_norm_c[l], N_HEADS_C).reshape(1, WIDTH_C)
    gn_row = jnp.tile(out_norm_b[l], N_HEADS_B).reshape(1, WIDTH_B)

    qa, ka, va, qkv, gate, qc, ab = _inproj(xp, g1, w_in_p, gq, gk, gqc)
    parts = [_attn_band(qa, ka, va, dil) for _, dil in PATTERNS]
    mk, mv = _memkv(mem_prompt[0], mem_norm_g[l].reshape(1, D_MODEL), w_mem_kv[l].astype(BF16), gkc)
    oa, oc = _combine_mem([p[0] for p in parts], [p[1] for p in parts], qc, mk, mv)
    ob, gdn_p = _gdn_prompt(qkv, ab, gate, conv_b_w[l], _lane_row(a_log_b[l]), _lane_row(dt_bias_b[l]), gn_row)
    x1 = _outproj(xp, oa, ob, oc, w_out_b)
    y_p, last_p = _ffn(x1, g2, w_up_b, conv_ffn_w[l], w_down_b, jnp.zeros((8, 2 * D_FF), F32), 1)
    n_keep = min(2048, t_p)
    win_k_p = ka[t_p - n_keep:].reshape(1, 1, n_keep, N_HEADS_A, HEAD_DIM)
    win_v_p = va[t_p - n_keep:].reshape(1, 1, n_keep, N_HEADS_A, HEAD_DIM)
    gconv_p = qkv[t_p - 3:].reshape(1, 1, 3, QKV_B)
    fconv_p = last_p[6:8].reshape(1, 1, 2, 2 * D_FF)

    qa_s, ka_s, va_s, qkv_s, gate_s, qc_s, ab_s = _inproj(xs, g1, w_in_p, gq, gk, gqc)
    b3 = lambda a: a.reshape(nb, s_len, a.shape[-1])
    oa_s, oc_s = _sample_attn(b3(qa_s), b3(ka_s), b3(va_s), cache_win_k[l].reshape(nb, -1, WIDTH_A),
                              cache_win_v[l].reshape(nb, -1, WIDTH_A), b3(qc_s),
                              cache_mem_k[l].reshape(nb, N_MEM, WIDTH_C), cache_mem_v[l].reshape(nb, N_MEM, WIDTH_C))
    xe_s = jnp.concatenate([state_gdn_conv[l], b3(qkv_s)], axis=1)
    ab3 = b3(ab_s)
    ob_s, gdn_s = _sample_gdn(xe_s, conv_b_w[l], ab3[..., 0:N_HEADS_B], ab3[..., N_HEADS_B:2 * N_HEADS_B], a_log_b[l],
                              dt_bias_b[l], b3(gate_s), out_norm_b[l], state_gdn[l])
    x1_s = _outproj(xs, oa_s.reshape(nb * s_len, WIDTH_A), ob_s, oc_s.reshape(nb * s_len, WIDTH_C), w_out_b)
    x1_t = x1_s.reshape(nb, s_len, D_MODEL).transpose(1, 0, 2).reshape(s_len * nb, D_MODEL)
    carry_s = state_ffn_conv[l].transpose(1, 0, 2).reshape(2 * nb, 2 * D_FF)
    y_t, last_s = _ffn(x1_t, g2, w_up_b, conv_ffn_w[l], w_down_b, carry_s, nb)
    y_s = y_t.reshape(s_len, nb, D_MODEL).transpose(1, 0, 2)
    fconv_s = last_s.reshape(2, nb, 2 * D_FF).transpose(1, 0, 2)[None]

    return (y_p.reshape(1, t_p, D_MODEL), y_s,
            win_k_p, win_v_p,
            ka_s.reshape(1, nb, s_len, N_HEADS_A, HEAD_DIM), va_s.reshape(1, nb, s_len, N_HEADS_A, HEAD_DIM),
            gdn_p[None, None], gdn_s[None],
            gconv_p, xe_s[:, -3:][None],
            fconv_p, fconv_s,
            mk.reshape(1, 1, N_MEM, N_HEADS_C, HEAD_DIM), mv.reshape(1, 1, N_MEM, N_HEADS_C, HEAD_DIM))
```

```python
import functools

import numpy as np
import jax
import jax.numpy as jnp
from jax import lax
from jax.experimental import pallas as pl
from jax.experimental.pallas import tpu as pltpu

F32 = jnp.float32
BF16 = jnp.bfloat16
EPS = 1e-6
NEG = -1e30

D_MODEL = 1024
HEAD_DIM = 64
N_HEADS_A = 8
PATTERNS = ((128, 1), (512, 4), (2048, 16))
DILATIONS = tuple(d for _, d in PATTERNS)
MAX_WINDOW = 2048
N_HEADS_B = 4
N_HEADS_C = 4
N_MEM = 256
D_FF = 2816
WIDTH_A = N_HEADS_A * HEAD_DIM
WIDTH_B = N_HEADS_B * HEAD_DIM
WIDTH_C = N_HEADS_C * HEAD_DIM
QKV_B = 3 * WIDTH_B
CHUNK = 64
NBAND = 128
ROW_TILE = 512
FF_TILE = 256
LANES = 128
VMEM_LIMIT = 56 * 1024 * 1024

SEG_QA, SEG_KA, SEG_VA = (0, 512), (512, 1024), (1024, 1536)
SEG_QKV, SEG_GATE, SEG_QC, SEG_AB = (1536, 2304), (2304, 2560), (2560, 2816), (2816, 2944)
N_IN_PAD = 2944


def _params(*sem):
    return pltpu.CompilerParams(dimension_semantics=sem, vmem_limit_bytes=VMEM_LIMIT)


def _dot(a, b):
    return jnp.dot(a, b, preferred_element_type=F32)


def _dot_nt(a, b):
    return lax.dot_general(a, b, (((1,), (1,)), ((), ())), preferred_element_type=F32)


def _dot_tn(a, b):
    return lax.dot_general(a, b, (((0,), (0,)), ((), ())), preferred_element_type=F32)


def _dot_split(a, b, parts, left=False):
    acc = None
    rem = a
    for _ in range(parts):
        piece = rem.astype(BF16)
        term = _dot(b, piece) if left else _dot(piece, b)
        acc = term if acc is None else acc + term
        rem = rem - piece.astype(F32)
    return acc


def _sigmoid(x):
    return 1.0 / (1.0 + jnp.exp(-x))


def _silu(x):
    return x * _sigmoid(x)


def _softplus(x):
    return jnp.maximum(x, 0.0) + jnp.log1p(jnp.exp(-jnp.abs(x)))


def _group_ones(width, group=HEAD_DIM):
    i = np.arange(width)
    return jnp.asarray((i[:, None] // group) == (i[None, :] // group), BF16)


def _full(shape):
    nd = len(shape)
    return pl.BlockSpec(shape, lambda *_: (0,) * nd)


def _inproj_kernel(dils, x_ref, g1_ref, w_ref, gq_ref, gk_ref, gc_ref, ga_ref, gcc_ref, *refs):
    nd = len(dils)
    q_refs, k_refs, v_refs = refs[0:nd], refs[nd:2 * nd], refs[2 * nd:3 * nd]
    ka_ref, va_ref, qkv_ref, gate_ref, qc_ref, ab_ref, scr_ref = refs[3 * nd:]
    tm = x_ref.shape[0]
    x = x_ref[...]
    ms = jnp.mean(x * x, axis=-1, keepdims=True)
    xn = (x * lax.rsqrt(ms + EPS) * g1_ref[...]).astype(BF16)

    def seg(s):
        return _dot(xn, w_ref[:, s[0]:s[1]])

    def head_norm(z, ones_ref, gain):
        ss = _dot_split(z * z, ones_ref[...], 2) * (1.0 / HEAD_DIM)
        return z * lax.rsqrt(ss + EPS) * gain

    def emit(z, out_refs):
        n_tiles = WIDTH_A // LANES
        for c in range(n_tiles):
            scr_ref[c] = z[:, c * LANES:(c + 1) * LANES]
        for d, ref in zip(dils, out_refs):
            if d == 1:
                ref[...] = z.astype(BF16)
            else:
                for r in range(d):
                    for c in range(n_tiles):
                        col = r * WIDTH_A + c * LANES
                        ref[:, col:col + LANES] = scr_ref[c, pl.ds(r, tm // d, stride=d), :].astype(BF16)

    scale = HEAD_DIM ** -0.5
    emit(head_norm(seg(SEG_QA), ga_ref, gq_ref[...]) * scale, q_refs)
    ka = head_norm(seg(SEG_KA), ga_ref, gk_ref[...])
    ka_ref[...] = ka
    emit(ka, k_refs)
    va = seg(SEG_VA)
    va_ref[...] = va
    emit(va, v_refs)
    qkv_ref[...] = seg(SEG_QKV)
    gate_ref[...] = seg(SEG_GATE)
    qc_ref[...] = (head_norm(seg(SEG_QC), gcc_ref, gc_ref[...]) * scale).astype(BF16)
    ab_ref[...] = seg(SEG_AB)


def _inproj(x, g1, w_in_p, gq, gk, gc, dils, keep_rows):
    rows = x.shape[0]
    tm = ROW_TILE
    nt = rows // tm
    skip = nt - keep_rows // tm
    row = lambda w: pl.BlockSpec((tm, w), lambda i: (i, 0))
    tail = pl.BlockSpec((tm, WIDTH_A), lambda i: (jnp.maximum(i - skip, 0), 0))
    dil_specs = [pl.BlockSpec((tm // d, d * WIDTH_A), lambda i: (i, 0)) for d in dils]
    dil_shapes = [jax.ShapeDtypeStruct((rows // d, d * WIDTH_A), BF16) for d in dils]
    outs = [(QKV_B, F32), (WIDTH_B, F32), (WIDTH_C, BF16), (LANES, F32)]
    res = pl.pallas_call(
        functools.partial(_inproj_kernel, dils),
        grid=(nt,),
        in_specs=[row(D_MODEL), _full((1, D_MODEL)), _full((D_MODEL, N_IN_PAD)), _full((1, WIDTH_A)),
                  _full((1, WIDTH_A)), _full((1, WIDTH_C)), _full((WIDTH_A, WIDTH_A)), _full((WIDTH_C, WIDTH_C))],
        out_specs=dil_specs * 3 + [tail, tail] + [row(w) for w, _ in outs],
        out_shape=dil_shapes * 3 + [jax.ShapeDtypeStruct((keep_rows, WIDTH_A), F32)] * 2
        + [jax.ShapeDtypeStruct((rows, w), dt) for w, dt in outs],
        scratch_shapes=[pltpu.VMEM((WIDTH_A // LANES, tm, LANES), F32)],
        compiler_params=_params("arbitrary"),
        name="inproj",
    )(x, g1, w_in_p, gq, gk, gc, _group_ones(WIDTH_A), _group_ones(WIDTH_C))
    nd = len(dils)
    return res[0:nd], res[nd:2 * nd], res[2 * nd:3 * nd], res[3 * nd:]


def _alibi_slopes():
    return np.exp2(-8.0 * np.arange(1, N_HEADS_A + 1, dtype=np.float64) / N_HEADS_A)


def _band_bias(dil):
    qi = np.arange(NBAND)[:, None]
    kj = np.arange(2 * NBAND)[None, :]
    delta = qi + NBAND - kj
    in_band = (delta >= 0) & (delta <= NBAND)
    bias = -_alibi_slopes()[:, None, None] * (delta * dil)[None].astype(np.float64)
    general = np.where(in_band[None], bias, NEG)
    first = np.where((in_band & (kj >= NBAND))[None], bias, NEG)
    return jnp.asarray(np.stack([general, first]), F32)


def _attn_band_kernel(q_ref, kp_ref, kc_ref, vp_ref, vc_ref, bias_ref, o_ref, lse_ref):
    first = (pl.program_id(1) == 0).astype(jnp.int32)
    lane = lax.broadcasted_iota(jnp.int32, (NBAND, LANES), 1)
    low = lane < HEAD_DIM
    ones = jnp.ones((2 * NBAND, LANES), BF16)
    lse = jnp.zeros((NBAND, LANES), F32)
    for pair in range(N_HEADS_A // 2):
        cols = slice(pair * LANES, (pair + 1) * LANES)
        q = q_ref[:, cols]
        k = jnp.concatenate([kp_ref[:, cols], kc_ref[:, cols]], axis=0)
        v_ext = jnp.concatenate([jnp.concatenate([vp_ref[:, cols], vc_ref[:, cols]], axis=0), ones], axis=1)
        outs = []
        for half in range(2):
            h = 2 * pair + half
            qm = jnp.where(low if half == 0 else ~low, q, jnp.zeros((), BF16))
            s = _dot_nt(qm, k) + bias_ref[first, h]
            m = jnp.max(s, axis=-1, keepdims=True)
            p = jnp.exp(s - m).astype(BF16)
            r = _dot(p, v_ext)
            den = r[:, LANES:]
            outs.append(r[:, :LANES] / den)
            lse = jnp.where(lane == h, m + jnp.log(den), lse)
        o_ref[:, cols] = jnp.where(low, outs[0], outs[1]).astype(BF16)
    lse_ref[...] = lse


def _attn_band(q_d, k_d, v_d, dil):
    rows = q_d.shape[0]
    nblk = rows // NBAND
    cur = lambda w: pl.BlockSpec((NBAND, w), lambda r, n: (n, r))
    prev = lambda w: pl.BlockSpec((NBAND, w), lambda r, n: (jnp.maximum(n - 1, 0), r))
    return pl.pallas_call(
        _attn_band_kernel,
        grid=(dil, nblk),
        in_specs=[cur(WIDTH_A), prev(WIDTH_A), cur(WIDTH_A), prev(WIDTH_A), cur(WIDTH_A),
                  _full((2, N_HEADS_A, NBAND, 2 * NBAND))],
        out_specs=[cur(WIDTH_A), cur(LANES)],
        out_shape=[jax.ShapeDtypeStruct((rows, dil * WIDTH_A), BF16), jax.ShapeDtypeStruct((rows, dil * LANES), F32)],
        compiler_params=_params("parallel", "arbitrary"),
        name=f"attn_band_d{dil}",
    )(q_d, k_d, k_d, v_d, v_d, _band_bias(dil))


def _memkv_kernel(mem_ref, g_ref, w_ref, gk_ref, ones_ref, mk_ref, mv_ref):
    x = mem_ref[...]
    ms = jnp.mean(x * x, axis=-1, keepdims=True)
    xn = (x * lax.rsqrt(ms + EPS) * g_ref[...]).astype(BF16)
    zk = _dot(xn, w_ref[:, 0:WIDTH_C])
    ss = _dot_split(zk * zk, ones_ref[...], 2) * (1.0 / HEAD_DIM)
    mk_ref[...] = zk * lax.rsqrt(ss + EPS) * gk_ref[...]
    mv_ref[...] = _dot(xn, w_ref[:, WIDTH_C:2 * WIDTH_C])


def _memkv(mem, g, w, gk):
    n = mem.shape[0]
    return pl.pallas_call(
        _memkv_kernel,
        out_shape=[jax.ShapeDtypeStruct((n, WIDTH_C), F32)] * 2,
        compiler_params=pltpu.CompilerParams(vmem_limit_bytes=VMEM_LIMIT),
        name="memkv",
    )(mem, g, w, gk, _group_ones(WIDTH_C))


def _head_spread():
    e = np.zeros((LANES, WIDTH_A), np.float32)
    for h in range(N_HEADS_A):
        e[h, h * HEAD_DIM:(h + 1) * HEAD_DIM] = 1.0
    return jnp.asarray(e, BF16)


def _combine_mem_kernel(dils, *refs):
    nd = len(dils)
    o_refs, l_refs = refs[0:nd], refs[nd:2 * nd]
    qc_ref, mk_ref, mv_ref, e_ref, oa_ref, oc_ref, o_scr, l_scr = refs[2 * nd:]
    tm = oa_ref.shape[0]
    os_, ls = [], []
    for p, d in enumerate(dils):
        if d == 1:
            os_.append(o_refs[p][...].astype(F32))
            ls.append(l_refs[p][...])
            continue
        n_tiles = WIDTH_A // LANES
        for r in range(d):
            for c in range(n_tiles):
                col = r * WIDTH_A + c * LANES
                o_scr[p * n_tiles + c, pl.ds(r, tm // d, stride=d), :] = o_refs[p][:, col:col + LANES].astype(F32)
            l_scr[p, pl.ds(r, tm // d, stride=d), :] = l_refs[p][:, r * LANES:(r + 1) * LANES]
        os_.append(jnp.concatenate([o_scr[p * n_tiles + c] for c in range(n_tiles)], axis=1))
        ls.append(l_scr[p])
    m = functools.reduce(jnp.maximum, ls)
    es = [jnp.exp(l - m) for l in ls]
    tot = functools.reduce(lambda a, b: a + b, es)
    acc = None
    for o, e in zip(os_, es):
        term = o * _dot_split(e / tot, e_ref[...], 2)
        acc = term if acc is None else acc + term
    oa_ref[...] = acc.astype(BF16)

    q = qc_ref[...]
    mk = mk_ref[...].astype(BF16)
    mv = mv_ref[...].astype(BF16)
    for h in range(N_HEADS_C):
        sl = slice(h * HEAD_DIM, (h + 1) * HEAD_DIM)
        s = _dot_nt(q[:, sl], mk[:, sl])
        mx = jnp.max(s, axis=-1, keepdims=True)
        p = jnp.exp(s - mx)
        den_c = jnp.sum(p, axis=-1, keepdims=True)
        oc_ref[:, sl] = (_dot(p.astype(BF16), mv[:, sl]) / den_c).astype(BF16)


def _combine_mem(os_, lses, dils, qc, mk, mv):
    t = qc.shape[0]
    tm = ROW_TILE
    nd = len(dils)
    row = lambda w: pl.BlockSpec((tm, w), lambda i: (i, 0))
    o_specs = [pl.BlockSpec((tm // d, d * WIDTH_A), lambda i: (i, 0)) for d in dils]
    l_specs = [pl.BlockSpec((tm // d, d * LANES), lambda i: (i, 0)) for d in dils]
    return pl.pallas_call(
        functools.partial(_combine_mem_kernel, dils),
        grid=(t // tm,),
        in_specs=o_specs + l_specs + [row(WIDTH_C), _full((N_MEM, WIDTH_C)), _full((N_MEM, WIDTH_C)),
                                     _full((LANES, WIDTH_A))],
        out_specs=[row(WIDTH_A), row(WIDTH_C)],
        out_shape=[jax.ShapeDtypeStruct((t, WIDTH_A), BF16), jax.ShapeDtypeStruct((t, WIDTH_C), BF16)],
        scratch_shapes=[pltpu.VMEM((nd * WIDTH_A // LANES, tm, LANES), F32), pltpu.VMEM((nd, tm, LANES), F32)],
        compiler_params=_params("parallel"),
        name="combine_mem",
    )(*os_, *lses, qc, mk, mv, _head_spread())


def _head_lane_consts():
    e_g = np.zeros((LANES, WIDTH_B), np.float32)
    e_b = np.zeros((LANES, WIDTH_B), np.float32)
    for h in range(N_HEADS_B):
        e_g[h, h * HEAD_DIM:(h + 1) * HEAD_DIM] = 1.0
        e_b[N_HEADS_B + h, h * HEAD_DIM:(h + 1) * HEAD_DIM] = 1.0
    i = np.arange(CHUNK)
    ltri = (i[None, :] <= i[:, None]).astype(np.float32)
    return jnp.asarray(e_g, BF16), jnp.asarray(e_b, BF16), jnp.asarray(ltri, BF16), jnp.ones((CHUNK, CHUNK), BF16)


def _block_diag(x, mask):
    return jnp.where(mask, jnp.concatenate([x] * N_HEADS_B, axis=0), jnp.zeros((), x.dtype))


def _gdn_prep_kernel(x_ref, halo_ref, ab_ref, cw_ref, alog_ref, dtb_ref, ones_ref, eg_ref, eb_ref, ltri_ref, one64_ref,
                     m_ref, attn_ref, qd_ref, kd_ref, vb_ref, kbe_ref, egl_ref, xe_ref):
    tm = x_ref.shape[0]
    halo = jnp.where(pl.program_id(0) > 0, halo_ref[...], 0.0)
    xe_ref[0:8, :] = halo
    xe_ref[8:8 + tm, :] = x_ref[...]
    cw = cw_ref[...]
    y = cw[3:4] * xe_ref[8:8 + tm, :] + cw[2:3] * xe_ref[7:7 + tm, :] + cw[1:2] * xe_ref[6:6 + tm, :] \
        + cw[0:1] * xe_ref[5:5 + tm, :]
    y = _silu(y)

    def l2n(z):
        return z * lax.rsqrt(_dot_split(z * z, ones_ref[...], 2) + EPS)

    q = l2n(y[:, 0:WIDTH_B]) * (HEAD_DIM ** -0.5)
    k = l2n(y[:, WIDTH_B:2 * WIDTH_B])
    v = y[:, 2 * WIDTH_B:3 * WIDTH_B]
    ab = ab_ref[...]
    g = -jnp.exp(alog_ref[...]) * _softplus(ab + dtb_ref[...])
    gx = _dot_split(g, eg_ref[...], 3)
    bx = _dot_split(_sigmoid(ab), eb_ref[...], 2)

    row = lax.broadcasted_iota(jnp.int32, (CHUNK, WIDTH_B), 0)
    col = lax.broadcasted_iota(jnp.int32, (CHUNK, WIDTH_B), 1) % CHUNK
    diag = row == col
    r4 = lax.broadcasted_iota(jnp.int32, (WIDTH_B, WIDTH_B), 0) // HEAD_DIM
    c4 = lax.broadcasted_iota(jnp.int32, (WIDTH_B, WIDTH_B), 1) // HEAD_DIM
    bd_mask = r4 == c4
    for c in range(tm // CHUNK):
        sl = slice(c * CHUNK, (c + 1) * CHUNK)
        gc = _dot_split(gx[sl], ltri_ref[...], 3, left=True)
        gl = gc[CHUNK - 1:CHUNK, :]
        gcj = _dot_split(jnp.where(diag, gc, 0.0), one64_ref[...], 3, left=True)
        dec = jnp.exp(jnp.where(row >= col, gc - gcj, NEG))
        kc, qc, vc, bc = k[sl], q[sl], v[sl], bx[sl]
        kb = kc * bc
        lhs = jnp.concatenate([kb, qc], axis=0).astype(BF16)
        kkqk = _dot_nt(lhs, _block_diag(kc.astype(BF16), bd_mask))
        m_ref[sl, :] = jnp.where(row > col, kkqk[0:CHUNK] * dec, 0.0)
        attn_ref[sl, :] = (kkqk[CHUNK:2 * CHUNK] * dec).astype(BF16)
        egc = jnp.exp(gc)
        qd_ref[sl, :] = (qc * egc).astype(BF16)
        kd_ref[sl, :] = (kc * jnp.exp(gl - gc)).astype(BF16)
        vb_ref[sl, :] = (vc * bc).astype(BF16)
        kbe_ref[sl, :] = (kb * egc).astype(BF16)
        egl_ref[c:c + 1, :] = jnp.exp(gl)


def _gdn_prep(qkv, ab, conv_w, alog_row, dtb_row):
    t = qkv.shape[0]
    tm = ROW_TILE
    nc = tm // CHUNK
    row = lambda w: pl.BlockSpec((tm, w), lambda i: (i, 0))
    e_g, e_b, ltri, one64 = _head_lane_consts()
    outs = [F32, BF16, BF16, BF16, BF16, BF16]
    return pl.pallas_call(
        _gdn_prep_kernel,
        grid=(t // tm,),
        in_specs=[row(QKV_B), pl.BlockSpec((8, QKV_B), lambda i: (jnp.maximum(i * (tm // 8) - 1, 0), 0)), row(LANES),
                  _full((4, QKV_B)), _full((1, LANES)), _full((1, LANES)), _full((WIDTH_B, WIDTH_B)),
                  _full((LANES, WIDTH_B)), _full((LANES, WIDTH_B)), _full((CHUNK, CHUNK)), _full((CHUNK, CHUNK))],
        out_specs=[row(WIDTH_B)] * 6 + [pl.BlockSpec((nc, WIDTH_B), lambda i: (i, 0))],
        out_shape=[jax.ShapeDtypeStruct((t, WIDTH_B), dt) for dt in outs]
        + [jax.ShapeDtypeStruct((t // CHUNK, WIDTH_B), F32)],
        scratch_shapes=[pltpu.VMEM((tm + 8, QKV_B), F32)],
        compiler_params=_params("parallel"),
        name="gdn_prep",
    )(qkv, qkv, ab, conv_w, alog_row, dtb_row, _group_ones(WIDTH_B), e_g, e_b, ltri, one64)


def _tinv_kernel(m_ref, out_ref, n_ref):
    blk = pl.program_id(0)
    for ii in range(8):
        i = blk * 8 + ii
        for kb in range(8):
            k0 = 8 * kb
            init = tuple(m_ref[ii, k0 + kk] for kk in range(8))

            def body(j, acc, ii=ii, k0=k0):
                mij = m_ref[ii, j]
                return tuple(acc[kk] + mij * n_ref[j, k0 + kk] for kk in range(8))

            acc = lax.fori_loop(k0, jnp.maximum(i, k0), body, init)
            for kk in range(8):
                val = jnp.where(k0 + kk < i, -acc[kk], 0.0)
                n_ref[i, k0 + kk] = val
                out_ref[ii, k0 + kk] = val


def _tinv(mt):
    return pl.pallas_call(
        _tinv_kernel,
        grid=(CHUNK // 8,),
        in_specs=[pl.BlockSpec((8, CHUNK, 8, LANES), lambda i: (i, 0, 0, 0))],
        out_specs=pl.BlockSpec((8, CHUNK, 8, LANES), lambda i: (i, 0, 0, 0)),
        out_shape=jax.ShapeDtypeStruct(mt.shape, F32),
        scratch_shapes=[pltpu.VMEM(mt.shape, F32)],
        compiler_params=_params("arbitrary"),
        name="gdn_tinv",
    )(mt)


def _gdn_scan_kernel(n_ref, vb_ref, kbe_ref, attn_ref, qd_ref, kd_ref, egl_ref, gate_ref, gn_ref, ones_ref,
                     ob_ref, s_out_ref, s_ref):
    tm = n_ref.shape[0]

    @pl.when(pl.program_id(0) == 0)
    def _():
        s_ref[...] = jnp.zeros_like(s_ref)

    row = lax.broadcasted_iota(jnp.int32, (CHUNK, WIDTH_B), 0)
    col = lax.broadcasted_iota(jnp.int32, (CHUNK, WIDTH_B), 1) % CHUNK
    eye = (row == col).astype(F32)
    r4 = lax.broadcasted_iota(jnp.int32, (WIDTH_B, WIDTH_B), 0) // HEAD_DIM
    c4 = lax.broadcasted_iota(jnp.int32, (WIDTH_B, WIDTH_B), 1) // HEAD_DIM
    bd_mask = r4 == c4
    s = s_ref[...]
    for c in range(tm // CHUNK):
        sl = slice(c * CHUNK, (c + 1) * CHUNK)
        tinv = (n_ref[sl, :] + eye).astype(BF16)
        u = _dot(tinv, _block_diag(vb_ref[sl, :], bd_mask))
        w = _dot(tinv, _block_diag(kbe_ref[sl, :], bd_mask))
        lhs = jnp.concatenate([w.astype(BF16), qd_ref[sl, :]], axis=0)
        ws = _dot(lhs, s.astype(BF16))
        v_new = (u - ws[0:CHUNK]).astype(BF16)
        o = ws[CHUNK:2 * CHUNK] + _dot(attn_ref[sl, :], _block_diag(v_new, bd_mask))
        s = s * egl_ref[c:c + 1, :] + jnp.where(bd_mask, _dot_tn(kd_ref[sl, :], v_new), 0.0)
        ms = _dot_split(o * o, ones_ref[...], 2) * (1.0 / HEAD_DIM)
        ob_ref[sl, :] = (o * lax.rsqrt(ms + EPS) * gn_ref[...] * _silu(gate_ref[sl, :])).astype(BF16)
    s_ref[...] = s
    s_out_ref[...] = s


def _gdn_scan(n_cat, vb, kbe, attn, qd, kd, egl, gate, gn_row):
    t = n_cat.shape[0]
    tm = ROW_TILE
    nc = tm // CHUNK
    row = lambda w: pl.BlockSpec((tm, w), lambda i: (i, 0))
    return pl.pallas_call(
        _gdn_scan_kernel,
        grid=(t // tm,),
        in_specs=[row(WIDTH_B)] * 6 + [pl.BlockSpec((nc, WIDTH_B), lambda i: (i, 0)), row(WIDTH_B),
                                       _full((1, WIDTH_B)), _full((WIDTH_B, WIDTH_B))],
        out_specs=[row(WIDTH_B), _full((WIDTH_B, WIDTH_B))],
        out_shape=[jax.ShapeDtypeStruct((t, WIDTH_B), BF16), jax.ShapeDtypeStruct((WIDTH_B, WIDTH_B), F32)],
        scratch_shapes=[pltpu.VMEM((WIDTH_B, WIDTH_B), F32)],
        compiler_params=_params("arbitrary"),
        name="gdn_scan",
    )(n_cat, vb, kbe, attn, qd, kd, egl, gate, gn_row, _group_ones(WIDTH_B))


def _gdn_prompt(qkv, ab, gate, conv_w, alog_row, dtb_row, gn_row):
    t = qkv.shape[0]
    nchunk = t // CHUNK
    nsys = nchunk * N_HEADS_B
    nsys_pad = -(-nsys // 1024) * 1024
    assert nsys_pad == 1024
    m, attn, qd, kd, vb, kbe, egl = _gdn_prep(qkv, ab, conv_w, alog_row, dtb_row)
    mt = m.reshape(nchunk, CHUNK, N_HEADS_B, CHUNK).transpose(1, 3, 0, 2).reshape(CHUNK, CHUNK, nsys)
    mt = jnp.pad(mt, ((0, 0), (0, 0), (0, nsys_pad - nsys))).reshape(CHUNK, CHUNK, 8, LANES)
    nt = _tinv(mt).reshape(CHUNK, CHUNK, nsys_pad)[:, :, :nsys]
    n_cat = nt.reshape(CHUNK, CHUNK, nchunk, N_HEADS_B).transpose(2, 0, 3, 1).reshape(t, WIDTH_B)
    ob, s_bd = _gdn_scan(n_cat, vb, kbe, attn, qd, kd, egl, gate, gn_row)
    s_fin = jnp.stack([s_bd[h * HEAD_DIM:(h + 1) * HEAD_DIM, h * HEAD_DIM:(h + 1) * HEAD_DIM] for h in range(N_HEADS_B)])
    return ob, s_fin


def _outproj_kernel(x_ref, oa_ref, ob_ref, oc_ref, w_ref, y_ref):
    acc = _dot(oa_ref[...], w_ref[0:WIDTH_A, :])
    acc = acc + _dot(ob_ref[...], w_ref[WIDTH_A:WIDTH_A + WIDTH_B, :])
    acc = acc + _dot(oc_ref[...], w_ref[WIDTH_A + WIDTH_B:, :])
    y_ref[...] = x_ref[...] + acc


def _outproj(x, oa, ob, oc, w_out):
    rows = x.shape[0]
    tm = ROW_TILE
    row = lambda w: pl.BlockSpec((tm, w), lambda i: (i, 0))
    return pl.pallas_call(
        _outproj_kernel,
        grid=(rows // tm,),
        in_specs=[row(D_MODEL), row(WIDTH_A), row(WIDTH_B), row(WIDTH_C), _full((D_MODEL, D_MODEL))],
        out_specs=row(D_MODEL),
        out_shape=jax.ShapeDtypeStruct((rows, D_MODEL), F32),
        compiler_params=_params("parallel"),
        name="outproj",
    )(x, oa, ob, oc, w_out)


def _ffn_kernel(shift, carry_rows, x_ref, g_ref, wup_ref, cw_ref, wdn_ref, c0_ref, y_ref, last_ref, carry_ref, ext_ref):
    tm = x_ref.shape[0]

    @pl.when(pl.program_id(0) == 0)
    def _():
        carry_ref[...] = c0_ref[...]

    x = x_ref[...]
    ms = jnp.mean(x * x, axis=-1, keepdims=True)
    xn = (x * lax.rsqrt(ms + EPS) * g_ref[...]).astype(BF16)
    y_ref[...] = x
    tf = FF_TILE
    base = carry_rows
    for j in range(D_FF // tf):
        halves = []
        for half in range(2):
            cols = slice(half * D_FF + j * tf, half * D_FF + (j + 1) * tf)
            up = _dot(xn, wup_ref[:, cols])
            ext_ref[0:base, :] = carry_ref[:, cols]
            ext_ref[base:base + tm, :] = up
            carry_ref[:, cols] = ext_ref[tm:tm + base, :]
            cw = cw_ref[:, cols]
            halves.append(cw[0:1] * ext_ref[base - 2 * shift:base - 2 * shift + tm, :]
                          + cw[1:2] * ext_ref[base - shift:base - shift + tm, :] + cw[2:3] * up)
        hidden = (_silu(halves[0]) * halves[1]).astype(BF16)
        y_ref[...] += _dot(hidden, wdn_ref[j * tf:(j + 1) * tf, :])
    last_ref[...] = carry_ref[...]


def _ffn(x, g, w_up, conv_w, w_down, carry0, shift):
    rows = x.shape[0]
    tm = ROW_TILE
    carry_rows = carry0.shape[0]
    row = lambda w: pl.BlockSpec((tm, w), lambda i: (i, 0))
    once = lambda shape: pl.BlockSpec(shape, lambda i: (0,) * len(shape), pipeline_mode=pl.Buffered(1))
    return pl.pallas_call(
        functools.partial(_ffn_kernel, shift, carry_rows),
        grid=(rows // tm,),
        in_specs=[row(D_MODEL), once((1, D_MODEL)), once((D_MODEL, 2 * D_FF)), once((3, 2 * D_FF)),
                  once((D_FF, D_MODEL)), once((carry_rows, 2 * D_FF))],
        out_specs=[row(D_MODEL), _full((carry_rows, 2 * D_FF))],
        out_shape=[jax.ShapeDtypeStruct((rows, D_MODEL), F32), jax.ShapeDtypeStruct((carry_rows, 2 * D_FF), F32)],
        scratch_shapes=[pltpu.VMEM((carry_rows, 2 * D_FF), F32), pltpu.VMEM((tm + carry_rows, FF_TILE), F32)],
        compiler_params=_params("arbitrary"),
        name=f"ffn_shift{shift}",
    )(x, g, w_up, conv_w, w_down, carry0)


STRIDE_16 = 16
N_STRIDED = 96
DENSE_START = STRIDE_16 * N_STRIDED
SAMPLE_GROUP = 2


def _sample_bias(wbuf, s_len):
    def table(key_pos):
        q_pos = wbuf + np.arange(s_len)
        dist = q_pos[:, None] - key_pos[None, :]
        mult = np.zeros(dist.shape)
        for window, dil in PATTERNS:
            mult += (dist >= 0) & (dist <= window) & (dist % dil == 0)
        logm = np.log(np.maximum(mult, 1))
        bias = -_alibi_slopes()[:, None, None] * np.maximum(dist, 0)[None] + logm[None]
        bias = np.where(mult[None] > 0, bias, NEG)
        same_head = np.eye(N_HEADS_A, dtype=bool)[:, None, None, :]
        full = np.where(same_head, bias[..., None], NEG)
        return full.reshape(N_HEADS_A * s_len, -1)
    strided = table((STRIDE_16 * np.arange(N_STRIDED)[:, None] + np.arange(4)[None, :]).reshape(-1))
    dense = table(np.arange(DENSE_START, wbuf))
    new = table(np.concatenate([wbuf + np.arange(s_len), np.full(8 - s_len, 10 ** 6)]))
    return jnp.asarray(strided, F32), jnp.asarray(dense, F32), jnp.asarray(new, F32)


def _sample_attn_kernel(s_len, q_ref, kn_ref, vn_ref, ks_ref, vs_ref, kd_ref, vd_ref, bs_ref, bd_ref, bn_ref,
                        qc_ref, mk_ref, mv_ref, oa_ref, oc_ref):
    r = lax.broadcasted_iota(jnp.int32, (N_HEADS_C * s_len, WIDTH_C), 0) // s_len
    c = lax.broadcasted_iota(jnp.int32, (N_HEADS_C * s_len, WIDTH_C), 1) // HEAD_DIM
    sel_c = r == c
    for b in range(q_ref.shape[0]):
        q = q_ref[b]
        blocks = [(kn_ref[b], vn_ref[b], bn_ref), (ks_ref[b], vs_ref[b], bs_ref), (kd_ref[b], vd_ref[b], bd_ref)]
        blocks = [(kk.reshape(-1, HEAD_DIM).astype(BF16), vv.reshape(-1, HEAD_DIM).astype(BF16), bias[...])
                  for kk, vv, bias in blocks]
        scores = [_dot_nt(q, kk) + bias for kk, _, bias in blocks]
        m = functools.reduce(jnp.maximum, [jnp.max(s, axis=-1, keepdims=True) for s in scores])
        den = 0.0
        num = 0.0
        for s, (_, vv, _) in zip(scores, blocks):
            p = jnp.exp(s - m)
            den = den + jnp.sum(p, axis=-1, keepdims=True)
            num = num + _dot(p.astype(BF16), vv)
        oa_ref[b] = num / den

        qm = jnp.where(sel_c, qc_ref[b], jnp.zeros((), BF16))
        s = _dot_nt(qm, mk_ref[b].astype(BF16))
        mx = jnp.max(s, axis=-1, keepdims=True)
        p = jnp.exp(s - mx)
        oc_ref[b] = _dot(p.astype(BF16), mv_ref[b].astype(BF16)) / jnp.sum(p, axis=-1, keepdims=True)


def _sample_attn(qa, ka, va, cache_k, cache_v, qc, mem_k, mem_v):
    nb, s_len = qa.shape[:2]
    wbuf = cache_k.shape[1]
    nh = N_HEADS_A
    assert wbuf == MAX_WINDOW and s_len <= 4
    g = SAMPLE_GROUP
    bs, bd, bn = _sample_bias(wbuf, s_len)
    q_rows = qa.transpose(0, 2, 1, 3).reshape(nb, nh * s_len, HEAD_DIM)
    pad8 = lambda a: jnp.pad(a, ((0, 0), (0, 8 - s_len), (0, 0), (0, 0)))
    grouped = lambda a: a.reshape(nb, wbuf // STRIDE_16, STRIDE_16, nh, HEAD_DIM)
    per_b = lambda shape: pl.BlockSpec((g,) + shape, lambda i: (i,) + (0,) * len(shape))
    n_dense = (wbuf - DENSE_START) // STRIDE_16
    spec_strided = pl.BlockSpec((g, N_STRIDED, 4, nh, HEAD_DIM), lambda i: (i, 0, 0, 0, 0))
    spec_dense = pl.BlockSpec((g, n_dense, STRIDE_16, nh, HEAD_DIM), lambda i: (i, N_STRIDED // n_dense, 0, 0, 0))
    oa, oc = pl.pallas_call(
        functools.partial(_sample_attn_kernel, s_len),
        grid=(nb // g,),
        in_specs=[per_b((nh * s_len, HEAD_DIM)), per_b((8, nh, HEAD_DIM)), per_b((8, nh, HEAD_DIM)),
                  spec_strided, spec_strided, spec_dense, spec_dense,
                  _full(bs.shape), _full(bd.shape), _full(bn.shape),
                  per_b((N_HEADS_C * s_len, WIDTH_C)), per_b((N_MEM, WIDTH_C)), per_b((N_MEM, WIDTH_C))],
        out_specs=[per_b((nh * s_len, HEAD_DIM)), per_b((N_HEADS_C * s_len, WIDTH_C))],
        out_shape=[jax.ShapeDtypeStruct((nb, nh * s_len, HEAD_DIM), F32),
                   jax.ShapeDtypeStruct((nb, N_HEADS_C * s_len, WIDTH_C), F32)],
        compiler_params=_params("parallel"),
        name="sample_attn",
    )(q_rows, pad8(ka), pad8(va), grouped(cache_k), grouped(cache_v), grouped(cache_k), grouped(cache_v),
      bs, bd, bn, jnp.tile(qc, (1, N_HEADS_C, 1)), mem_k, mem_v)
    oa = oa.reshape(nb, nh, s_len, HEAD_DIM).transpose(0, 2, 1, 3).reshape(nb, s_len, WIDTH_A)
    oc = oc.reshape(nb, N_HEADS_C, s_len, N_HEADS_C, HEAD_DIM)
    oc = jnp.stack([oc[:, h, :, h, :] for h in range(N_HEADS_C)], axis=2).reshape(nb, s_len, WIDTH_C)
    return oa.astype(BF16), oc.astype(BF16)


def _sample_gdn_kernel(xe_ref, cw_ref, a_ref, b_ref, alog_ref, dtb_ref, gate_ref, gn_ref, s_ref, o_ref, s_out_ref,
                       q_scr, k_scr, v_scr):
    s_len = o_ref.shape[0]
    s_out_ref[...] = s_ref[...]
    for t in range(s_len):
        for part, scr in enumerate((q_scr, k_scr, v_scr)):
            y = sum(cw_ref[j, part] * xe_ref[t + j, part] for j in range(4))
            y = _silu(y)
            if part < 2:
                y = y * lax.rsqrt(jnp.sum(y * y, axis=0, keepdims=True) + EPS)
            if part == 0:
                y = y * (HEAD_DIM ** -0.5)
            scr[...] = y
        decay = jnp.exp(-jnp.exp(alog_ref[...]) * _softplus(a_ref[t] + dtb_ref[...]))
        beta = _sigmoid(b_ref[t])

        def read_body(dk, r):
            return r + k_scr[pl.ds(dk, 1), :] * s_out_ref[dk]

        r = lax.fori_loop(0, HEAD_DIM, read_body, jnp.zeros(v_scr.shape, F32), unroll=8)
        v_new = beta * (v_scr[...] - decay * r)

        def write_body(dk, o):
            s_new = decay * s_out_ref[dk] + k_scr[pl.ds(dk, 1), :] * v_new
            s_out_ref[dk] = s_new
            return o + q_scr[pl.ds(dk, 1), :] * s_new

        o = lax.fori_loop(0, HEAD_DIM, write_body, jnp.zeros(v_scr.shape, F32), unroll=8)
        ms = jnp.mean(o * o, axis=0, keepdims=True)
        o_ref[t] = o * lax.rsqrt(ms + EPS) * gn_ref[...] * _silu(gate_ref[t])


def _sample_gdn(xe, conv_w, a_b, b_b, a_log, dt_bias, gate, out_norm, state):
    nb, ext, _ = xe.shape
    s_len = ext - 3
    nh, hd = N_HEADS_B, HEAD_DIM
    xe_t = xe.reshape(nb, ext, 3, nh, hd).transpose(1, 2, 3, 4, 0)
    cw_t = jnp.broadcast_to(conv_w.reshape(4, 3, nh, hd)[..., None], (4, 3, nh, hd, nb))
    a_t = a_b.transpose(1, 2, 0).reshape(s_len, nh, 1, nb)
    b_t = b_b.transpose(1, 2, 0).reshape(s_len, nh, 1, nb)
    alog_t = jnp.broadcast_to(a_log.reshape(nh, 1, 1), (nh, 1, nb))
    dtb_t = jnp.broadcast_to(dt_bias.reshape(nh, 1, 1), (nh, 1, nb))
    gate_t = gate.reshape(nb, s_len, nh, hd).transpose(1, 2, 3, 0)
    gn_t = jnp.broadcast_to(out_norm.reshape(1, hd, 1), (1, hd, nb))
    s_t = state.transpose(1, 2, 3, 0)
    o_t, s_new = pl.pallas_call(
        _sample_gdn_kernel,
        grid=(nh,),
        in_specs=[pl.BlockSpec((ext, 3, None, hd, nb), lambda h: (0, 0, h, 0, 0)),
                  pl.BlockSpec((4, 3, None, hd, nb), lambda h: (0, 0, h, 0, 0)),
                  pl.BlockSpec((s_len, None, 1, nb), lambda h: (0, h, 0, 0)),
                  pl.BlockSpec((s_len, None, 1, nb), lambda h: (0, h, 0, 0)),
                  pl.BlockSpec((None, 1, nb), lambda h: (h, 0, 0)),
                  pl.BlockSpec((None, 1, nb), lambda h: (h, 0, 0)),
                  pl.BlockSpec((s_len, None, hd, nb), lambda h: (0, h, 0, 0)),
                  pl.BlockSpec((None, hd, nb), lambda h: (0, 0, 0)),
                  pl.BlockSpec((None, hd, hd, nb), lambda h: (h, 0, 0, 0))],
        out_specs=[pl.BlockSpec((s_len, None, hd, nb), lambda h: (0, h, 0, 0)),
                   pl.BlockSpec((None, hd, hd, nb), lambda h: (h, 0, 0, 0))],
        out_shape=[jax.ShapeDtypeStruct((s_len, nh, hd, nb), F32), jax.ShapeDtypeStruct((nh, hd, hd, nb), F32)],
        scratch_shapes=[pltpu.VMEM((hd, nb), F32)] * 3,
        compiler_params=_params("parallel"),
        name="sample_gdn",
    )(xe_t, cw_t, a_t, b_t, alog_t, dtb_t, gate_t, gn_t, s_t)
    ob = o_t.transpose(3, 0, 1, 2).reshape(nb * s_len, nh * hd).astype(BF16)
    return ob, s_new.transpose(3, 0, 1, 2)


def _permute_w_in(w_in):
    a_start = 3 * WIDTH_A + QKV_B + WIDTH_B
    qc_start = a_start + 2 * N_HEADS_B
    pad = jnp.zeros((w_in.shape[0], N_IN_PAD - w_in.shape[1]), w_in.dtype)
    return jnp.concatenate([w_in[:, :a_start], w_in[:, qc_start:], w_in[:, a_start:qc_start], pad], axis=1).astype(BF16)


def _lane_row(v, width=LANES):
    return jnp.zeros((1, width), F32).at[0, :v.shape[0]].set(v)


def kernel(x_prompt, x_sample, cache_win_k, cache_win_v, state_gdn, state_gdn_conv, state_ffn_conv, cache_mem_k,
           cache_mem_v, mem_prompt, norm1_g, w_in, q_norm_a, k_norm_a, conv_b_w, a_log_b, dt_bias_b, out_norm_b,
           mem_norm_g, w_mem_kv, q_norm_c, k_norm_c, w_out, norm2_g, w_up, conv_ffn_w, w_down):
    depth = norm1_g.shape[0]
    assert depth == 1 and x_prompt.shape[0] == 1
    l = 0
    t_p = x_prompt.shape[1]
    nb, s_len = x_sample.shape[:2]
    xp = x_prompt.reshape(t_p, D_MODEL)
    xs = x_sample.reshape(nb * s_len, D_MODEL)

    w_in_p = _permute_w_in(w_in[l])
    w_out_b = w_out[l].astype(BF16)
    w_up_b = w_up[l].astype(BF16)
    w_down_b = w_down[l].astype(BF16)
    g1 = norm1_g[l].reshape(1, D_MODEL)
    g2 = norm2_g[l].reshape(1, D_MODEL)
    gq = jnp.tile(q_norm_a[l], N_HEADS_A).reshape(1, WIDTH_A)
    gk = jnp.tile(k_norm_a[l], N_HEADS_A).reshape(1, WIDTH_A)
    gqc = jnp.tile(q_norm_c[l], N_HEADS_C).reshape(1, WIDTH_C)
    gkc = jnp.tile(k_norm_c[l], N_HEADS_C).reshape(1, WIDTH_C)
    gn_row = jnp.tile(out_norm_b[l], N_HEADS_B).reshape(1, WIDTH_B)

    n_keep = min(MAX_WINDOW, t_p)
    q_ds, k_ds, v_ds, (ka, va, qkv, gate, qc, ab) = _inproj(xp, g1, w_in_p, gq, gk, gqc, DILATIONS, n_keep)
    parts = [_attn_band(q_d, k_d, v_d, dil) for q_d, k_d, v_d, dil in zip(q_ds, k_ds, v_ds, DILATIONS)]
    mk, mv = _memkv(mem_prompt[0], mem_norm_g[l].reshape(1, D_MODEL), w_mem_kv[l].astype(BF16), gkc)
    oa, oc = _combine_mem([p[0] for p in parts], [p[1] for p in parts], DILATIONS, qc, mk, mv)
    ob, gdn_p = _gdn_prompt(qkv, ab, gate, conv_b_w[l], _lane_row(a_log_b[l]), _lane_row(dt_bias_b[l]), gn_row)
    x1 = _outproj(xp, oa, ob, oc, w_out_b)
    y_p, last_p = _ffn(x1, g2, w_up_b, conv_ffn_w[l], w_down_b, jnp.zeros((8, 2 * D_FF), F32), 1)
    win_k_p = ka.reshape(1, 1, n_keep, N_HEADS_A, HEAD_DIM)
    win_v_p = va.reshape(1, 1, n_keep, N_HEADS_A, HEAD_DIM)
    gconv_p = qkv[t_p - 3:].reshape(1, 1, 3, QKV_B)
    fconv_p = last_p[6:8].reshape(1, 1, 2, 2 * D_FF)

    rows_s = nb * s_len
    (qa_s,), _, _, (ka_s, va_s, qkv_s, gate_s, qc_s, ab_s) = _inproj(xs, g1, w_in_p, gq, gk, gqc, (1,), rows_s)
    b3 = lambda a: a.reshape(nb, s_len, a.shape[-1])
    heads = lambda a: a.reshape(nb, s_len, N_HEADS_A, HEAD_DIM)
    oa_s, oc_s = _sample_attn(heads(qa_s), heads(ka_s), heads(va_s), cache_win_k[l], cache_win_v[l], b3(qc_s),
                              cache_mem_k[l].reshape(nb, N_MEM, WIDTH_C), cache_mem_v[l].reshape(nb, N_MEM, WIDTH_C))
    xe_s = jnp.concatenate([state_gdn_conv[l], b3(qkv_s)], axis=1)
    ab3 = b3(ab_s)
    ob_s, gdn_s = _sample_gdn(xe_s, conv_b_w[l], ab3[..., 0:N_HEADS_B], ab3[..., N_HEADS_B:2 * N_HEADS_B], a_log_b[l],
                              dt_bias_b[l], b3(gate_s), out_norm_b[l], state_gdn[l])
    x1_s = _outproj(xs, oa_s.reshape(rows_s, WIDTH_A), ob_s, oc_s.reshape(rows_s, WIDTH_C), w_out_b)
    x1_t = x1_s.reshape(nb, s_len, D_MODEL).transpose(1, 0, 2).reshape(rows_s, D_MODEL)
    carry_s = state_ffn_conv[l].transpose(1, 0, 2).reshape(2 * nb, 2 * D_FF)
    y_t, last_s = _ffn(x1_t, g2, w_up_b, conv_ffn_w[l], w_down_b, carry_s, nb)
    y_s = y_t.reshape(s_len, nb, D_MODEL).transpose(1, 0, 2)
    fconv_s = last_s.reshape(2, nb, 2 * D_FF).transpose(1, 0, 2)[None]

    return (y_p.reshape(1, t_p, D_MODEL), y_s,
            win_k_p, win_v_p,
            ka_s.reshape(1, nb, s_len, N_HEADS_A, HEAD_DIM), va_s.reshape(1, nb, s_len, N_HEADS_A, HEAD_DIM),
            gdn_p[None, None], gdn_s[None],
            gconv_p, xe_s[:, -3:][None],
            fconv_p, fconv_s,
            mk.reshape(1, 1, N_MEM, N_HEADS_C, HEAD_DIM), mv.reshape(1, 1, N_MEM, N_HEADS_C, HEAD_DIM))
```

```python
import functools

import numpy as np
import jax
import jax.numpy as jnp
from jax import lax
from jax.experimental import pallas as pl
from jax.experimental.pallas import tpu as pltpu

F32 = jnp.float32
BF16 = jnp.bfloat16
EPS = 1e-6
NEG = -1e30

D_MODEL = 1024
HEAD_DIM = 64
N_HEADS_A = 8
PATTERNS = ((128, 1), (512, 4), (2048, 16))
DILATIONS = tuple(d for _, d in PATTERNS)
MAX_WINDOW = 2048
N_HEADS_B = 4
N_HEADS_C = 4
N_MEM = 256
D_FF = 2816
WIDTH_A = N_HEADS_A * HEAD_DIM
WIDTH_B = N_HEADS_B * HEAD_DIM
WIDTH_C = N_HEADS_C * HEAD_DIM
QKV_B = 3 * WIDTH_B
CHUNK = 64
NBAND = 128
BAND_BLOCKS = 2
ROW_TILE = 512
FF_TILE = 256
LANES = 128
VMEM_LIMIT = 56 * 1024 * 1024

SEG_QA, SEG_KA, SEG_VA = (0, 512), (512, 1024), (1024, 1536)
SEG_QKV, SEG_GATE, SEG_QC, SEG_AB = (1536, 2304), (2304, 2560), (2560, 2816), (2816, 2944)
N_IN_PAD = 2944


def _params(*sem):
    return pltpu.CompilerParams(dimension_semantics=sem, vmem_limit_bytes=VMEM_LIMIT)


def _dot(a, b):
    return jnp.dot(a, b, preferred_element_type=F32)


def _dot_nt(a, b):
    return lax.dot_general(a, b, (((1,), (1,)), ((), ())), preferred_element_type=F32)


def _dot_tn(a, b):
    return lax.dot_general(a, b, (((0,), (0,)), ((), ())), preferred_element_type=F32)


def _dot_split(a, b, parts, left=False):
    acc = None
    rem = a
    for _ in range(parts):
        piece = rem.astype(BF16)
        term = _dot(b, piece) if left else _dot(piece, b)
        acc = term if acc is None else acc + term
        rem = rem - piece.astype(F32)
    return acc


def _sigmoid(x):
    return 1.0 / (1.0 + jnp.exp(-x))


def _silu(x):
    return x * _sigmoid(x)


def _softplus(x):
    return jnp.maximum(x, 0.0) + jnp.log1p(jnp.exp(-jnp.abs(x)))


def _group_ones(width, group=HEAD_DIM):
    i = np.arange(width)
    return jnp.asarray((i[:, None] // group) == (i[None, :] // group), BF16)


def _full(shape):
    nd = len(shape)
    return pl.BlockSpec(shape, lambda *_: (0,) * nd)


def _inproj_kernel(dils, x_ref, g1_ref, w_ref, gq_ref, gk_ref, gc_ref, ga_ref, gcc_ref, *refs):
    nd = len(dils)
    q_refs, k_refs, v_refs = refs[0:nd], refs[nd:2 * nd], refs[2 * nd:3 * nd]
    ka_ref, va_ref, qkv_ref, gate_ref, qc_ref, ab_ref, scr_ref = refs[3 * nd:]
    tm = x_ref.shape[0]
    x = x_ref[...]
    ms = jnp.mean(x * x, axis=-1, keepdims=True)
    xn = (x * lax.rsqrt(ms + EPS) * g1_ref[...]).astype(BF16)

    def seg(s):
        return _dot(xn, w_ref[:, s[0]:s[1]])

    def head_norm(z, ones_ref, gain):
        ss = _dot_split(z * z, ones_ref[...], 2) * (1.0 / HEAD_DIM)
        return z * lax.rsqrt(ss + EPS) * gain

    def emit(z, out_refs):
        n_tiles = WIDTH_A // LANES
        for c in range(n_tiles):
            scr_ref[c] = z[:, c * LANES:(c + 1) * LANES]
        for d, ref in zip(dils, out_refs):
            if d == 1:
                ref[...] = z.astype(BF16)
            else:
                for r in range(d):
                    for c in range(n_tiles):
                        col = r * WIDTH_A + c * LANES
                        ref[:, col:col + LANES] = scr_ref[c, pl.ds(r, tm // d, stride=d), :].astype(BF16)

    scale = HEAD_DIM ** -0.5
    emit(head_norm(seg(SEG_QA), ga_ref, gq_ref[...]) * scale, q_refs)
    ka = head_norm(seg(SEG_KA), ga_ref, gk_ref[...])
    ka_ref[...] = ka
    emit(ka, k_refs)
    va = seg(SEG_VA)
    va_ref[...] = va
    emit(va, v_refs)
    qkv_ref[...] = seg(SEG_QKV)
    gate_ref[...] = seg(SEG_GATE)
    qc_ref[...] = (head_norm(seg(SEG_QC), gcc_ref, gc_ref[...]) * scale).astype(BF16)
    ab_ref[...] = seg(SEG_AB)


def _inproj(x, g1, w_in_p, gq, gk, gc, dils, keep_rows):
    rows = x.shape[0]
    tm = ROW_TILE
    nt = rows // tm
    skip = nt - keep_rows // tm
    row = lambda w: pl.BlockSpec((tm, w), lambda i: (i, 0))
    tail = pl.BlockSpec((tm, WIDTH_A), lambda i: (jnp.maximum(i - skip, 0), 0))
    dil_specs = [pl.BlockSpec((tm // d, d * WIDTH_A), lambda i: (i, 0)) for d in dils]
    dil_shapes = [jax.ShapeDtypeStruct((rows // d, d * WIDTH_A), BF16) for d in dils]
    outs = [(QKV_B, F32), (WIDTH_B, F32), (WIDTH_C, BF16), (LANES, F32)]
    res = pl.pallas_call(
        functools.partial(_inproj_kernel, dils),
        grid=(nt,),
        in_specs=[row(D_MODEL), _full((1, D_MODEL)), _full((D_MODEL, N_IN_PAD)), _full((1, WIDTH_A)),
                  _full((1, WIDTH_A)), _full((1, WIDTH_C)), _full((WIDTH_A, WIDTH_A)), _full((WIDTH_C, WIDTH_C))],
        out_specs=dil_specs * 3 + [tail, tail] + [row(w) for w, _ in outs],
        out_shape=dil_shapes * 3 + [jax.ShapeDtypeStruct((keep_rows, WIDTH_A), F32)] * 2
        + [jax.ShapeDtypeStruct((rows, w), dt) for w, dt in outs],
        scratch_shapes=[pltpu.VMEM((WIDTH_A // LANES, tm, LANES), F32)],
        compiler_params=_params("arbitrary"),
        name="inproj",
    )(x, g1, w_in_p, gq, gk, gc, _group_ones(WIDTH_A), _group_ones(WIDTH_C))
    nd = len(dils)
    return res[0:nd], res[nd:2 * nd], res[2 * nd:3 * nd], res[3 * nd:]


def _alibi_slopes():
    return np.exp2(-8.0 * np.arange(1, N_HEADS_A + 1, dtype=np.float64) / N_HEADS_A)


def _band_bias(dil):
    qi = np.arange(NBAND)[:, None]
    kj = np.arange(2 * NBAND)[None, :]
    delta = qi + NBAND - kj
    in_band = (delta >= 0) & (delta <= NBAND)
    bias = -_alibi_slopes()[:, None, None] * (delta * dil)[None].astype(np.float64)
    general = np.where(in_band[None], bias, NEG)
    first = np.where((in_band & (kj >= NBAND))[None], bias, NEG)
    return jnp.asarray(np.stack([general, first]), F32)


def _attn_band_kernel(q_ref, kp_ref, kc_ref, vp_ref, vc_ref, bias_ref, o_ref, lse_ref):
    lane = lax.broadcasted_iota(jnp.int32, (NBAND, LANES), 1)
    low = lane < HEAD_DIM
    ones = jnp.ones((2 * NBAND, LANES), BF16)
    for blk in range(BAND_BLOCKS):
        first = (pl.program_id(1) == 0).astype(jnp.int32) if blk == 0 else 0
        rows = slice(blk * NBAND, (blk + 1) * NBAND)
        lse = jnp.zeros((NBAND, LANES), F32)
        for pair in range(N_HEADS_A // 2):
            cols = slice(pair * LANES, (pair + 1) * LANES)
            q = q_ref[rows, cols]
            if blk == 0:
                k = jnp.concatenate([kp_ref[:, cols], kc_ref[rows, cols]], axis=0)
                v = jnp.concatenate([vp_ref[:, cols], vc_ref[rows, cols]], axis=0)
            else:
                k = kc_ref[(blk - 1) * NBAND:(blk + 1) * NBAND, cols]
                v = vc_ref[(blk - 1) * NBAND:(blk + 1) * NBAND, cols]
            v_ext = jnp.concatenate([v, ones], axis=1)
            outs = []
            for half in range(2):
                h = 2 * pair + half
                qm = jnp.where(low if half == 0 else ~low, q, jnp.zeros((), BF16))
                s = _dot_nt(qm, k) + bias_ref[first, h]
                m = jnp.max(s, axis=-1, keepdims=True)
                p = jnp.exp(s - m).astype(BF16)
                r = _dot(p, v_ext)
                den = r[:, LANES:]
                outs.append(r[:, :LANES] / den)
                lse = jnp.where(lane == h, m + jnp.log(den), lse)
            o_ref[rows, cols] = jnp.where(low, outs[0], outs[1]).astype(BF16)
        lse_ref[rows, :] = lse


def _attn_band(q_d, k_d, v_d, dil):
    rows = q_d.shape[0]
    step = BAND_BLOCKS * NBAND
    cur = lambda w: pl.BlockSpec((step, w), lambda r, n: (n, r))
    prev = lambda w: pl.BlockSpec((NBAND, w), lambda r, n: (jnp.maximum(n * BAND_BLOCKS - 1, 0), r))
    return pl.pallas_call(
        _attn_band_kernel,
        grid=(dil, rows // step),
        in_specs=[cur(WIDTH_A), prev(WIDTH_A), cur(WIDTH_A), prev(WIDTH_A), cur(WIDTH_A),
                  _full((2, N_HEADS_A, NBAND, 2 * NBAND))],
        out_specs=[cur(WIDTH_A), cur(LANES)],
        out_shape=[jax.ShapeDtypeStruct((rows, dil * WIDTH_A), BF16), jax.ShapeDtypeStruct((rows, dil * LANES), F32)],
        compiler_params=_params("parallel", "arbitrary"),
        name=f"attn_band_d{dil}",
    )(q_d, k_d, k_d, v_d, v_d, _band_bias(dil))


def _memkv_kernel(mem_ref, g_ref, w_ref, gk_ref, ones_ref, mk_ref, mv_ref):
    x = mem_ref[...]
    ms = jnp.mean(x * x, axis=-1, keepdims=True)
    xn = (x * lax.rsqrt(ms + EPS) * g_ref[...]).astype(BF16)
    zk = _dot(xn, w_ref[:, 0:WIDTH_C])
    ss = _dot_split(zk * zk, ones_ref[...], 2) * (1.0 / HEAD_DIM)
    mk_ref[...] = zk * lax.rsqrt(ss + EPS) * gk_ref[...]
    mv_ref[...] = _dot(xn, w_ref[:, WIDTH_C:2 * WIDTH_C])


def _memkv(mem, g, w, gk):
    n = mem.shape[0]
    return pl.pallas_call(
        _memkv_kernel,
        out_shape=[jax.ShapeDtypeStruct((n, WIDTH_C), F32)] * 2,
        compiler_params=pltpu.CompilerParams(vmem_limit_bytes=VMEM_LIMIT),
        name="memkv",
    )(mem, g, w, gk, _group_ones(WIDTH_C))


def _head_spread():
    e = np.zeros((LANES, WIDTH_A), np.float32)
    for h in range(N_HEADS_A):
        e[h, h * HEAD_DIM:(h + 1) * HEAD_DIM] = 1.0
    return jnp.asarray(e, BF16)


def _combine_mem_kernel(dils, *refs):
    nd = len(dils)
    o_refs, l_refs = refs[0:nd], refs[nd:2 * nd]
    qc_ref, mk_ref, mv_ref, e_ref, oa_ref, oc_ref, o_scr, l_scr = refs[2 * nd:]
    tm = oa_ref.shape[0]
    os_, ls = [], []
    for p, d in enumerate(dils):
        if d == 1:
            os_.append(o_refs[p][...].astype(F32))
            ls.append(l_refs[p][...])
            continue
        n_tiles = WIDTH_A // LANES
        for r in range(d):
            for c in range(n_tiles):
                col = r * WIDTH_A + c * LANES
                o_scr[p * n_tiles + c, pl.ds(r, tm // d, stride=d), :] = o_refs[p][:, col:col + LANES].astype(F32)
            l_scr[p, pl.ds(r, tm // d, stride=d), :] = l_refs[p][:, r * LANES:(r + 1) * LANES]
        os_.append(jnp.concatenate([o_scr[p * n_tiles + c] for c in range(n_tiles)], axis=1))
        ls.append(l_scr[p])
    m = functools.reduce(jnp.maximum, ls)
    es = [jnp.exp(l - m) for l in ls]
    tot = functools.reduce(lambda a, b: a + b, es)
    acc = None
    for o, e in zip(os_, es):
        term = o * _dot_split(e / tot, e_ref[...], 2)
        acc = term if acc is None else acc + term
    oa_ref[...] = acc.astype(BF16)

    q = qc_ref[...]
    mk = mk_ref[...].astype(BF16)
    mv = mv_ref[...].astype(BF16)
    for h in range(N_HEADS_C):
        sl = slice(h * HEAD_DIM, (h + 1) * HEAD_DIM)
        s = _dot_nt(q[:, sl], mk[:, sl])
        mx = jnp.max(s, axis=-1, keepdims=True)
        p = jnp.exp(s - mx)
        den_c = jnp.sum(p, axis=-1, keepdims=True)
        oc_ref[:, sl] = (_dot(p.astype(BF16), mv[:, sl]) / den_c).astype(BF16)


def _combine_mem(os_, lses, dils, qc, mk, mv):
    t = qc.shape[0]
    tm = ROW_TILE
    nd = len(dils)
    row = lambda w: pl.BlockSpec((tm, w), lambda i: (i, 0))
    o_specs = [pl.BlockSpec((tm // d, d * WIDTH_A), lambda i: (i, 0)) for d in dils]
    l_specs = [pl.BlockSpec((tm // d, d * LANES), lambda i: (i, 0)) for d in dils]
    return pl.pallas_call(
        functools.partial(_combine_mem_kernel, dils),
        grid=(t // tm,),
        in_specs=o_specs + l_specs + [row(WIDTH_C), _full((N_MEM, WIDTH_C)), _full((N_MEM, WIDTH_C)),
                                     _full((LANES, WIDTH_A))],
        out_specs=[row(WIDTH_A), row(WIDTH_C)],
        out_shape=[jax.ShapeDtypeStruct((t, WIDTH_A), BF16), jax.ShapeDtypeStruct((t, WIDTH_C), BF16)],
        scratch_shapes=[pltpu.VMEM((nd * WIDTH_A // LANES, tm, LANES), F32), pltpu.VMEM((nd, tm, LANES), F32)],
        compiler_params=_params("parallel"),
        name="combine_mem",
    )(*os_, *lses, qc, mk, mv, _head_spread())


def _head_lane_consts():
    e_g = np.zeros((LANES, WIDTH_B), np.float32)
    e_b = np.zeros((LANES, WIDTH_B), np.float32)
    for h in range(N_HEADS_B):
        e_g[h, h * HEAD_DIM:(h + 1) * HEAD_DIM] = 1.0
        e_b[N_HEADS_B + h, h * HEAD_DIM:(h + 1) * HEAD_DIM] = 1.0
    i = np.arange(CHUNK)
    ltri = (i[None, :] <= i[:, None]).astype(np.float32)
    return jnp.asarray(e_g, BF16), jnp.asarray(e_b, BF16), jnp.asarray(ltri, BF16), jnp.ones((CHUNK, CHUNK), BF16)


def _block_diag(x, mask):
    return jnp.where(mask, jnp.concatenate([x] * N_HEADS_B, axis=0), jnp.zeros((), x.dtype))


def _gdn_prep_kernel(x_ref, halo_ref, ab_ref, cw_ref, alog_ref, dtb_ref, ones_ref, eg_ref, eb_ref, ltri_ref, one64_ref,
                     m_ref, attn_ref, qd_ref, kd_ref, vb_ref, kbe_ref, egl_ref, xe_ref):
    tm = x_ref.shape[0]
    halo = jnp.where(pl.program_id(0) > 0, halo_ref[...], 0.0)
    xe_ref[0:8, :] = halo
    xe_ref[8:8 + tm, :] = x_ref[...]
    cw = cw_ref[...]
    y = cw[3:4] * xe_ref[8:8 + tm, :] + cw[2:3] * xe_ref[7:7 + tm, :] + cw[1:2] * xe_ref[6:6 + tm, :] \
        + cw[0:1] * xe_ref[5:5 + tm, :]
    y = _silu(y)

    def l2n(z):
        return z * lax.rsqrt(_dot_split(z * z, ones_ref[...], 2) + EPS)

    q = l2n(y[:, 0:WIDTH_B]) * (HEAD_DIM ** -0.5)
    k = l2n(y[:, WIDTH_B:2 * WIDTH_B])
    v = y[:, 2 * WIDTH_B:3 * WIDTH_B]
    ab = ab_ref[...]
    g = -jnp.exp(alog_ref[...]) * _softplus(ab + dtb_ref[...])
    gx = _dot_split(g, eg_ref[...], 3)
    bx = _dot_split(_sigmoid(ab), eb_ref[...], 2)

    row = lax.broadcasted_iota(jnp.int32, (CHUNK, WIDTH_B), 0)
    col = lax.broadcasted_iota(jnp.int32, (CHUNK, WIDTH_B), 1) % CHUNK
    diag = row == col
    r4 = lax.broadcasted_iota(jnp.int32, (WIDTH_B, WIDTH_B), 0) // HEAD_DIM
    c4 = lax.broadcasted_iota(jnp.int32, (WIDTH_B, WIDTH_B), 1) // HEAD_DIM
    bd_mask = r4 == c4
    for c in range(tm // CHUNK):
        sl = slice(c * CHUNK, (c + 1) * CHUNK)
        gc = _dot_split(gx[sl], ltri_ref[...], 3, left=True)
        gl = gc[CHUNK - 1:CHUNK, :]
        gcj = _dot_split(jnp.where(diag, gc, 0.0), one64_ref[...], 3, left=True)
        dec = jnp.exp(jnp.where(row >= col, gc - gcj, NEG))
        kc, qc, vc, bc = k[sl], q[sl], v[sl], bx[sl]
        kb = kc * bc
        lhs = jnp.concatenate([kb, qc], axis=0).astype(BF16)
        kkqk = _dot_nt(lhs, _block_diag(kc.astype(BF16), bd_mask))
        m_ref[sl, :] = jnp.where(row > col, kkqk[0:CHUNK] * dec, 0.0)
        attn_ref[sl, :] = (kkqk[CHUNK:2 * CHUNK] * dec).astype(BF16)
        egc = jnp.exp(gc)
        qd_ref[sl, :] = (qc * egc).astype(BF16)
        kd_ref[sl, :] = (kc * jnp.exp(gl - gc)).astype(BF16)
        vb_ref[sl, :] = (vc * bc).astype(BF16)
        kbe_ref[sl, :] = (kb * egc).astype(BF16)
        egl_ref[c:c + 1, :] = jnp.exp(gl)


def _gdn_prep(qkv, ab, conv_w, alog_row, dtb_row):
    t = qkv.shape[0]
    tm = ROW_TILE
    nc = tm // CHUNK
    row = lambda w: pl.BlockSpec((tm, w), lambda i: (i, 0))
    e_g, e_b, ltri, one64 = _head_lane_consts()
    outs = [F32, BF16, BF16, BF16, BF16, BF16]
    return pl.pallas_call(
        _gdn_prep_kernel,
        grid=(t // tm,),
        in_specs=[row(QKV_B), pl.BlockSpec((8, QKV_B), lambda i: (jnp.maximum(i * (tm // 8) - 1, 0), 0)), row(LANES),
                  _full((4, QKV_B)), _full((1, LANES)), _full((1, LANES)), _full((WIDTH_B, WIDTH_B)),
                  _full((LANES, WIDTH_B)), _full((LANES, WIDTH_B)), _full((CHUNK, CHUNK)), _full((CHUNK, CHUNK))],
        out_specs=[row(WIDTH_B)] * 6 + [pl.BlockSpec((nc, WIDTH_B), lambda i: (i, 0))],
        out_shape=[jax.ShapeDtypeStruct((t, WIDTH_B), dt) for dt in outs]
        + [jax.ShapeDtypeStruct((t // CHUNK, WIDTH_B), F32)],
        scratch_shapes=[pltpu.VMEM((tm + 8, QKV_B), F32)],
        compiler_params=_params("parallel"),
        name="gdn_prep",
    )(qkv, qkv, ab, conv_w, alog_row, dtb_row, _group_ones(WIDTH_B), e_g, e_b, ltri, one64)


def _tinv_kernel(m_ref, out_ref, n_ref):
    blk = pl.program_id(0)
    for ii in range(8):
        i = blk * 8 + ii
        for kb in range(8):
            k0 = 8 * kb
            init = tuple(m_ref[ii, k0 + kk] for kk in range(8))

            def body(j, acc, ii=ii, k0=k0):
                mij = m_ref[ii, j]
                return tuple(acc[kk] + mij * n_ref[j, k0 + kk] for kk in range(8))

            acc = lax.fori_loop(k0, jnp.maximum(i, k0), body, init)
            for kk in range(8):
                val = jnp.where(k0 + kk < i, -acc[kk], 0.0)
                n_ref[i, k0 + kk] = val
                out_ref[ii, k0 + kk] = val


def _tinv(mt):
    return pl.pallas_call(
        _tinv_kernel,
        grid=(CHUNK // 8,),
        in_specs=[pl.BlockSpec((8, CHUNK, 8, LANES), lambda i: (i, 0, 0, 0))],
        out_specs=pl.BlockSpec((8, CHUNK, 8, LANES), lambda i: (i, 0, 0, 0)),
        out_shape=jax.ShapeDtypeStruct(mt.shape, F32),
        scratch_shapes=[pltpu.VMEM(mt.shape, F32)],
        compiler_params=_params("arbitrary"),
        name="gdn_tinv",
    )(mt)


def _gdn_scan_kernel(n_ref, vb_ref, kbe_ref, attn_ref, qd_ref, kd_ref, egl_ref, gate_ref, gn_ref, ones_ref,
                     ob_ref, s_out_ref, s_ref):
    tm = n_ref.shape[0]

    @pl.when(pl.program_id(0) == 0)
    def _():
        s_ref[...] = jnp.zeros_like(s_ref)

    row = lax.broadcasted_iota(jnp.int32, (CHUNK, WIDTH_B), 0)
    col = lax.broadcasted_iota(jnp.int32, (CHUNK, WIDTH_B), 1) % CHUNK
    eye = (row == col).astype(F32)
    r4 = lax.broadcasted_iota(jnp.int32, (WIDTH_B, WIDTH_B), 0) // HEAD_DIM
    c4 = lax.broadcasted_iota(jnp.int32, (WIDTH_B, WIDTH_B), 1) // HEAD_DIM
    bd_mask = r4 == c4
    s = s_ref[...]
    for c in range(tm // CHUNK):
        sl = slice(c * CHUNK, (c + 1) * CHUNK)
        tinv = (n_ref[sl, :] + eye).astype(BF16)
        u = _dot(tinv, _block_diag(vb_ref[sl, :], bd_mask))
        w = _dot(tinv, _block_diag(kbe_ref[sl, :], bd_mask))
        lhs = jnp.concatenate([w.astype(BF16), qd_ref[sl, :]], axis=0)
        ws = _dot(lhs, s.astype(BF16))
        v_new = (u - ws[0:CHUNK]).astype(BF16)
        o = ws[CHUNK:2 * CHUNK] + _dot(attn_ref[sl, :], _block_diag(v_new, bd_mask))
        s = s * egl_ref[c:c + 1, :] + jnp.where(bd_mask, _dot_tn(kd_ref[sl, :], v_new), 0.0)
        ms = _dot_split(o * o, ones_ref[...], 2) * (1.0 / HEAD_DIM)
        ob_ref[sl, :] = (o * lax.rsqrt(ms + EPS) * gn_ref[...] * _silu(gate_ref[sl, :])).astype(BF16)
    s_ref[...] = s
    s_out_ref[...] = s


def _gdn_scan(n_cat, vb, kbe, attn, qd, kd, egl, gate, gn_row):
    t = n_cat.shape[0]
    tm = ROW_TILE
    nc = tm // CHUNK
    row = lambda w: pl.BlockSpec((tm, w), lambda i: (i, 0))
    return pl.pallas_call(
        _gdn_scan_kernel,
        grid=(t // tm,),
        in_specs=[row(WIDTH_B)] * 6 + [pl.BlockSpec((nc, WIDTH_B), lambda i: (i, 0)), row(WIDTH_B),
                                       _full((1, WIDTH_B)), _full((WIDTH_B, WIDTH_B))],
        out_specs=[row(WIDTH_B), _full((WIDTH_B, WIDTH_B))],
        out_shape=[jax.ShapeDtypeStruct((t, WIDTH_B), BF16), jax.ShapeDtypeStruct((WIDTH_B, WIDTH_B), F32)],
        scratch_shapes=[pltpu.VMEM((WIDTH_B, WIDTH_B), F32)],
        compiler_params=_params("arbitrary"),
        name="gdn_scan",
    )(n_cat, vb, kbe, attn, qd, kd, egl, gate, gn_row, _group_ones(WIDTH_B))


def _gdn_prompt(qkv, ab, gate, conv_w, alog_row, dtb_row, gn_row):
    t = qkv.shape[0]
    nchunk = t // CHUNK
    nsys = nchunk * N_HEADS_B
    nsys_pad = -(-nsys // 1024) * 1024
    assert nsys_pad == 1024
    m, attn, qd, kd, vb, kbe, egl = _gdn_prep(qkv, ab, conv_w, alog_row, dtb_row)
    mt = m.reshape(nchunk, CHUNK, N_HEADS_B, CHUNK).transpose(1, 3, 0, 2).reshape(CHUNK, CHUNK, nsys)
    mt = jnp.pad(mt, ((0, 0), (0, 0), (0, nsys_pad - nsys))).reshape(CHUNK, CHUNK, 8, LANES)
    nt = _tinv(mt).reshape(CHUNK, CHUNK, nsys_pad)[:, :, :nsys]
    n_cat = nt.reshape(CHUNK, CHUNK, nchunk, N_HEADS_B).transpose(2, 0, 3, 1).reshape(t, WIDTH_B)
    ob, s_bd = _gdn_scan(n_cat, vb, kbe, attn, qd, kd, egl, gate, gn_row)
    s_fin = jnp.stack([s_bd[h * HEAD_DIM:(h + 1) * HEAD_DIM, h * HEAD_DIM:(h + 1) * HEAD_DIM] for h in range(N_HEADS_B)])
    return ob, s_fin


def _ffn_kernel(shift, carry_rows, x_ref, oa_ref, ob_ref, oc_ref, wo_ref, g_ref, wup_ref, cw_ref, wdn_ref, c0_ref,
                y_ref, last_ref, carry_ref, ext_ref):
    tm = x_ref.shape[0]

    @pl.when(pl.program_id(0) == 0)
    def _():
        carry_ref[...] = c0_ref[...]

    mix = _dot(oa_ref[...], wo_ref[0:WIDTH_A, :])
    mix = mix + _dot(ob_ref[...], wo_ref[WIDTH_A:WIDTH_A + WIDTH_B, :])
    mix = mix + _dot(oc_ref[...], wo_ref[WIDTH_A + WIDTH_B:, :])
    x = x_ref[...] + mix
    ms = jnp.mean(x * x, axis=-1, keepdims=True)
    xn = (x * lax.rsqrt(ms + EPS) * g_ref[...]).astype(BF16)
    y_ref[...] = x
    tf = FF_TILE
    base = carry_rows
    for j in range(D_FF // tf):
        halves = []
        for half in range(2):
            cols = slice(half * D_FF + j * tf, half * D_FF + (j + 1) * tf)
            up = _dot(xn, wup_ref[:, cols])
            ext_ref[0:base, :] = carry_ref[:, cols]
            ext_ref[base:base + tm, :] = up
            carry_ref[:, cols] = ext_ref[tm:tm + base, :]
            cw = cw_ref[:, cols]
            halves.append(cw[0:1] * ext_ref[base - 2 * shift:base - 2 * shift + tm, :]
                          + cw[1:2] * ext_ref[base - shift:base - shift + tm, :] + cw[2:3] * up)
        hidden = (_silu(halves[0]) * halves[1]).astype(BF16)
        y_ref[...] += _dot(hidden, wdn_ref[j * tf:(j + 1) * tf, :])
    last_ref[...] = carry_ref[...]


def _ffn(x, oa, ob, oc, w_out, g, w_up, conv_w, w_down, carry0, shift):
    rows = x.shape[0]
    tm = ROW_TILE
    carry_rows = carry0.shape[0]
    row = lambda w: pl.BlockSpec((tm, w), lambda i: (i, 0))
    once = lambda shape: pl.BlockSpec(shape, lambda i: (0,) * len(shape), pipeline_mode=pl.Buffered(1))
    return pl.pallas_call(
        functools.partial(_ffn_kernel, shift, carry_rows),
        grid=(rows // tm,),
        in_specs=[row(D_MODEL), row(WIDTH_A), row(WIDTH_B), row(WIDTH_C), once((D_MODEL, D_MODEL)),
                  once((1, D_MODEL)), once((D_MODEL, 2 * D_FF)), once((3, 2 * D_FF)),
                  once((D_FF, D_MODEL)), once((carry_rows, 2 * D_FF))],
        out_specs=[row(D_MODEL), _full((carry_rows, 2 * D_FF))],
        out_shape=[jax.ShapeDtypeStruct((rows, D_MODEL), F32), jax.ShapeDtypeStruct((carry_rows, 2 * D_FF), F32)],
        scratch_shapes=[pltpu.VMEM((carry_rows, 2 * D_FF), F32), pltpu.VMEM((tm + carry_rows, FF_TILE), F32)],
        compiler_params=_params("arbitrary"),
        name=f"ffn_shift{shift}",
    )(x, oa, ob, oc, w_out, g, w_up, conv_w, w_down, carry0)


SAMPLE_GROUP = 2
PAIR_ROWS = 16
NEW_ROWS = 16


def _sample_bias(wbuf, s_len):
    def table(key_pos):
        q_pos = wbuf + np.arange(s_len)
        dist = q_pos[:, None] - key_pos[None, :]
        mult = np.zeros(dist.shape)
        for window, dil in PATTERNS:
            mult += (dist >= 0) & (dist <= window) & (dist % dil == 0)
        logm = np.log(np.maximum(mult, 1))
        bias = -_alibi_slopes()[:, None, None] * np.maximum(dist, 0)[None] + logm[None]
        bias = np.where(mult[None] > 0, bias, NEG)
        out = np.zeros((N_HEADS_A, PAIR_ROWS // 2, key_pos.shape[0]))
        out[:, :s_len] = bias
        return out.reshape(N_HEADS_A // 2, PAIR_ROWS, -1)
    main = table(np.arange(wbuf))
    new = table(np.concatenate([wbuf + np.arange(s_len), np.full(NEW_ROWS - s_len, 10 ** 6)]))
    new[:, :, s_len:] = NEG
    return jnp.asarray(main, F32), jnp.asarray(new, F32)


def _sample_attn_kernel(q_ref, kn_ref, vn_ref, kt_ref, vt_ref, bm_ref, bn_ref, qc_ref, mkt_ref, mvt_ref, oa_ref, oc_ref):
    for b in range(q_ref.shape[0]):
        for pair in range(N_HEADS_A // 2):
            rows = slice(pair * LANES, (pair + 1) * LANES)
            q = q_ref[b, pair]
            s_main = _dot(q, kt_ref[b, rows, :].astype(BF16)) + bm_ref[pair]
            s_new = _dot_nt(q, kn_ref[b, :, rows].astype(BF16)) + bn_ref[pair]
            m = jnp.maximum(jnp.max(s_main, axis=-1, keepdims=True), jnp.max(s_new, axis=-1, keepdims=True))
            p_main = jnp.exp(s_main - m)
            p_new = jnp.exp(s_new - m)
            den = jnp.sum(p_main, axis=-1, keepdims=True) + jnp.sum(p_new, axis=-1, keepdims=True)
            num = _dot_nt(p_main.astype(BF16), vt_ref[b, rows, :].astype(BF16))
            num = num + _dot(p_new.astype(BF16), vn_ref[b, :, rows].astype(BF16))
            oa_ref[b, pair] = num / den
        for pair in range(N_HEADS_C // 2):
            rows = slice(pair * LANES, (pair + 1) * LANES)
            q = qc_ref[b, pair]
            s = _dot(q, mkt_ref[b, rows, :].astype(BF16))
            p = jnp.exp(s - jnp.max(s, axis=-1, keepdims=True))
            num = _dot_nt(p.astype(BF16), mvt_ref[b, rows, :].astype(BF16))
            oc_ref[b, pair] = num / jnp.sum(p, axis=-1, keepdims=True)


def _pair_rows(q, s_len):
    nb, _, nh, hd = q.shape
    q = q.reshape(nb, s_len, nh // 2, 2, hd).transpose(0, 2, 3, 1, 4)
    q = jnp.pad(q, ((0, 0), (0, 0), (0, 0), (0, PAIR_ROWS // 2 - s_len), (0, 0)))
    own = jnp.eye(2, dtype=q.dtype)[None, None, :, None, :, None]
    return (q[:, :, :, :, None, :] * own).reshape(nb, nh // 2, PAIR_ROWS, 2 * hd)


def _unpair_rows(o, s_len):
    nb, npair = o.shape[:2]
    o = o.reshape(nb, npair, 2, PAIR_ROWS // 2, 2, HEAD_DIM)
    o = jnp.stack([o[:, :, half, :s_len, half, :] for half in range(2)], axis=2)
    return o.transpose(0, 3, 1, 2, 4).reshape(nb, s_len, npair * 2 * HEAD_DIM)


def _sample_attn(qa, ka, va, cache_k, cache_v, qc, mem_k, mem_v):
    nb, s_len = qa.shape[:2]
    wbuf = cache_k.shape[1]
    n_mem = mem_k.shape[1]
    assert s_len <= PAIR_ROWS // 2
    g = SAMPLE_GROUP
    bm, bn = _sample_bias(wbuf, s_len)
    transposed = lambda a: a.transpose(0, 2, 3, 1).reshape(nb, a.shape[2] * a.shape[3], a.shape[1])
    pad_new = lambda a: jnp.pad(a, ((0, 0), (0, NEW_ROWS - s_len), (0, 0)))
    per_b = lambda shape: pl.BlockSpec((g,) + shape, lambda i: (i,) + (0,) * len(shape))
    oa, oc = pl.pallas_call(
        _sample_attn_kernel,
        grid=(nb // g,),
        in_specs=[per_b((N_HEADS_A // 2, PAIR_ROWS, LANES)), per_b((NEW_ROWS, WIDTH_A)), per_b((NEW_ROWS, WIDTH_A)),
                  per_b((WIDTH_A, wbuf)), per_b((WIDTH_A, wbuf)), _full(bm.shape), _full(bn.shape),
                  per_b((N_HEADS_C // 2, PAIR_ROWS, LANES)), per_b((WIDTH_C, n_mem)), per_b((WIDTH_C, n_mem))],
        out_specs=[per_b((N_HEADS_A // 2, PAIR_ROWS, LANES)), per_b((N_HEADS_C // 2, PAIR_ROWS, LANES))],
        out_shape=[jax.ShapeDtypeStruct((nb, N_HEADS_A // 2, PAIR_ROWS, LANES), F32),
                   jax.ShapeDtypeStruct((nb, N_HEADS_C // 2, PAIR_ROWS, LANES), F32)],
        compiler_params=_params("parallel"),
        name="sample_attn",
    )(_pair_rows(qa, s_len), pad_new(ka), pad_new(va), transposed(cache_k), transposed(cache_v), bm, bn,
      _pair_rows(qc, s_len), transposed(mem_k), transposed(mem_v))
    return _unpair_rows(oa, s_len).astype(BF16), _unpair_rows(oc, s_len).astype(BF16)


def _sample_gdn_kernel(xe_ref, cw_ref, a_ref, b_ref, alog_ref, dtb_ref, gate_ref, gn_ref, s_ref, o_ref, s_out_ref,
                       q_scr, k_scr, v_scr):
    s_len = o_ref.shape[0]
    s_out_ref[...] = s_ref[...]
    for t in range(s_len):
        for part, scr in enumerate((q_scr, k_scr, v_scr)):
            y = sum(cw_ref[j, part] * xe_ref[t + j, part] for j in range(4))
            y = _silu(y)
            if part < 2:
                y = y * lax.rsqrt(jnp.sum(y * y, axis=0, keepdims=True) + EPS)
            if part == 0:
                y = y * (HEAD_DIM ** -0.5)
            scr[...] = y
        decay = jnp.exp(-jnp.exp(alog_ref[...]) * _softplus(a_ref[t] + dtb_ref[...]))
        beta = _sigmoid(b_ref[t])

        def read_body(dk, r):
            return r + k_scr[pl.ds(dk, 1), :] * s_out_ref[dk]

        r = lax.fori_loop(0, HEAD_DIM, read_body, jnp.zeros(v_scr.shape, F32), unroll=8)
        v_new = beta * (v_scr[...] - decay * r)

        def write_body(dk, o):
            s_new = decay * s_out_ref[dk] + k_scr[pl.ds(dk, 1), :] * v_new
            s_out_ref[dk] = s_new
            return o + q_scr[pl.ds(dk, 1), :] * s_new

        o = lax.fori_loop(0, HEAD_DIM, write_body, jnp.zeros(v_scr.shape, F32), unroll=8)
        ms = jnp.mean(o * o, axis=0, keepdims=True)
        o_ref[t] = o * lax.rsqrt(ms + EPS) * gn_ref[...] * _silu(gate_ref[t])


def _sample_gdn(xe, conv_w, a_b, b_b, a_log, dt_bias, gate, out_norm, state):
    nb, ext, _ = xe.shape
    s_len = ext - 3
    nh, hd = N_HEADS_B, HEAD_DIM
    xe_t = xe.reshape(nb, ext, 3, nh, hd).transpose(1, 2, 3, 4, 0)
    cw_t = jnp.broadcast_to(conv_w.reshape(4, 3, nh, hd)[..., None], (4, 3, nh, hd, nb))
    a_t = a_b.transpose(1, 2, 0).reshape(s_len, nh, 1, nb)
    b_t = b_b.transpose(1, 2, 0).reshape(s_len, nh, 1, nb)
    alog_t = jnp.broadcast_to(a_log.reshape(nh, 1, 1), (nh, 1, nb))
    dtb_t = jnp.broadcast_to(dt_bias.reshape(nh, 1, 1), (nh, 1, nb))
    gate_t = gate.reshape(nb, s_len, nh, hd).transpose(1, 2, 3, 0)
    gn_t = jnp.broadcast_to(out_norm.reshape(1, hd, 1), (1, hd, nb))
    s_t = state.transpose(1, 2, 3, 0)
    o_t, s_new = pl.pallas_call(
        _sample_gdn_kernel,
        grid=(nh,),
        in_specs=[pl.BlockSpec((ext, 3, None, hd, nb), lambda h: (0, 0, h, 0, 0)),
                  pl.BlockSpec((4, 3, None, hd, nb), lambda h: (0, 0, h, 0, 0)),
                  pl.BlockSpec((s_len, None, 1, nb), lambda h: (0, h, 0, 0)),
                  pl.BlockSpec((s_len, None, 1, nb), lambda h: (0, h, 0, 0)),
                  pl.BlockSpec((None, 1, nb), lambda h: (h, 0, 0)),
                  pl.BlockSpec((None, 1, nb), lambda h: (h, 0, 0)),
                  pl.BlockSpec((s_len, None, hd, nb), lambda h: (0, h, 0, 0)),
                  pl.BlockSpec((None, hd, nb), lambda h: (0, 0, 0)),
                  pl.BlockSpec((None, hd, hd, nb), lambda h: (h, 0, 0, 0))],
        out_specs=[pl.BlockSpec((s_len, None, hd, nb), lambda h: (0, h, 0, 0)),
                   pl.BlockSpec((None, hd, hd, nb), lambda h: (h, 0, 0, 0))],
        out_shape=[jax.ShapeDtypeStruct((s_len, nh, hd, nb), F32), jax.ShapeDtypeStruct((nh, hd, hd, nb), F32)],
        scratch_shapes=[pltpu.VMEM((hd, nb), F32)] * 3,
        compiler_params=_params("parallel"),
        name="sample_gdn",
    )(xe_t, cw_t, a_t, b_t, alog_t, dtb_t, gate_t, gn_t, s_t)
    ob = o_t.transpose(3, 0, 1, 2).reshape(nb * s_len, nh * hd).astype(BF16)
    return ob, s_new.transpose(3, 0, 1, 2)


def _permute_w_in(w_in):
    a_start = 3 * WIDTH_A + QKV_B + WIDTH_B
    qc_start = a_start + 2 * N_HEADS_B
    pad = jnp.zeros((w_in.shape[0], N_IN_PAD - w_in.shape[1]), w_in.dtype)
    return jnp.concatenate([w_in[:, :a_start], w_in[:, qc_start:], w_in[:, a_start:qc_start], pad], axis=1).astype(BF16)


def _lane_row(v, width=LANES):
    return jnp.zeros((1, width), F32).at[0, :v.shape[0]].set(v)


def kernel(x_prompt, x_sample, cache_win_k, cache_win_v, state_gdn, state_gdn_conv, state_ffn_conv, cache_mem_k,
           cache_mem_v, mem_prompt, norm1_g, w_in, q_norm_a, k_norm_a, conv_b_w, a_log_b, dt_bias_b, out_norm_b,
           mem_norm_g, w_mem_kv, q_norm_c, k_norm_c, w_out, norm2_g, w_up, conv_ffn_w, w_down):
    depth = norm1_g.shape[0]
    assert depth == 1 and x_prompt.shape[0] == 1
    l = 0
    t_p = x_prompt.shape[1]
    nb, s_len = x_sample.shape[:2]
    xp = x_prompt.reshape(t_p, D_MODEL)
    xs = x_sample.reshape(nb * s_len, D_MODEL)

    w_in_p = _permute_w_in(w_in[l])
    w_out_b = w_out[l].astype(BF16)
    w_up_b = w_up[l].astype(BF16)
    w_down_b = w_down[l].astype(BF16)
    g1 = norm1_g[l].reshape(1, D_MODEL)
    g2 = norm2_g[l].reshape(1, D_MODEL)
    gq = jnp.tile(q_norm_a[l], N_HEADS_A).reshape(1, WIDTH_A)
    gk = jnp.tile(k_norm_a[l], N_HEADS_A).reshape(1, WIDTH_A)
    gqc = jnp.tile(q_norm_c[l], N_HEADS_C).reshape(1, WIDTH_C)
    gkc = jnp.tile(k_norm_c[l], N_HEADS_C).reshape(1, WIDTH_C)
    gn_row = jnp.tile(out_norm_b[l], N_HEADS_B).reshape(1, WIDTH_B)

    n_keep = min(MAX_WINDOW, t_p)
    q_ds, k_ds, v_ds, (ka, va, qkv, gate, qc, ab) = _inproj(xp, g1, w_in_p, gq, gk, gqc, DILATIONS, n_keep)
    parts = [_attn_band(q_d, k_d, v_d, dil) for q_d, k_d, v_d, dil in zip(q_ds, k_ds, v_ds, DILATIONS)]
    mk, mv = _memkv(mem_prompt[0], mem_norm_g[l].reshape(1, D_MODEL), w_mem_kv[l].astype(BF16), gkc)
    oa, oc = _combine_mem([p[0] for p in parts], [p[1] for p in parts], DILATIONS, qc, mk, mv)
    ob, gdn_p = _gdn_prompt(qkv, ab, gate, conv_b_w[l], _lane_row(a_log_b[l]), _lane_row(dt_bias_b[l]), gn_row)
    y_p, last_p = _ffn(xp, oa, ob, oc, w_out_b, g2, w_up_b, conv_ffn_w[l], w_down_b, jnp.zeros((8, 2 * D_FF), F32), 1)
    win_k_p = ka.reshape(1, 1, n_keep, N_HEADS_A, HEAD_DIM)
    win_v_p = va.reshape(1, 1, n_keep, N_HEADS_A, HEAD_DIM)
    gconv_p = qkv[t_p - 3:].reshape(1, 1, 3, QKV_B)
    fconv_p = last_p[6:8].reshape(1, 1, 2, 2 * D_FF)

    rows_s = nb * s_len
    (qa_s,), _, _, (ka_s, va_s, qkv_s, gate_s, qc_s, ab_s) = _inproj(xs, g1, w_in_p, gq, gk, gqc, (1,), rows_s)
    b3 = lambda a: a.reshape(nb, s_len, a.shape[-1])
    heads = lambda a: a.reshape(nb, s_len, -1, HEAD_DIM)
    oa_s, oc_s = _sample_attn(heads(qa_s), b3(ka_s), b3(va_s), cache_win_k[l], cache_win_v[l], heads(qc_s),
                              cache_mem_k[l], cache_mem_v[l])
    xe_s = jnp.concatenate([state_gdn_conv[l], b3(qkv_s)], axis=1)
    ab3 = b3(ab_s)
    ob_s, gdn_s = _sample_gdn(xe_s, conv_b_w[l], ab3[..., 0:N_HEADS_B], ab3[..., N_HEADS_B:2 * N_HEADS_B], a_log_b[l],
                              dt_bias_b[l], b3(gate_s), out_norm_b[l], state_gdn[l])
    time_major = lambda a: a.reshape(nb, s_len, a.shape[-1]).transpose(1, 0, 2).reshape(rows_s, a.shape[-1])
    carry_s = state_ffn_conv[l].transpose(1, 0, 2).reshape(2 * nb, 2 * D_FF)
    y_t, last_s = _ffn(time_major(xs), time_major(oa_s), time_major(ob_s), time_major(oc_s), w_out_b, g2, w_up_b,
                       conv_ffn_w[l], w_down_b, carry_s, nb)
    y_s = y_t.reshape(s_len, nb, D_MODEL).transpose(1, 0, 2)
    fconv_s = last_s.reshape(2, nb, 2 * D_FF).transpose(1, 0, 2)[None]

    return (y_p.reshape(1, t_p, D_MODEL), y_s,
            win_k_p, win_v_p,
            ka_s.reshape(1, nb, s_len, N_HEADS_A, HEAD_DIM), va_s.reshape(1, nb, s_len, N_HEADS_A, HEAD_DIM),
            gdn_p[None, None], gdn_s[None],
            gconv_p, xe_s[:, -3:][None],
            fconv_p, fconv_s,
            mk.reshape(1, 1, N_MEM, N_HEADS_C, HEAD_DIM), mv.reshape(1, 1, N_MEM, N_HEADS_C, HEAD_DIM))
```

```python
import functools

import numpy as np
import jax
import jax.numpy as jnp
from jax import lax
from jax.experimental import pallas as pl
from jax.experimental.pallas import tpu as pltpu

F32 = jnp.float32
BF16 = jnp.bfloat16
EPS = 1e-6
NEG = -1e30

D_MODEL = 1024
HEAD_DIM = 64
N_HEADS_A = 8
PATTERNS = ((128, 1), (512, 4), (2048, 16))
DILATIONS = tuple(d for _, d in PATTERNS)
MAX_WINDOW = 2048
N_HEADS_B = 4
N_HEADS_C = 4
N_MEM = 256
D_FF = 2816
WIDTH_A = N_HEADS_A * HEAD_DIM
WIDTH_B = N_HEADS_B * HEAD_DIM
WIDTH_C = N_HEADS_C * HEAD_DIM
QKV_B = 3 * WIDTH_B
CHUNK = 64
NBAND = 128
BAND_BLOCKS = 4
ROW_TILE = 512
FF_TILE = 256
LANES = 128
VMEM_LIMIT = 56 * 1024 * 1024

SEG_QA, SEG_KA, SEG_VA = (0, 512), (512, 1024), (1024, 1536)
SEG_QKV, SEG_GATE, SEG_QC, SEG_AB = (1536, 2304), (2304, 2560), (2560, 2816), (2816, 2944)
N_IN_PAD = 2944


def _params(*sem):
    return pltpu.CompilerParams(dimension_semantics=sem, vmem_limit_bytes=VMEM_LIMIT)


def _dot(a, b):
    return jnp.dot(a, b, preferred_element_type=F32)


def _dot_nt(a, b):
    return lax.dot_general(a, b, (((1,), (1,)), ((), ())), preferred_element_type=F32)


def _dot_tn(a, b):
    return lax.dot_general(a, b, (((0,), (0,)), ((), ())), preferred_element_type=F32)


def _dot_split(a, b, parts, left=False):
    acc = None
    rem = a
    for _ in range(parts):
        piece = rem.astype(BF16)
        term = _dot(b, piece) if left else _dot(piece, b)
        acc = term if acc is None else acc + term
        rem = rem - piece.astype(F32)
    return acc


def _sigmoid(x):
    return 1.0 / (1.0 + jnp.exp(-x))


def _silu(x):
    return x * _sigmoid(x)


def _softplus(x):
    return jnp.maximum(x, 0.0) + jnp.log1p(jnp.exp(-jnp.abs(x)))


def _group_ones(width, group=HEAD_DIM):
    i = np.arange(width)
    return jnp.asarray((i[:, None] // group) == (i[None, :] // group), BF16)


def _full(shape):
    nd = len(shape)
    return pl.BlockSpec(shape, lambda *_: (0,) * nd)


def _inproj_kernel(dils, x_ref, g1_ref, w_ref, gq_ref, gk_ref, gc_ref, ga_ref, gcc_ref, *refs):
    nd = len(dils)
    q_refs, k_refs, v_refs = refs[0:nd], refs[nd:2 * nd], refs[2 * nd:3 * nd]
    ka_ref, va_ref, qkv_ref, gate_ref, qc_ref, ab_ref, scr_ref = refs[3 * nd:]
    tm = x_ref.shape[0]
    x = x_ref[...]
    ms = jnp.mean(x * x, axis=-1, keepdims=True)
    xn = (x * lax.rsqrt(ms + EPS) * g1_ref[...]).astype(BF16)

    def seg(s):
        return _dot(xn, w_ref[:, s[0]:s[1]])

    def head_norm(z, ones_ref, gain):
        ss = _dot_split(z * z, ones_ref[...], 2) * (1.0 / HEAD_DIM)
        return z * lax.rsqrt(ss + EPS) * gain

    def emit(z, out_refs):
        n_tiles = WIDTH_A // LANES
        for c in range(n_tiles):
            scr_ref[c] = z[:, c * LANES:(c + 1) * LANES]
        for d, ref in zip(dils, out_refs):
            if d == 1:
                ref[...] = z.astype(BF16)
            else:
                for r in range(d):
                    for c in range(n_tiles):
                        col = r * WIDTH_A + c * LANES
                        ref[:, col:col + LANES] = scr_ref[c, pl.ds(r, tm // d, stride=d), :].astype(BF16)

    scale = HEAD_DIM ** -0.5
    emit(head_norm(seg(SEG_QA), ga_ref, gq_ref[...]) * scale, q_refs)
    ka = head_norm(seg(SEG_KA), ga_ref, gk_ref[...])
    ka_ref[...] = ka
    emit(ka, k_refs)
    va = seg(SEG_VA)
    va_ref[...] = va
    emit(va, v_refs)
    qkv_ref[...] = seg(SEG_QKV)
    gate_ref[...] = seg(SEG_GATE)
    qc_ref[...] = (head_norm(seg(SEG_QC), gcc_ref, gc_ref[...]) * scale).astype(BF16)
    ab_ref[...] = seg(SEG_AB)


def _inproj(x, g1, w_in_p, gq, gk, gc, dils, keep_rows):
    rows = x.shape[0]
    tm = ROW_TILE
    nt = rows // tm
    skip = nt - keep_rows // tm
    row = lambda w: pl.BlockSpec((tm, w), lambda i: (i, 0))
    tail = pl.BlockSpec((tm, WIDTH_A), lambda i: (jnp.maximum(i - skip, 0), 0))
    dil_specs = [pl.BlockSpec((tm // d, d * WIDTH_A), lambda i: (i, 0)) for d in dils]
    dil_shapes = [jax.ShapeDtypeStruct((rows // d, d * WIDTH_A), BF16) for d in dils]
    outs = [(QKV_B, F32), (WIDTH_B, F32), (WIDTH_C, BF16), (LANES, F32)]
    res = pl.pallas_call(
        functools.partial(_inproj_kernel, dils),
        grid=(nt,),
        in_specs=[row(D_MODEL), _full((1, D_MODEL)), _full((D_MODEL, N_IN_PAD)), _full((1, WIDTH_A)),
                  _full((1, WIDTH_A)), _full((1, WIDTH_C)), _full((WIDTH_A, WIDTH_A)), _full((WIDTH_C, WIDTH_C))],
        out_specs=dil_specs * 3 + [tail, tail] + [row(w) for w, _ in outs],
        out_shape=dil_shapes * 3 + [jax.ShapeDtypeStruct((keep_rows, WIDTH_A), F32)] * 2
        + [jax.ShapeDtypeStruct((rows, w), dt) for w, dt in outs],
        scratch_shapes=[pltpu.VMEM((WIDTH_A // LANES, tm, LANES), F32)],
        compiler_params=_params("arbitrary"),
        name="inproj",
    )(x, g1, w_in_p, gq, gk, gc, _group_ones(WIDTH_A), _group_ones(WIDTH_C))
    nd = len(dils)
    return res[0:nd], res[nd:2 * nd], res[2 * nd:3 * nd], res[3 * nd:]


def _alibi_slopes():
    return np.exp2(-8.0 * np.arange(1, N_HEADS_A + 1, dtype=np.float64) / N_HEADS_A)


def _band_bias(dil):
    qi = np.arange(NBAND)[:, None]
    kj = np.arange(2 * NBAND)[None, :]
    delta = qi + NBAND - kj
    in_band = (delta >= 0) & (delta <= NBAND)
    bias = -_alibi_slopes()[:, None, None] * (delta * dil)[None].astype(np.float64)
    general = np.where(in_band[None], bias, NEG)
    first = np.where((in_band & (kj >= NBAND))[None], bias, NEG)
    return jnp.asarray(np.stack([general, first]), F32)


def _attn_band_kernel(q_ref, kp_ref, kc_ref, vp_ref, vc_ref, bias_ref, o_ref, lse_ref):
    lane = lax.broadcasted_iota(jnp.int32, (NBAND, LANES), 1)
    low = lane < HEAD_DIM
    ones = jnp.ones((2 * NBAND, LANES), BF16)
    for blk in range(BAND_BLOCKS):
        first = (pl.program_id(1) == 0).astype(jnp.int32) if blk == 0 else 0
        rows = slice(blk * NBAND, (blk + 1) * NBAND)
        lse = jnp.zeros((NBAND, LANES), F32)
        for pair in range(N_HEADS_A // 2):
            cols = slice(pair * LANES, (pair + 1) * LANES)
            q = q_ref[rows, cols]
            if blk == 0:
                k = jnp.concatenate([kp_ref[:, cols], kc_ref[rows, cols]], axis=0)
                v = jnp.concatenate([vp_ref[:, cols], vc_ref[rows, cols]], axis=0)
            else:
                k = kc_ref[(blk - 1) * NBAND:(blk + 1) * NBAND, cols]
                v = vc_ref[(blk - 1) * NBAND:(blk + 1) * NBAND, cols]
            v_ext = jnp.concatenate([v, ones], axis=1)
            outs = []
            for half in range(2):
                h = 2 * pair + half
                qm = jnp.where(low if half == 0 else ~low, q, jnp.zeros((), BF16))
                s = _dot_nt(qm, k) + bias_ref[first, h]
                m = jnp.max(s, axis=-1, keepdims=True)
                p = jnp.exp(s - m).astype(BF16)
                r = _dot(p, v_ext)
                den = r[:, LANES:]
                outs.append(r[:, :LANES] / den)
                lse = jnp.where(lane == h, m + jnp.log(den), lse)
            o_ref[rows, cols] = jnp.where(low, outs[0], outs[1]).astype(BF16)
        lse_ref[rows, :] = lse


def _attn_band(q_d, k_d, v_d, dil):
    rows = q_d.shape[0]
    step = BAND_BLOCKS * NBAND
    cur = lambda w: pl.BlockSpec((step, w), lambda r, n: (n, r))
    prev = lambda w: pl.BlockSpec((NBAND, w), lambda r, n: (jnp.maximum(n * BAND_BLOCKS - 1, 0), r))
    return pl.pallas_call(
        _attn_band_kernel,
        grid=(dil, rows // step),
        in_specs=[cur(WIDTH_A), prev(WIDTH_A), cur(WIDTH_A), prev(WIDTH_A), cur(WIDTH_A),
                  _full((2, N_HEADS_A, NBAND, 2 * NBAND))],
        out_specs=[cur(WIDTH_A), cur(LANES)],
        out_shape=[jax.ShapeDtypeStruct((rows, dil * WIDTH_A), BF16), jax.ShapeDtypeStruct((rows, dil * LANES), F32)],
        compiler_params=_params("parallel", "arbitrary"),
        name=f"attn_band_d{dil}",
    )(q_d, k_d, k_d, v_d, v_d, _band_bias(dil))


def _memkv_kernel(mem_ref, g_ref, w_ref, gk_ref, ones_ref, mk_ref, mv_ref):
    x = mem_ref[...]
    ms = jnp.mean(x * x, axis=-1, keepdims=True)
    xn = (x * lax.rsqrt(ms + EPS) * g_ref[...]).astype(BF16)
    zk = _dot(xn, w_ref[:, 0:WIDTH_C])
    ss = _dot_split(zk * zk, ones_ref[...], 2) * (1.0 / HEAD_DIM)
    mk_ref[...] = zk * lax.rsqrt(ss + EPS) * gk_ref[...]
    mv_ref[...] = _dot(xn, w_ref[:, WIDTH_C:2 * WIDTH_C])


def _memkv(mem, g, w, gk):
    n = mem.shape[0]
    return pl.pallas_call(
        _memkv_kernel,
        out_shape=[jax.ShapeDtypeStruct((n, WIDTH_C), F32)] * 2,
        compiler_params=pltpu.CompilerParams(vmem_limit_bytes=VMEM_LIMIT),
        name="memkv",
    )(mem, g, w, gk, _group_ones(WIDTH_C))


def _head_spread():
    e = np.zeros((LANES, WIDTH_A), np.float32)
    for h in range(N_HEADS_A):
        e[h, h * HEAD_DIM:(h + 1) * HEAD_DIM] = 1.0
    return jnp.asarray(e, BF16)


def _combine_mem_kernel(dils, *refs):
    nd = len(dils)
    o_refs, l_refs = refs[0:nd], refs[nd:2 * nd]
    qc_ref, mk_ref, mv_ref, e_ref, oa_ref, oc_ref, o_scr, l_scr = refs[2 * nd:]
    tm = oa_ref.shape[0]
    os_, ls = [], []
    for p, d in enumerate(dils):
        if d == 1:
            os_.append(o_refs[p][...].astype(F32))
            ls.append(l_refs[p][...])
            continue
        n_tiles = WIDTH_A // LANES
        for r in range(d):
            for c in range(n_tiles):
                col = r * WIDTH_A + c * LANES
                o_scr[p * n_tiles + c, pl.ds(r, tm // d, stride=d), :] = o_refs[p][:, col:col + LANES].astype(F32)
            l_scr[p, pl.ds(r, tm // d, stride=d), :] = l_refs[p][:, r * LANES:(r + 1) * LANES]
        os_.append(jnp.concatenate([o_scr[p * n_tiles + c] for c in range(n_tiles)], axis=1))
        ls.append(l_scr[p])
    m = functools.reduce(jnp.maximum, ls)
    es = [jnp.exp(l - m) for l in ls]
    tot = functools.reduce(lambda a, b: a + b, es)
    acc = None
    for o, e in zip(os_, es):
        term = o * _dot_split(e / tot, e_ref[...], 2)
        acc = term if acc is None else acc + term
    oa_ref[...] = acc.astype(BF16)

    q = qc_ref[...]
    mk = mk_ref[...].astype(BF16)
    mv = mv_ref[...].astype(BF16)
    for h in range(N_HEADS_C):
        sl = slice(h * HEAD_DIM, (h + 1) * HEAD_DIM)
        s = _dot_nt(q[:, sl], mk[:, sl])
        mx = jnp.max(s, axis=-1, keepdims=True)
        p = jnp.exp(s - mx)
        den_c = jnp.sum(p, axis=-1, keepdims=True)
        oc_ref[:, sl] = (_dot(p.astype(BF16), mv[:, sl]) / den_c).astype(BF16)


def _combine_mem(os_, lses, dils, qc, mk, mv):
    t = qc.shape[0]
    tm = ROW_TILE
    nd = len(dils)
    row = lambda w: pl.BlockSpec((tm, w), lambda i: (i, 0))
    o_specs = [pl.BlockSpec((tm // d, d * WIDTH_A), lambda i: (i, 0)) for d in dils]
    l_specs = [pl.BlockSpec((tm // d, d * LANES), lambda i: (i, 0)) for d in dils]
    return pl.pallas_call(
        functools.partial(_combine_mem_kernel, dils),
        grid=(t // tm,),
        in_specs=o_specs + l_specs + [row(WIDTH_C), _full((N_MEM, WIDTH_C)), _full((N_MEM, WIDTH_C)),
                                     _full((LANES, WIDTH_A))],
        out_specs=[row(WIDTH_A), row(WIDTH_C)],
        out_shape=[jax.ShapeDtypeStruct((t, WIDTH_A), BF16), jax.ShapeDtypeStruct((t, WIDTH_C), BF16)],
        scratch_shapes=[pltpu.VMEM((nd * WIDTH_A // LANES, tm, LANES), F32), pltpu.VMEM((nd, tm, LANES), F32)],
        compiler_params=_params("parallel"),
        name="combine_mem",
    )(*os_, *lses, qc, mk, mv, _head_spread())


def _head_lane_consts():
    e_g = np.zeros((LANES, WIDTH_B), np.float32)
    e_b = np.zeros((LANES, WIDTH_B), np.float32)
    for h in range(N_HEADS_B):
        e_g[h, h * HEAD_DIM:(h + 1) * HEAD_DIM] = 1.0
        e_b[N_HEADS_B + h, h * HEAD_DIM:(h + 1) * HEAD_DIM] = 1.0
    i = np.arange(CHUNK)
    ltri = (i[None, :] <= i[:, None]).astype(np.float32)
    return jnp.asarray(e_g, BF16), jnp.asarray(e_b, BF16), jnp.asarray(ltri, BF16), jnp.ones((CHUNK, CHUNK), BF16)


def _block_diag(x, mask):
    return jnp.where(mask, jnp.concatenate([x] * N_HEADS_B, axis=0), jnp.zeros((), x.dtype))


def _gdn_prep_kernel(x_ref, halo_ref, ab_ref, cw_ref, alog_ref, dtb_ref, ones_ref, eg_ref, eb_ref, ltri_ref, one64_ref,
                     m_ref, attn_ref, qd_ref, kd_ref, vb_ref, kbe_ref, egl_ref, xe_ref, m_scr):
    tm = x_ref.shape[0]
    nc = tm // CHUNK
    halo = jnp.where(pl.program_id(0) > 0, halo_ref[...], 0.0)
    xe_ref[0:8, :] = halo
    xe_ref[8:8 + tm, :] = x_ref[...]
    cw = cw_ref[...]
    y = cw[3:4] * xe_ref[8:8 + tm, :] + cw[2:3] * xe_ref[7:7 + tm, :] + cw[1:2] * xe_ref[6:6 + tm, :] \
        + cw[0:1] * xe_ref[5:5 + tm, :]
    y = _silu(y)

    def l2n(z):
        return z * lax.rsqrt(_dot_split(z * z, ones_ref[...], 2) + EPS)

    q = l2n(y[:, 0:WIDTH_B]) * (HEAD_DIM ** -0.5)
    k = l2n(y[:, WIDTH_B:2 * WIDTH_B])
    v = y[:, 2 * WIDTH_B:3 * WIDTH_B]
    ab = ab_ref[...]
    g = -jnp.exp(alog_ref[...]) * _softplus(ab + dtb_ref[...])
    gx = _dot_split(g, eg_ref[...], 2)
    bx = _dot_split(_sigmoid(ab), eb_ref[...], 2)

    row = lax.broadcasted_iota(jnp.int32, (CHUNK, WIDTH_B), 0)
    col = lax.broadcasted_iota(jnp.int32, (CHUNK, WIDTH_B), 1) % CHUNK
    diag = row == col
    r4 = lax.broadcasted_iota(jnp.int32, (WIDTH_B, WIDTH_B), 0) // HEAD_DIM
    c4 = lax.broadcasted_iota(jnp.int32, (WIDTH_B, WIDTH_B), 1) // HEAD_DIM
    bd_mask = r4 == c4
    for c in range(tm // CHUNK):
        sl = slice(c * CHUNK, (c + 1) * CHUNK)
        gc = _dot_split(gx[sl], ltri_ref[...], 2, left=True)
        gl = gc[CHUNK - 1:CHUNK, :]
        gcj = _dot_split(jnp.where(diag, gc, 0.0), one64_ref[...], 2, left=True)
        dec = jnp.exp(jnp.where(row >= col, gc - gcj, NEG))
        kc, qc, vc, bc = k[sl], q[sl], v[sl], bx[sl]
        kb = kc * bc
        lhs = jnp.concatenate([kb, qc], axis=0).astype(BF16)
        kkqk = _dot_nt(lhs, _block_diag(kc.astype(BF16), bd_mask))
        m_c = jnp.where(row > col, kkqk[0:CHUNK] * dec, 0.0)
        for lt in range(WIDTH_B // LANES):
            m_scr[pl.ds(lt * nc + c, CHUNK, stride=nc * WIDTH_B // LANES), :] = m_c[:, lt * LANES:(lt + 1) * LANES]
        attn_ref[sl, :] = (kkqk[CHUNK:2 * CHUNK] * dec).astype(BF16)
        egc = jnp.exp(gc)
        qd_ref[sl, :] = (qc * egc).astype(BF16)
        kd_ref[sl, :] = (kc * jnp.exp(gl - gc)).astype(BF16)
        vb_ref[sl, :] = (vc * bc).astype(BF16)
        kbe_ref[sl, :] = (kb * egc).astype(BF16)
        egl_ref[c:c + 1, :] = jnp.exp(gl)
    m_ref[...] = m_scr[...].reshape(m_ref.shape)


def _gdn_prep(qkv, ab, conv_w, alog_row, dtb_row):
    t = qkv.shape[0]
    tm = ROW_TILE
    nc = tm // CHUNK
    assert nc == 8
    n_lt = WIDTH_B // LANES
    row = lambda w: pl.BlockSpec((tm, w), lambda i: (i, 0))
    m_spec = pl.BlockSpec((CHUNK, n_lt, nc, LANES), lambda i: (0, 0, i, 0))
    e_g, e_b, ltri, one64 = _head_lane_consts()
    outs = [BF16, BF16, BF16, BF16, BF16]
    return pl.pallas_call(
        _gdn_prep_kernel,
        grid=(t // tm,),
        in_specs=[row(QKV_B), pl.BlockSpec((8, QKV_B), lambda i: (jnp.maximum(i * (tm // 8) - 1, 0), 0)), row(LANES),
                  _full((4, QKV_B)), _full((1, LANES)), _full((1, LANES)), _full((WIDTH_B, WIDTH_B)),
                  _full((LANES, WIDTH_B)), _full((LANES, WIDTH_B)), _full((CHUNK, CHUNK)), _full((CHUNK, CHUNK))],
        out_specs=[m_spec] + [row(WIDTH_B)] * 5 + [pl.BlockSpec((nc, WIDTH_B), lambda i: (i, 0))],
        out_shape=[jax.ShapeDtypeStruct((CHUNK, n_lt, t // CHUNK, LANES), F32)]
        + [jax.ShapeDtypeStruct((t, WIDTH_B), dt) for dt in outs]
        + [jax.ShapeDtypeStruct((t // CHUNK, WIDTH_B), F32)],
        scratch_shapes=[pltpu.VMEM((tm + 8, QKV_B), F32), pltpu.VMEM((CHUNK * n_lt * nc, LANES), F32)],
        compiler_params=_params("parallel"),
        name="gdn_prep",
    )(qkv, qkv, ab, conv_w, alog_row, dtb_row, _group_ones(WIDTH_B), e_g, e_b, ltri, one64)


def _tinv_kernel(m_ref, out_ref, n_ref, mrow_ref, nrow_ref):
    nslots = n_ref.shape[2]
    heads_per_tile = LANES // CHUNK
    blk = pl.program_id(0)
    for ii in range(8):
        i = blk * 8 + ii
        for g in range(nslots // N_HEADS_B):
            for lt in range(WIDTH_B // LANES):
                xt = m_ref[ii, lt, g * LANES:(g + 1) * LANES, :].T
                for h2 in range(heads_per_tile):
                    slot = g * N_HEADS_B + lt * heads_per_tile + h2
                    mrow_ref[ii, pl.ds(slot, CHUNK, stride=nslots), :] = xt[h2 * CHUNK:(h2 + 1) * CHUNK, :]
        for kb in range(8):
            k0 = 8 * kb
            init = tuple(mrow_ref[ii, (k0 + kk) * nslots:(k0 + kk + 1) * nslots, :] for kk in range(8))

            def body(j, acc, ii=ii, k0=k0):
                mij = mrow_ref[ii, pl.ds(pl.multiple_of(j * nslots, nslots), nslots), :]
                return tuple(acc[kk] + mij * n_ref[j, k0 + kk] for kk in range(8))

            acc = lax.fori_loop(k0, jnp.maximum(i, k0), body, init)
            for kk in range(8):
                val = jnp.where(k0 + kk < i, -acc[kk], 0.0)
                n_ref[i, k0 + kk] = val
                nrow_ref[(k0 + kk) * nslots:(k0 + kk + 1) * nslots, :] = val
        for g in range(nslots // N_HEADS_B):
            for lt in range(WIDTH_B // LANES):
                slot0 = g * N_HEADS_B + lt * heads_per_tile
                parts = [nrow_ref[pl.ds(slot0 + h2, CHUNK, stride=nslots), :] for h2 in range(heads_per_tile)]
                out_ref[ii, lt, g * LANES:(g + 1) * LANES, :] = jnp.concatenate(parts, axis=0).T


def _tinv(m4):
    _, n_lt, nchunk, _ = m4.shape
    assert nchunk % LANES == 0
    nslots = (nchunk // LANES) * N_HEADS_B
    blk = pl.BlockSpec((8, n_lt, nchunk, LANES), lambda i: (i, 0, 0, 0))
    return pl.pallas_call(
        _tinv_kernel,
        grid=(CHUNK // 8,),
        in_specs=[blk],
        out_specs=blk,
        out_shape=jax.ShapeDtypeStruct(m4.shape, F32),
        scratch_shapes=[pltpu.VMEM((CHUNK, CHUNK, nslots, LANES), F32), pltpu.VMEM((8, CHUNK * nslots, LANES), F32),
                        pltpu.VMEM((CHUNK * nslots, LANES), F32)],
        compiler_params=_params("arbitrary"),
        name="gdn_tinv",
    )(m4)


def _gdn_scan_kernel(n_ref, vb_ref, kbe_ref, attn_ref, qd_ref, kd_ref, egl_ref, gate_ref, gn_ref, ones_ref,
                     ob_ref, s_out_ref, s_ref, n_scr):
    tm = vb_ref.shape[0]
    nc = tm // CHUNK
    n_lt = WIDTH_B // LANES

    @pl.when(pl.program_id(0) == 0)
    def _():
        s_ref[...] = jnp.zeros_like(s_ref)

    n_scr[...] = n_ref[...].reshape(n_scr.shape)

    row = lax.broadcasted_iota(jnp.int32, (CHUNK, WIDTH_B), 0)
    col = lax.broadcasted_iota(jnp.int32, (CHUNK, WIDTH_B), 1) % CHUNK
    eye = (row == col).astype(F32)
    r4 = lax.broadcasted_iota(jnp.int32, (WIDTH_B, WIDTH_B), 0) // HEAD_DIM
    c4 = lax.broadcasted_iota(jnp.int32, (WIDTH_B, WIDTH_B), 1) // HEAD_DIM
    bd_mask = r4 == c4
    s = s_ref[...]
    for c in range(tm // CHUNK):
        sl = slice(c * CHUNK, (c + 1) * CHUNK)
        n_c = jnp.concatenate([n_scr[pl.ds(lt * nc + c, CHUNK, stride=nc * n_lt), :] for lt in range(n_lt)], axis=1)
        tinv = (n_c + eye).astype(BF16)
        u = _dot(tinv, _block_diag(vb_ref[sl, :], bd_mask))
        w = _dot(tinv, _block_diag(kbe_ref[sl, :], bd_mask))
        lhs = jnp.concatenate([w.astype(BF16), qd_ref[sl, :]], axis=0)
        ws = _dot(lhs, s.astype(BF16))
        v_new = (u - ws[0:CHUNK]).astype(BF16)
        o = ws[CHUNK:2 * CHUNK] + _dot(attn_ref[sl, :], _block_diag(v_new, bd_mask))
        s = s * egl_ref[c:c + 1, :] + jnp.where(bd_mask, _dot_tn(kd_ref[sl, :], v_new), 0.0)
        ms = _dot_split(o * o, ones_ref[...], 2) * (1.0 / HEAD_DIM)
        ob_ref[sl, :] = (o * lax.rsqrt(ms + EPS) * gn_ref[...] * _silu(gate_ref[sl, :])).astype(BF16)
    s_ref[...] = s
    s_out_ref[...] = s


def _gdn_scan(n4, vb, kbe, attn, qd, kd, egl, gate, gn_row):
    t = vb.shape[0]
    tm = ROW_TILE
    nc = tm // CHUNK
    n_lt = WIDTH_B // LANES
    row = lambda w: pl.BlockSpec((tm, w), lambda i: (i, 0))
    return pl.pallas_call(
        _gdn_scan_kernel,
        grid=(t // tm,),
        in_specs=[pl.BlockSpec((CHUNK, n_lt, nc, LANES), lambda i: (0, 0, i, 0))] + [row(WIDTH_B)] * 5
        + [pl.BlockSpec((nc, WIDTH_B), lambda i: (i, 0)), row(WIDTH_B), _full((1, WIDTH_B)), _full((WIDTH_B, WIDTH_B))],
        out_specs=[row(WIDTH_B), _full((WIDTH_B, WIDTH_B))],
        out_shape=[jax.ShapeDtypeStruct((t, WIDTH_B), BF16), jax.ShapeDtypeStruct((WIDTH_B, WIDTH_B), F32)],
        scratch_shapes=[pltpu.VMEM((WIDTH_B, WIDTH_B), F32), pltpu.VMEM((CHUNK * n_lt * nc, LANES), F32)],
        compiler_params=_params("arbitrary"),
        name="gdn_scan",
    )(n4, vb, kbe, attn, qd, kd, egl, gate, gn_row, _group_ones(WIDTH_B))


def _gdn_prompt(qkv, ab, gate, conv_w, alog_row, dtb_row, gn_row):
    m4, attn, qd, kd, vb, kbe, egl = _gdn_prep(qkv, ab, conv_w, alog_row, dtb_row)
    ob, s_bd = _gdn_scan(_tinv(m4), vb, kbe, attn, qd, kd, egl, gate, gn_row)
    s_fin = jnp.stack([s_bd[h * HEAD_DIM:(h + 1) * HEAD_DIM, h * HEAD_DIM:(h + 1) * HEAD_DIM] for h in range(N_HEADS_B)])
    return ob, s_fin


def _ffn_kernel(shift, carry_rows, x_ref, oa_ref, ob_ref, oc_ref, wo_ref, g_ref, wup_ref, cw_ref, wdn_ref, c0_ref,
                y_ref, last_ref, carry_ref, ext_ref, hid_ref):
    tm = x_ref.shape[0]

    @pl.when(pl.program_id(0) == 0)
    def _():
        carry_ref[...] = c0_ref[...]

    mix = _dot(oa_ref[...], wo_ref[0:WIDTH_A, :])
    mix = mix + _dot(ob_ref[...], wo_ref[WIDTH_A:WIDTH_A + WIDTH_B, :])
    mix = mix + _dot(oc_ref[...], wo_ref[WIDTH_A + WIDTH_B:, :])
    x = x_ref[...] + mix
    ms = jnp.mean(x * x, axis=-1, keepdims=True)
    xn = (x * lax.rsqrt(ms + EPS) * g_ref[...]).astype(BF16)
    tf = FF_TILE
    base = carry_rows
    for j in range(D_FF // tf):
        halves = []
        for half in range(2):
            cols = slice(half * D_FF + j * tf, half * D_FF + (j + 1) * tf)
            up = _dot(xn, wup_ref[:, cols])
            ext_ref[0:base, :] = carry_ref[:, cols]
            ext_ref[base:base + tm, :] = up
            carry_ref[:, cols] = ext_ref[tm:tm + base, :]
            cw = cw_ref[:, cols]
            halves.append(cw[0:1] * ext_ref[base - 2 * shift:base - 2 * shift + tm, :]
                          + cw[1:2] * ext_ref[base - shift:base - shift + tm, :] + cw[2:3] * up)
        hid_ref[:, j * tf:(j + 1) * tf] = (_silu(halves[0]) * halves[1]).astype(BF16)
    y_ref[...] = x + _dot(hid_ref[...], wdn_ref[...])
    last_ref[...] = carry_ref[...]


def _ffn(x, oa, ob, oc, w_out, g, w_up, conv_w, w_down, carry0, shift):
    rows = x.shape[0]
    tm = ROW_TILE
    carry_rows = carry0.shape[0]
    row = lambda w: pl.BlockSpec((tm, w), lambda i: (i, 0))
    once = lambda shape: pl.BlockSpec(shape, lambda i: (0,) * len(shape), pipeline_mode=pl.Buffered(1))
    return pl.pallas_call(
        functools.partial(_ffn_kernel, shift, carry_rows),
        grid=(rows // tm,),
        in_specs=[row(D_MODEL), row(WIDTH_A), row(WIDTH_B), row(WIDTH_C), once((D_MODEL, D_MODEL)),
                  once((1, D_MODEL)), once((D_MODEL, 2 * D_FF)), once((3, 2 * D_FF)),
                  once((D_FF, D_MODEL)), once((carry_rows, 2 * D_FF))],
        out_specs=[row(D_MODEL), _full((carry_rows, 2 * D_FF))],
        out_shape=[jax.ShapeDtypeStruct((rows, D_MODEL), F32), jax.ShapeDtypeStruct((carry_rows, 2 * D_FF), F32)],
        scratch_shapes=[pltpu.VMEM((carry_rows, 2 * D_FF), F32), pltpu.VMEM((tm + carry_rows, FF_TILE), F32),
                        pltpu.VMEM((tm, D_FF), BF16)],
        compiler_params=_params("arbitrary"),
        name=f"ffn_shift{shift}",
    )(x, oa, ob, oc, w_out, g, w_up, conv_w, w_down, carry0)


SAMPLE_GROUP = 2
PAIR_ROWS = 16
NEW_ROWS = 16


def _sample_bias(wbuf, s_len):
    def table(key_pos):
        q_pos = wbuf + np.arange(s_len)
        dist = q_pos[:, None] - key_pos[None, :]
        mult = np.zeros(dist.shape)
        for window, dil in PATTERNS:
            mult += (dist >= 0) & (dist <= window) & (dist % dil == 0)
        logm = np.log(np.maximum(mult, 1))
        bias = -_alibi_slopes()[:, None, None] * np.maximum(dist, 0)[None] + logm[None]
        bias = np.where(mult[None] > 0, bias, NEG)
        out = np.zeros((N_HEADS_A, PAIR_ROWS // 2, key_pos.shape[0]))
        out[:, :s_len] = bias
        return out.reshape(N_HEADS_A // 2, PAIR_ROWS, -1)
    main = table(np.arange(wbuf))
    new = table(np.concatenate([wbuf + np.arange(s_len), np.full(NEW_ROWS - s_len, 10 ** 6)]))
    new[:, :, s_len:] = NEG
    return jnp.asarray(main, F32), jnp.asarray(new, F32)


def _sample_attn_kernel(q_ref, kn_ref, vn_ref, kt_ref, vt_ref, bm_ref, bn_ref, qc_ref, mkt_ref, mvt_ref, oa_ref, oc_ref):
    for b in range(q_ref.shape[0]):
        for pair in range(N_HEADS_A // 2):
            rows = slice(pair * LANES, (pair + 1) * LANES)
            q = q_ref[b, pair]
            s_main = _dot(q, kt_ref[b, rows, :].astype(BF16)) + bm_ref[pair]
            s_new = _dot_nt(q, kn_ref[b, :, rows].astype(BF16)) + bn_ref[pair]
            m = jnp.maximum(jnp.max(s_main, axis=-1, keepdims=True), jnp.max(s_new, axis=-1, keepdims=True))
            p_main = jnp.exp(s_main - m)
            p_new = jnp.exp(s_new - m)
            den = jnp.sum(p_main, axis=-1, keepdims=True) + jnp.sum(p_new, axis=-1, keepdims=True)
            num = _dot_nt(p_main.astype(BF16), vt_ref[b, rows, :].astype(BF16))
            num = num + _dot(p_new.astype(BF16), vn_ref[b, :, rows].astype(BF16))
            oa_ref[b, pair] = num / den
        for pair in range(N_HEADS_C // 2):
            rows = slice(pair * LANES, (pair + 1) * LANES)
            q = qc_ref[b, pair]
            s = _dot(q, mkt_ref[b, rows, :].astype(BF16))
            p = jnp.exp(s - jnp.max(s, axis=-1, keepdims=True))
            num = _dot_nt(p.astype(BF16), mvt_ref[b, rows, :].astype(BF16))
            oc_ref[b, pair] = num / jnp.sum(p, axis=-1, keepdims=True)


def _pair_rows(q, s_len):
    nb, _, nh, hd = q.shape
    q = q.reshape(nb, s_len, nh // 2, 2, hd).transpose(0, 2, 3, 1, 4)
    q = jnp.pad(q, ((0, 0), (0, 0), (0, 0), (0, PAIR_ROWS // 2 - s_len), (0, 0)))
    own = jnp.eye(2, dtype=q.dtype)[None, None, :, None, :, None]
    return (q[:, :, :, :, None, :] * own).reshape(nb, nh // 2, PAIR_ROWS, 2 * hd)


def _unpair_rows(o, s_len):
    nb, npair = o.shape[:2]
    o = o.reshape(nb, npair, 2, PAIR_ROWS // 2, 2, HEAD_DIM)
    o = jnp.stack([o[:, :, half, :s_len, half, :] for half in range(2)], axis=2)
    return o.transpose(0, 3, 1, 2, 4).reshape(nb, s_len, npair * 2 * HEAD_DIM)


def _sample_attn(qa, ka, va, cache_k, cache_v, qc, mem_k, mem_v):
    nb, s_len = qa.shape[:2]
    wbuf = cache_k.shape[1]
    n_mem = mem_k.shape[1]
    assert s_len <= PAIR_ROWS // 2
    g = SAMPLE_GROUP
    bm, bn = _sample_bias(wbuf, s_len)
    transposed = lambda a: a.transpose(0, 2, 3, 1).reshape(nb, a.shape[2] * a.shape[3], a.shape[1])
    pad_new = lambda a: jnp.pad(a, ((0, 0), (0, NEW_ROWS - s_len), (0, 0)))
    per_b = lambda shape: pl.BlockSpec((g,) + shape, lambda i: (i,) + (0,) * len(shape))
    oa, oc = pl.pallas_call(
        _sample_attn_kernel,
        grid=(nb // g,),
        in_specs=[per_b((N_HEADS_A // 2, PAIR_ROWS, LANES)), per_b((NEW_ROWS, WIDTH_A)), per_b((NEW_ROWS, WIDTH_A)),
                  per_b((WIDTH_A, wbuf)), per_b((WIDTH_A, wbuf)), _full(bm.shape), _full(bn.shape),
                  per_b((N_HEADS_C // 2, PAIR_ROWS, LANES)), per_b((WIDTH_C, n_mem)), per_b((WIDTH_C, n_mem))],
        out_specs=[per_b((N_HEADS_A // 2, PAIR_ROWS, LANES)), per_b((N_HEADS_C // 2, PAIR_ROWS, LANES))],
        out_shape=[jax.ShapeDtypeStruct((nb, N_HEADS_A // 2, PAIR_ROWS, LANES), F32),
                   jax.ShapeDtypeStruct((nb, N_HEADS_C // 2, PAIR_ROWS, LANES), F32)],
        compiler_params=_params("parallel"),
        name="sample_attn",
    )(_pair_rows(qa, s_len), pad_new(ka), pad_new(va), transposed(cache_k), transposed(cache_v), bm, bn,
      _pair_rows(qc, s_len), transposed(mem_k), transposed(mem_v))
    return _unpair_rows(oa, s_len).astype(BF16), _unpair_rows(oc, s_len).astype(BF16)


def _sample_gdn_kernel(xe_ref, cw_ref, a_ref, b_ref, alog_ref, dtb_ref, gate_ref, gn_ref, s_ref, o_ref, s_out_ref,
                       q_scr, k_scr, v_scr):
    s_len = o_ref.shape[0]
    s_out_ref[...] = s_ref[...]
    for t in range(s_len):
        for part, scr in enumerate((q_scr, k_scr, v_scr)):
            y = sum(cw_ref[j, part] * xe_ref[t + j, part] for j in range(4))
            y = _silu(y)
            if part < 2:
                y = y * lax.rsqrt(jnp.sum(y * y, axis=0, keepdims=True) + EPS)
            if part == 0:
                y = y * (HEAD_DIM ** -0.5)
            scr[...] = y
        decay = jnp.exp(-jnp.exp(alog_ref[...]) * _softplus(a_ref[t] + dtb_ref[...]))
        beta = _sigmoid(b_ref[t])

        def read_body(dk, r):
            return r + k_scr[pl.ds(dk, 1), :] * s_out_ref[dk]

        r = lax.fori_loop(0, HEAD_DIM, read_body, jnp.zeros(v_scr.shape, F32), unroll=8)
        v_new = beta * (v_scr[...] - decay * r)

        def write_body(dk, o):
            s_new = decay * s_out_ref[dk] + k_scr[pl.ds(dk, 1), :] * v_new
            s_out_ref[dk] = s_new
            return o + q_scr[pl.ds(dk, 1), :] * s_new

        o = lax.fori_loop(0, HEAD_DIM, write_body, jnp.zeros(v_scr.shape, F32), unroll=8)
        ms = jnp.mean(o * o, axis=0, keepdims=True)
        o_ref[t] = o * lax.rsqrt(ms + EPS) * gn_ref[...] * _silu(gate_ref[t])


def _sample_gdn(xe, conv_w, a_b, b_b, a_log, dt_bias, gate, out_norm, state):
    nb, ext, _ = xe.shape
    s_len = ext - 3
    nh, hd = N_HEADS_B, HEAD_DIM
    xe_t = xe.reshape(nb, ext, 3, nh, hd).transpose(1, 2, 3, 4, 0)
    cw_t = jnp.broadcast_to(conv_w.reshape(4, 3, nh, hd)[..., None], (4, 3, nh, hd, nb))
    a_t = a_b.transpose(1, 2, 0).reshape(s_len, nh, 1, nb)
    b_t = b_b.transpose(1, 2, 0).reshape(s_len, nh, 1, nb)
    alog_t = jnp.broadcast_to(a_log.reshape(nh, 1, 1), (nh, 1, nb))
    dtb_t = jnp.broadcast_to(dt_bias.reshape(nh, 1, 1), (nh, 1, nb))
    gate_t = gate.reshape(nb, s_len, nh, hd).transpose(1, 2, 3, 0)
    gn_t = jnp.broadcast_to(out_norm.reshape(1, hd, 1), (1, hd, nb))
    s_t = state.transpose(1, 2, 3, 0)
    o_t, s_new = pl.pallas_call(
        _sample_gdn_kernel,
        grid=(nh,),
        in_specs=[pl.BlockSpec((ext, 3, None, hd, nb), lambda h: (0, 0, h, 0, 0)),
                  pl.BlockSpec((4, 3, None, hd, nb), lambda h: (0, 0, h, 0, 0)),
                  pl.BlockSpec((s_len, None, 1, nb), lambda h: (0, h, 0, 0)),
                  pl.BlockSpec((s_len, None, 1, nb), lambda h: (0, h, 0, 0)),
                  pl.BlockSpec((None, 1, nb), lambda h: (h, 0, 0)),
                  pl.BlockSpec((None, 1, nb), lambda h: (h, 0, 0)),
                  pl.BlockSpec((s_len, None, hd, nb), lambda h: (0, h, 0, 0)),
                  pl.BlockSpec((None, hd, nb), lambda h: (0, 0, 0)),
                  pl.BlockSpec((None, hd, hd, nb), lambda h: (h, 0, 0, 0))],
        out_specs=[pl.BlockSpec((s_len, None, hd, nb), lambda h: (0, h, 0, 0)),
                   pl.BlockSpec((None, hd, hd, nb), lambda h: (h, 0, 0, 0))],
        out_shape=[jax.ShapeDtypeStruct((s_len, nh, hd, nb), F32), jax.ShapeDtypeStruct((nh, hd, hd, nb), F32)],
        scratch_shapes=[pltpu.VMEM((hd, nb), F32)] * 3,
        compiler_params=_params("parallel"),
        name="sample_gdn",
    )(xe_t, cw_t, a_t, b_t, alog_t, dtb_t, gate_t, gn_t, s_t)
    ob = o_t.transpose(3, 0, 1, 2).reshape(nb * s_len, nh * hd).astype(BF16)
    return ob, s_new.transpose(3, 0, 1, 2)


def _permute_w_in(w_in):
    a_start = 3 * WIDTH_A + QKV_B + WIDTH_B
    qc_start = a_start + 2 * N_HEADS_B
    pad = jnp.zeros((w_in.shape[0], N_IN_PAD - w_in.shape[1]), w_in.dtype)
    return jnp.concatenate([w_in[:, :a_start], w_in[:, qc_start:], w_in[:, a_start:qc_start], pad], axis=1).astype(BF16)


def _lane_row(v, width=LANES):
    return jnp.zeros((1, width), F32).at[0, :v.shape[0]].set(v)


def kernel(x_prompt, x_sample, cache_win_k, cache_win_v, state_gdn, state_gdn_conv, state_ffn_conv, cache_mem_k,
           cache_mem_v, mem_prompt, norm1_g, w_in, q_norm_a, k_norm_a, conv_b_w, a_log_b, dt_bias_b, out_norm_b,
           mem_norm_g, w_mem_kv, q_norm_c, k_norm_c, w_out, norm2_g, w_up, conv_ffn_w, w_down):
    depth = norm1_g.shape[0]
    assert depth == 1 and x_prompt.shape[0] == 1
    l = 0
    t_p = x_prompt.shape[1]
    nb, s_len = x_sample.shape[:2]
    xp = x_prompt.reshape(t_p, D_MODEL)
    xs = x_sample.reshape(nb * s_len, D_MODEL)

    w_in_p = _permute_w_in(w_in[l])
    w_out_b = w_out[l].astype(BF16)
    w_up_b = w_up[l].astype(BF16)
    w_down_b = w_down[l].astype(BF16)
    g1 = norm1_g[l].reshape(1, D_MODEL)
    g2 = norm2_g[l].reshape(1, D_MODEL)
    gq = jnp.tile(q_norm_a[l], N_HEADS_A).reshape(1, WIDTH_A)
    gk = jnp.tile(k_norm_a[l], N_HEADS_A).reshape(1, WIDTH_A)
    gqc = jnp.tile(q_norm_c[l], N_HEADS_C).reshape(1, WIDTH_C)
    gkc = jnp.tile(k_norm_c[l], N_HEADS_C).reshape(1, WIDTH_C)
    gn_row = jnp.tile(out_norm_b[l], N_HEADS_B).reshape(1, WIDTH_B)

    n_keep = min(MAX_WINDOW, t_p)
    q_ds, k_ds, v_ds, (ka, va, qkv, gate, qc, ab) = _inproj(xp, g1, w_in_p, gq, gk, gqc, DILATIONS, n_keep)
    parts = [_attn_band(q_d, k_d, v_d, dil) for q_d, k_d, v_d, dil in zip(q_ds, k_ds, v_ds, DILATIONS)]
    mk, mv = _memkv(mem_prompt[0], mem_norm_g[l].reshape(1, D_MODEL), w_mem_kv[l].astype(BF16), gkc)
    oa, oc = _combine_mem([p[0] for p in parts], [p[1] for p in parts], DILATIONS, qc, mk, mv)
    ob, gdn_p = _gdn_prompt(qkv, ab, gate, conv_b_w[l], _lane_row(a_log_b[l]), _lane_row(dt_bias_b[l]), gn_row)
    y_p, last_p = _ffn(xp, oa, ob, oc, w_out_b, g2, w_up_b, conv_ffn_w[l], w_down_b, jnp.zeros((8, 2 * D_FF), F32), 1)
    win_k_p = ka.reshape(1, 1, n_keep, N_HEADS_A, HEAD_DIM)
    win_v_p = va.reshape(1, 1, n_keep, N_HEADS_A, HEAD_DIM)
    gconv_p = qkv[t_p - 3:].reshape(1, 1, 3, QKV_B)
    fconv_p = last_p[6:8].reshape(1, 1, 2, 2 * D_FF)

    rows_s = nb * s_len
    (qa_s,), _, _, (ka_s, va_s, qkv_s, gate_s, qc_s, ab_s) = _inproj(xs, g1, w_in_p, gq, gk, gqc, (1,), rows_s)
    b3 = lambda a: a.reshape(nb, s_len, a.shape[-1])
    heads = lambda a: a.reshape(nb, s_len, -1, HEAD_DIM)
    oa_s, oc_s = _sample_attn(heads(qa_s), b3(ka_s), b3(va_s), cache_win_k[l], cache_win_v[l], heads(qc_s),
                              cache_mem_k[l], cache_mem_v[l])
    xe_s = jnp.concatenate([state_gdn_conv[l], b3(qkv_s)], axis=1)
    ab3 = b3(ab_s)
    ob_s, gdn_s = _sample_gdn(xe_s, conv_b_w[l], ab3[..., 0:N_HEADS_B], ab3[..., N_HEADS_B:2 * N_HEADS_B], a_log_b[l],
                              dt_bias_b[l], b3(gate_s), out_norm_b[l], state_gdn[l])
    time_major = lambda a: a.reshape(nb, s_len, a.shape[-1]).transpose(1, 0, 2).reshape(rows_s, a.shape[-1])
    carry_s = state_ffn_conv[l].transpose(1, 0, 2).reshape(2 * nb, 2 * D_FF)
    y_t, last_s = _ffn(time_major(xs), time_major(oa_s), time_major(ob_s), time_major(oc_s), w_out_b, g2, w_up_b,
                       conv_ffn_w[l], w_down_b, carry_s, nb)
    y_s = y_t.reshape(s_len, nb, D_MODEL).transpose(1, 0, 2)
    fconv_s = last_s.reshape(2, nb, 2 * D_FF).transpose(1, 0, 2)[None]

    return (y_p.reshape(1, t_p, D_MODEL), y_s,
            win_k_p, win_v_p,
            ka_s.reshape(1, nb, s_len, N_HEADS_A, HEAD_DIM), va_s.reshape(1, nb, s_len, N_HEADS_A, HEAD_DIM),
            gdn_p[None, None], gdn_s[None],
            gconv_p, xe_s[:, -3:][None],
            fconv_p, fconv_s,
            mk.reshape(1, 1, N_MEM, N_HEADS_C, HEAD_DIM), mv.reshape(1, 1, N_MEM, N_HEADS_C, HEAD_DIM))
```

```python
import functools

import numpy as np
import jax
import jax.numpy as jnp
from jax import lax
from jax.experimental import pallas as pl
from jax.experimental.pallas import tpu as pltpu

F32 = jnp.float32
BF16 = jnp.bfloat16
EPS = 1e-6
NEG = -1e30

D_MODEL = 1024
HEAD_DIM = 64
N_HEADS_A = 8
PATTERNS = ((128, 1), (512, 4), (2048, 16))
DILATIONS = tuple(d for _, d in PATTERNS)
MAX_WINDOW = 2048
N_HEADS_B = 4
N_HEADS_C = 4
N_MEM = 256
D_FF = 2816
WIDTH_A = N_HEADS_A * HEAD_DIM
WIDTH_B = N_HEADS_B * HEAD_DIM
WIDTH_C = N_HEADS_C * HEAD_DIM
QKV_B = 3 * WIDTH_B
CHUNK = 64
NBAND = 128
BAND_BLOCKS = 4
ROW_TILE = 512
FF_TILE = 256
LANES = 128
VMEM_LIMIT = 56 * 1024 * 1024

SEG_QA, SEG_KA, SEG_VA = (0, 512), (512, 1024), (1024, 1536)
SEG_QKV, SEG_GATE, SEG_QC, SEG_AB = (1536, 2304), (2304, 2560), (2560, 2816), (2816, 2944)
N_IN_PAD = 2944


def _params(*sem):
    return pltpu.CompilerParams(dimension_semantics=sem, vmem_limit_bytes=VMEM_LIMIT)


def _dot(a, b):
    return jnp.dot(a, b, preferred_element_type=F32)


def _dot_nt(a, b):
    return lax.dot_general(a, b, (((1,), (1,)), ((), ())), preferred_element_type=F32)


def _dot_tn(a, b):
    return lax.dot_general(a, b, (((0,), (0,)), ((), ())), preferred_element_type=F32)


def _dot_split(a, b, parts, left=False):
    acc = None
    rem = a
    for _ in range(parts):
        piece = rem.astype(BF16)
        term = _dot(b, piece) if left else _dot(piece, b)
        acc = term if acc is None else acc + term
        rem = rem - piece.astype(F32)
    return acc


def _sigmoid(x):
    return 1.0 / (1.0 + jnp.exp(-x))


def _silu(x):
    return x * _sigmoid(x)


def _softplus(x):
    return jnp.maximum(x, 0.0) + jnp.log1p(jnp.exp(-jnp.abs(x)))


def _group_ones(width, group=HEAD_DIM):
    i = np.arange(width)
    return jnp.asarray((i[:, None] // group) == (i[None, :] // group), BF16)


def _full(shape):
    nd = len(shape)
    return pl.BlockSpec(shape, lambda *_: (0,) * nd)


def _call_with_rider(host_kernel, rider, *, grid, in_specs, out_specs, out_shape, scratch_shapes, args, sem, name):
    if rider is None:
        rider = dict(units=None, in_specs=[], out_specs=[], out_shape=[], args=[])
    n_hi, n_ri, n_ho, n_ro = len(in_specs), len(rider["in_specs"]), len(out_specs), len(rider["out_specs"])

    def body(*refs):
        host_in, refs = refs[:n_hi], refs[n_hi:]
        rider_in, refs = refs[:n_ri], refs[n_ri:]
        host_out, refs = refs[:n_ho], refs[n_ho:]
        rider_out, scratch = refs[:n_ro], refs[n_ro:]
        units = rider["units"](*rider_in, *rider_out) if rider["units"] is not None else []
        host_kernel(units, *host_in, *host_out, *scratch)

    outs = pl.pallas_call(
        body, grid=grid, in_specs=list(in_specs) + rider["in_specs"], out_specs=list(out_specs) + rider["out_specs"],
        out_shape=list(out_shape) + rider["out_shape"], scratch_shapes=scratch_shapes,
        compiler_params=_params(sem), name=name,
    )(*args, *rider["args"])
    return outs[:n_ho], outs[n_ho:]


def _inproj_kernel(dils, x_ref, g1_ref, w_ref, gq_ref, gk_ref, gc_ref, ga_ref, gcc_ref, *refs):
    nd = len(dils)
    q_refs, k_refs, v_refs = refs[0:nd], refs[nd:2 * nd], refs[2 * nd:3 * nd]
    ka_ref, va_ref, qkv_ref, gate_ref, qc_ref, ab_ref, scr_ref = refs[3 * nd:]
    tm = x_ref.shape[0]
    x = x_ref[...]
    ms = jnp.mean(x * x, axis=-1, keepdims=True)
    xn = (x * lax.rsqrt(ms + EPS) * g1_ref[...]).astype(BF16)

    def seg(s):
        return _dot(xn, w_ref[:, s[0]:s[1]])

    def head_norm(z, ones_ref, gain):
        ss = _dot_split(z * z, ones_ref[...], 2) * (1.0 / HEAD_DIM)
        return z * lax.rsqrt(ss + EPS) * gain

    def emit(z, out_refs):
        n_tiles = WIDTH_A // LANES
        for c in range(n_tiles):
            scr_ref[c] = z[:, c * LANES:(c + 1) * LANES]
        for d, ref in zip(dils, out_refs):
            if d == 1:
                ref[...] = z.astype(BF16)
            else:
                for r in range(d):
                    for c in range(n_tiles):
                        col = r * WIDTH_A + c * LANES
                        ref[:, col:col + LANES] = scr_ref[c, pl.ds(r, tm // d, stride=d), :].astype(BF16)

    scale = HEAD_DIM ** -0.5
    emit(head_norm(seg(SEG_QA), ga_ref, gq_ref[...]) * scale, q_refs)
    ka = head_norm(seg(SEG_KA), ga_ref, gk_ref[...])
    ka_ref[...] = ka
    emit(ka, k_refs)
    va = seg(SEG_VA)
    va_ref[...] = va
    emit(va, v_refs)
    qkv_ref[...] = seg(SEG_QKV)
    gate_ref[...] = seg(SEG_GATE)
    qc_ref[...] = (head_norm(seg(SEG_QC), gcc_ref, gc_ref[...]) * scale).astype(BF16)
    ab_ref[...] = seg(SEG_AB)


def _inproj(x, g1, w_in_p, gq, gk, gc, dils, keep_rows):
    rows = x.shape[0]
    tm = ROW_TILE
    nt = rows // tm
    skip = nt - keep_rows // tm
    row = lambda w: pl.BlockSpec((tm, w), lambda i: (i, 0))
    tail = pl.BlockSpec((tm, WIDTH_A), lambda i: (jnp.maximum(i - skip, 0), 0))
    dil_specs = [pl.BlockSpec((tm // d, d * WIDTH_A), lambda i: (i, 0)) for d in dils]
    dil_shapes = [jax.ShapeDtypeStruct((rows // d, d * WIDTH_A), BF16) for d in dils]
    outs = [(QKV_B, F32), (WIDTH_B, F32), (WIDTH_C, BF16), (LANES, F32)]
    res = pl.pallas_call(
        functools.partial(_inproj_kernel, dils),
        grid=(nt,),
        in_specs=[row(D_MODEL), _full((1, D_MODEL)), _full((D_MODEL, N_IN_PAD)), _full((1, WIDTH_A)),
                  _full((1, WIDTH_A)), _full((1, WIDTH_C)), _full((WIDTH_A, WIDTH_A)), _full((WIDTH_C, WIDTH_C))],
        out_specs=dil_specs * 3 + [tail, tail] + [row(w) for w, _ in outs],
        out_shape=dil_shapes * 3 + [jax.ShapeDtypeStruct((keep_rows, WIDTH_A), F32)] * 2
        + [jax.ShapeDtypeStruct((rows, w), dt) for w, dt in outs],
        scratch_shapes=[pltpu.VMEM((WIDTH_A // LANES, tm, LANES), F32)],
        compiler_params=_params("arbitrary"),
        name="inproj",
    )(x, g1, w_in_p, gq, gk, gc, _group_ones(WIDTH_A), _group_ones(WIDTH_C))
    nd = len(dils)
    return res[0:nd], res[nd:2 * nd], res[2 * nd:3 * nd], res[3 * nd:]


def _alibi_slopes():
    return np.exp2(-8.0 * np.arange(1, N_HEADS_A + 1, dtype=np.float64) / N_HEADS_A)


def _band_bias(dil):
    qi = np.arange(NBAND)[:, None]
    kj = np.arange(2 * NBAND)[None, :]
    delta = qi + NBAND - kj
    in_band = (delta >= 0) & (delta <= NBAND)
    bias = -_alibi_slopes()[:, None, None] * (delta * dil)[None].astype(np.float64)
    general = np.where(in_band[None], bias, NEG)
    first = np.where((in_band & (kj >= NBAND))[None], bias, NEG)
    return jnp.asarray(np.stack([general, first]), F32)


def _attn_band_kernel(q_ref, kp_ref, kc_ref, vp_ref, vc_ref, bias_ref, o_ref, lse_ref):
    lane = lax.broadcasted_iota(jnp.int32, (NBAND, LANES), 1)
    low = lane < HEAD_DIM
    ones = jnp.ones((2 * NBAND, LANES), BF16)
    for blk in range(BAND_BLOCKS):
        first = (pl.program_id(1) == 0).astype(jnp.int32) if blk == 0 else 0
        rows = slice(blk * NBAND, (blk + 1) * NBAND)
        lse = jnp.zeros((NBAND, LANES), F32)
        for pair in range(N_HEADS_A // 2):
            cols = slice(pair * LANES, (pair + 1) * LANES)
            q = q_ref[rows, cols]
            if blk == 0:
                k = jnp.concatenate([kp_ref[:, cols], kc_ref[rows, cols]], axis=0)
                v = jnp.concatenate([vp_ref[:, cols], vc_ref[rows, cols]], axis=0)
            else:
                k = kc_ref[(blk - 1) * NBAND:(blk + 1) * NBAND, cols]
                v = vc_ref[(blk - 1) * NBAND:(blk + 1) * NBAND, cols]
            v_ext = jnp.concatenate([v, ones], axis=1)
            outs = []
            for half in range(2):
                h = 2 * pair + half
                qm = jnp.where(low if half == 0 else ~low, q, jnp.zeros((), BF16))
                s = _dot_nt(qm, k) + bias_ref[first, h]
                m = jnp.max(s, axis=-1, keepdims=True)
                p = jnp.exp(s - m).astype(BF16)
                r = _dot(p, v_ext)
                den = r[:, LANES:]
                outs.append(r[:, :LANES] / den)
                lse = jnp.where(lane == h, m + jnp.log(den), lse)
            o_ref[rows, cols] = jnp.where(low, outs[0], outs[1]).astype(BF16)
        lse_ref[rows, :] = lse


def _attn_band(q_d, k_d, v_d, dil):
    rows = q_d.shape[0]
    step = BAND_BLOCKS * NBAND
    cur = lambda w: pl.BlockSpec((step, w), lambda r, n: (n, r))
    prev = lambda w: pl.BlockSpec((NBAND, w), lambda r, n: (jnp.maximum(n * BAND_BLOCKS - 1, 0), r))
    return pl.pallas_call(
        _attn_band_kernel,
        grid=(dil, rows // step),
        in_specs=[cur(WIDTH_A), prev(WIDTH_A), cur(WIDTH_A), prev(WIDTH_A), cur(WIDTH_A),
                  _full((2, N_HEADS_A, NBAND, 2 * NBAND))],
        out_specs=[cur(WIDTH_A), cur(LANES)],
        out_shape=[jax.ShapeDtypeStruct((rows, dil * WIDTH_A), BF16), jax.ShapeDtypeStruct((rows, dil * LANES), F32)],
        compiler_params=_params("parallel", "arbitrary"),
        name=f"attn_band_d{dil}",
    )(q_d, k_d, k_d, v_d, v_d, _band_bias(dil))


def _memkv_kernel(mem_ref, g_ref, w_ref, gk_ref, ones_ref, mk_ref, mv_ref):
    x = mem_ref[...]
    ms = jnp.mean(x * x, axis=-1, keepdims=True)
    xn = (x * lax.rsqrt(ms + EPS) * g_ref[...]).astype(BF16)
    zk = _dot(xn, w_ref[:, 0:WIDTH_C])
    ss = _dot_split(zk * zk, ones_ref[...], 2) * (1.0 / HEAD_DIM)
    mk_ref[...] = zk * lax.rsqrt(ss + EPS) * gk_ref[...]
    mv_ref[...] = _dot(xn, w_ref[:, WIDTH_C:2 * WIDTH_C])


def _memkv(mem, g, w, gk):
    n = mem.shape[0]
    return pl.pallas_call(
        _memkv_kernel,
        out_shape=[jax.ShapeDtypeStruct((n, WIDTH_C), F32)] * 2,
        compiler_params=pltpu.CompilerParams(vmem_limit_bytes=VMEM_LIMIT),
        name="memkv",
    )(mem, g, w, gk, _group_ones(WIDTH_C))


def _head_spread():
    e = np.zeros((LANES, WIDTH_A), np.float32)
    for h in range(N_HEADS_A):
        e[h, h * HEAD_DIM:(h + 1) * HEAD_DIM] = 1.0
    return jnp.asarray(e, BF16)


def _combine_mem_kernel(dils, *refs):
    nd = len(dils)
    o_refs, l_refs = refs[0:nd], refs[nd:2 * nd]
    qc_ref, mk_ref, mv_ref, e_ref, oa_ref, oc_ref, o_scr, l_scr = refs[2 * nd:]
    tm = oa_ref.shape[0]
    os_, ls = [], []
    for p, d in enumerate(dils):
        if d == 1:
            os_.append(o_refs[p][...].astype(F32))
            ls.append(l_refs[p][...])
            continue
        n_tiles = WIDTH_A // LANES
        for r in range(d):
            for c in range(n_tiles):
                col = r * WIDTH_A + c * LANES
                o_scr[p * n_tiles + c, pl.ds(r, tm // d, stride=d), :] = o_refs[p][:, col:col + LANES].astype(F32)
            l_scr[p, pl.ds(r, tm // d, stride=d), :] = l_refs[p][:, r * LANES:(r + 1) * LANES]
        os_.append(jnp.concatenate([o_scr[p * n_tiles + c] for c in range(n_tiles)], axis=1))
        ls.append(l_scr[p])
    m = functools.reduce(jnp.maximum, ls)
    es = [jnp.exp(l - m) for l in ls]
    tot = functools.reduce(lambda a, b: a + b, es)
    acc = None
    for o, e in zip(os_, es):
        term = o * _dot_split(e / tot, e_ref[...], 2)
        acc = term if acc is None else acc + term
    oa_ref[...] = acc.astype(BF16)

    q = qc_ref[...]
    mk = mk_ref[...].astype(BF16)
    mv = mv_ref[...].astype(BF16)
    for h in range(N_HEADS_C):
        sl = slice(h * HEAD_DIM, (h + 1) * HEAD_DIM)
        s = _dot_nt(q[:, sl], mk[:, sl])
        mx = jnp.max(s, axis=-1, keepdims=True)
        p = jnp.exp(s - mx)
        den_c = jnp.sum(p, axis=-1, keepdims=True)
        oc_ref[:, sl] = (_dot(p.astype(BF16), mv[:, sl]) / den_c).astype(BF16)


def _combine_mem(os_, lses, dils, qc, mk, mv):
    t = qc.shape[0]
    tm = ROW_TILE
    nd = len(dils)
    row = lambda w: pl.BlockSpec((tm, w), lambda i: (i, 0))
    o_specs = [pl.BlockSpec((tm // d, d * WIDTH_A), lambda i: (i, 0)) for d in dils]
    l_specs = [pl.BlockSpec((tm // d, d * LANES), lambda i: (i, 0)) for d in dils]
    return pl.pallas_call(
        functools.partial(_combine_mem_kernel, dils),
        grid=(t // tm,),
        in_specs=o_specs + l_specs + [row(WIDTH_C), _full((N_MEM, WIDTH_C)), _full((N_MEM, WIDTH_C)),
                                     _full((LANES, WIDTH_A))],
        out_specs=[row(WIDTH_A), row(WIDTH_C)],
        out_shape=[jax.ShapeDtypeStruct((t, WIDTH_A), BF16), jax.ShapeDtypeStruct((t, WIDTH_C), BF16)],
        scratch_shapes=[pltpu.VMEM((nd * WIDTH_A // LANES, tm, LANES), F32), pltpu.VMEM((nd, tm, LANES), F32)],
        compiler_params=_params("parallel"),
        name="combine_mem",
    )(*os_, *lses, qc, mk, mv, _head_spread())


def _head_lane_consts():
    e_g = np.zeros((LANES, WIDTH_B), np.float32)
    e_b = np.zeros((LANES, WIDTH_B), np.float32)
    for h in range(N_HEADS_B):
        e_g[h, h * HEAD_DIM:(h + 1) * HEAD_DIM] = 1.0
        e_b[N_HEADS_B + h, h * HEAD_DIM:(h + 1) * HEAD_DIM] = 1.0
    i = np.arange(CHUNK)
    ltri = (i[None, :] <= i[:, None]).astype(np.float32)
    return jnp.asarray(e_g, BF16), jnp.asarray(e_b, BF16), jnp.asarray(ltri, BF16), jnp.ones((CHUNK, CHUNK), BF16)


def _block_diag(x, mask):
    return jnp.where(mask, jnp.concatenate([x] * N_HEADS_B, axis=0), jnp.zeros((), x.dtype))


def _gdn_prep_kernel(units, x_ref, halo_ref, ab_ref, cw_ref, alog_ref, dtb_ref, ones_ref, eg_ref, eb_ref, ltri_ref, one64_ref,
                     m_ref, attn_ref, qd_ref, kd_ref, vb_ref, kbe_ref, egl_ref, xe_ref, m_scr):
    tm = x_ref.shape[0]
    nc = tm // CHUNK
    halo = jnp.where(pl.program_id(0) > 0, halo_ref[...], 0.0)
    xe_ref[0:8, :] = halo
    xe_ref[8:8 + tm, :] = x_ref[...]
    cw = cw_ref[...]
    y = cw[3:4] * xe_ref[8:8 + tm, :] + cw[2:3] * xe_ref[7:7 + tm, :] + cw[1:2] * xe_ref[6:6 + tm, :] \
        + cw[0:1] * xe_ref[5:5 + tm, :]
    y = _silu(y)

    def l2n(z):
        return z * lax.rsqrt(_dot_split(z * z, ones_ref[...], 2) + EPS)

    q = l2n(y[:, 0:WIDTH_B]) * (HEAD_DIM ** -0.5)
    k = l2n(y[:, WIDTH_B:2 * WIDTH_B])
    v = y[:, 2 * WIDTH_B:3 * WIDTH_B]
    ab = ab_ref[...]
    g = -jnp.exp(alog_ref[...]) * _softplus(ab + dtb_ref[...])
    gx = _dot_split(g, eg_ref[...], 2)
    bx = _dot_split(_sigmoid(ab), eb_ref[...], 2)

    row = lax.broadcasted_iota(jnp.int32, (CHUNK, WIDTH_B), 0)
    col = lax.broadcasted_iota(jnp.int32, (CHUNK, WIDTH_B), 1) % CHUNK
    diag = row == col
    r4 = lax.broadcasted_iota(jnp.int32, (WIDTH_B, WIDTH_B), 0) // HEAD_DIM
    c4 = lax.broadcasted_iota(jnp.int32, (WIDTH_B, WIDTH_B), 1) // HEAD_DIM
    bd_mask = r4 == c4
    for c in range(tm // CHUNK):
        sl = slice(c * CHUNK, (c + 1) * CHUNK)
        gc = _dot_split(gx[sl], ltri_ref[...], 2, left=True)
        gl = gc[CHUNK - 1:CHUNK, :]
        gcj = _dot_split(jnp.where(diag, gc, 0.0), one64_ref[...], 2, left=True)
        dec = jnp.exp(jnp.where(row >= col, gc - gcj, NEG))
        kc, qc, vc, bc = k[sl], q[sl], v[sl], bx[sl]
        kb = kc * bc
        lhs = jnp.concatenate([kb, qc], axis=0).astype(BF16)
        kkqk = _dot_nt(lhs, _block_diag(kc.astype(BF16), bd_mask))
        m_c = jnp.where(row > col, kkqk[0:CHUNK] * dec, 0.0)
        for lt in range(WIDTH_B // LANES):
            m_scr[pl.ds(lt * nc + c, CHUNK, stride=nc * WIDTH_B // LANES), :] = m_c[:, lt * LANES:(lt + 1) * LANES]
        attn_ref[sl, :] = (kkqk[CHUNK:2 * CHUNK] * dec).astype(BF16)
        egc = jnp.exp(gc)
        qd_ref[sl, :] = (qc * egc).astype(BF16)
        kd_ref[sl, :] = (kc * jnp.exp(gl - gc)).astype(BF16)
        vb_ref[sl, :] = (vc * bc).astype(BF16)
        kbe_ref[sl, :] = (kb * egc).astype(BF16)
        egl_ref[c:c + 1, :] = jnp.exp(gl)
        _run_share(units, c, nc)
    m_ref[...] = m_scr[...].reshape(m_ref.shape)


def _gdn_prep(qkv, ab, conv_w, alog_row, dtb_row, rider=None):
    t = qkv.shape[0]
    tm = ROW_TILE
    nc = tm // CHUNK
    assert nc == 8
    n_lt = WIDTH_B // LANES
    row = lambda w: pl.BlockSpec((tm, w), lambda i: (i, 0))
    m_spec = pl.BlockSpec((CHUNK, n_lt, nc, LANES), lambda i: (0, 0, i, 0))
    e_g, e_b, ltri, one64 = _head_lane_consts()
    outs = [BF16, BF16, BF16, BF16, BF16]
    return _call_with_rider(
        _gdn_prep_kernel, rider,
        grid=(t // tm,),
        in_specs=[row(QKV_B), pl.BlockSpec((8, QKV_B), lambda i: (jnp.maximum(i * (tm // 8) - 1, 0), 0)), row(LANES),
                  _full((4, QKV_B)), _full((1, LANES)), _full((1, LANES)), _full((WIDTH_B, WIDTH_B)),
                  _full((LANES, WIDTH_B)), _full((LANES, WIDTH_B)), _full((CHUNK, CHUNK)), _full((CHUNK, CHUNK))],
        out_specs=[m_spec] + [row(WIDTH_B)] * 5 + [pl.BlockSpec((nc, WIDTH_B), lambda i: (i, 0))],
        out_shape=[jax.ShapeDtypeStruct((CHUNK, n_lt, t // CHUNK, LANES), F32)]
        + [jax.ShapeDtypeStruct((t, WIDTH_B), dt) for dt in outs]
        + [jax.ShapeDtypeStruct((t // CHUNK, WIDTH_B), F32)],
        scratch_shapes=[pltpu.VMEM((tm + 8, QKV_B), F32), pltpu.VMEM((CHUNK * n_lt * nc, LANES), F32)],
        args=(qkv, qkv, ab, conv_w, alog_row, dtb_row, _group_ones(WIDTH_B), e_g, e_b, ltri, one64),
        sem="parallel", name="gdn_prep")


def _tinv_kernel(m_ref, out_ref, n_ref, mrow_ref, nrow_ref):
    nslots = n_ref.shape[2]
    heads_per_tile = LANES // CHUNK
    blk = pl.program_id(0)
    for ii in range(8):
        i = blk * 8 + ii
        for g in range(nslots // N_HEADS_B):
            for lt in range(WIDTH_B // LANES):
                xt = m_ref[ii, lt, g * LANES:(g + 1) * LANES, :].T
                for h2 in range(heads_per_tile):
                    slot = g * N_HEADS_B + lt * heads_per_tile + h2
                    mrow_ref[ii, pl.ds(slot, CHUNK, stride=nslots), :] = xt[h2 * CHUNK:(h2 + 1) * CHUNK, :]
        for kb in range(8):
            k0 = 8 * kb
            init = tuple(mrow_ref[ii, (k0 + kk) * nslots:(k0 + kk + 1) * nslots, :] for kk in range(8))

            def body(j, acc, ii=ii, k0=k0):
                mij = mrow_ref[ii, pl.ds(pl.multiple_of(j * nslots, nslots), nslots), :]
                return tuple(acc[kk] + mij * n_ref[j, k0 + kk] for kk in range(8))

            acc = lax.fori_loop(k0, jnp.maximum(i, k0), body, init)
            for kk in range(8):
                val = jnp.where(k0 + kk < i, -acc[kk], 0.0)
                n_ref[i, k0 + kk] = val
                nrow_ref[(k0 + kk) * nslots:(k0 + kk + 1) * nslots, :] = val
        for g in range(nslots // N_HEADS_B):
            for lt in range(WIDTH_B // LANES):
                slot0 = g * N_HEADS_B + lt * heads_per_tile
                parts = [nrow_ref[pl.ds(slot0 + h2, CHUNK, stride=nslots), :] for h2 in range(heads_per_tile)]
                out_ref[ii, lt, g * LANES:(g + 1) * LANES, :] = jnp.concatenate(parts, axis=0).T


def _tinv(m4):
    _, n_lt, nchunk, _ = m4.shape
    assert nchunk % LANES == 0
    nslots = (nchunk // LANES) * N_HEADS_B
    blk = pl.BlockSpec((8, n_lt, nchunk, LANES), lambda i: (i, 0, 0, 0))
    return pl.pallas_call(
        _tinv_kernel,
        grid=(CHUNK // 8,),
        in_specs=[blk],
        out_specs=blk,
        out_shape=jax.ShapeDtypeStruct(m4.shape, F32),
        scratch_shapes=[pltpu.VMEM((CHUNK, CHUNK, nslots, LANES), F32), pltpu.VMEM((8, CHUNK * nslots, LANES), F32),
                        pltpu.VMEM((CHUNK * nslots, LANES), F32)],
        compiler_params=_params("arbitrary"),
        name="gdn_tinv",
    )(m4)


def _gdn_scan_kernel(units, n_ref, vb_ref, kbe_ref, attn_ref, qd_ref, kd_ref, egl_ref, gate_ref, gn_ref, ones_ref,
                     ob_ref, s_out_ref, s_ref, n_scr):
    tm = vb_ref.shape[0]
    nc = tm // CHUNK
    n_lt = WIDTH_B // LANES

    @pl.when(pl.program_id(0) == 0)
    def _():
        s_ref[...] = jnp.zeros_like(s_ref)

    n_scr[...] = n_ref[...].reshape(n_scr.shape)

    row = lax.broadcasted_iota(jnp.int32, (CHUNK, WIDTH_B), 0)
    col = lax.broadcasted_iota(jnp.int32, (CHUNK, WIDTH_B), 1) % CHUNK
    eye = (row == col).astype(F32)
    r4 = lax.broadcasted_iota(jnp.int32, (WIDTH_B, WIDTH_B), 0) // HEAD_DIM
    c4 = lax.broadcasted_iota(jnp.int32, (WIDTH_B, WIDTH_B), 1) // HEAD_DIM
    bd_mask = r4 == c4
    s = s_ref[...]
    for c in range(tm // CHUNK):
        sl = slice(c * CHUNK, (c + 1) * CHUNK)
        n_c = jnp.concatenate([n_scr[pl.ds(lt * nc + c, CHUNK, stride=nc * n_lt), :] for lt in range(n_lt)], axis=1)
        tinv = (n_c + eye).astype(BF16)
        u = _dot(tinv, _block_diag(vb_ref[sl, :], bd_mask))
        w = _dot(tinv, _block_diag(kbe_ref[sl, :], bd_mask))
        lhs = jnp.concatenate([w.astype(BF16), qd_ref[sl, :]], axis=0)
        ws = _dot(lhs, s.astype(BF16))
        v_new = (u - ws[0:CHUNK]).astype(BF16)
        o = ws[CHUNK:2 * CHUNK] + _dot(attn_ref[sl, :], _block_diag(v_new, bd_mask))
        s = s * egl_ref[c:c + 1, :] + jnp.where(bd_mask, _dot_tn(kd_ref[sl, :], v_new), 0.0)
        ms = _dot_split(o * o, ones_ref[...], 2) * (1.0 / HEAD_DIM)
        ob_ref[sl, :] = (o * lax.rsqrt(ms + EPS) * gn_ref[...] * _silu(gate_ref[sl, :])).astype(BF16)
        _run_share(units, c, nc)
    s_ref[...] = s
    s_out_ref[...] = s


def _gdn_scan(n4, vb, kbe, attn, qd, kd, egl, gate, gn_row, rider=None):
    t = vb.shape[0]
    tm = ROW_TILE
    nc = tm // CHUNK
    n_lt = WIDTH_B // LANES
    row = lambda w: pl.BlockSpec((tm, w), lambda i: (i, 0))
    return _call_with_rider(
        _gdn_scan_kernel, rider,
        grid=(t // tm,),
        in_specs=[pl.BlockSpec((CHUNK, n_lt, nc, LANES), lambda i: (0, 0, i, 0))] + [row(WIDTH_B)] * 5
        + [pl.BlockSpec((nc, WIDTH_B), lambda i: (i, 0)), row(WIDTH_B), _full((1, WIDTH_B)), _full((WIDTH_B, WIDTH_B))],
        out_specs=[row(WIDTH_B), _full((WIDTH_B, WIDTH_B))],
        out_shape=[jax.ShapeDtypeStruct((t, WIDTH_B), BF16), jax.ShapeDtypeStruct((WIDTH_B, WIDTH_B), F32)],
        scratch_shapes=[pltpu.VMEM((WIDTH_B, WIDTH_B), F32), pltpu.VMEM((CHUNK * n_lt * nc, LANES), F32)],
        args=(n4, vb, kbe, attn, qd, kd, egl, gate, gn_row, _group_ones(WIDTH_B)),
        sem="arbitrary", name="gdn_scan")


def _gdn_prompt(qkv, ab, gate, conv_w, alog_row, dtb_row, gn_row, riders=(None, None)):
    (m4, attn, qd, kd, vb, kbe, egl), ride0 = _gdn_prep(qkv, ab, conv_w, alog_row, dtb_row, riders[0])
    (ob, s_bd), ride1 = _gdn_scan(_tinv(m4), vb, kbe, attn, qd, kd, egl, gate, gn_row, riders[1])
    s_fin = jnp.stack([s_bd[h * HEAD_DIM:(h + 1) * HEAD_DIM, h * HEAD_DIM:(h + 1) * HEAD_DIM] for h in range(N_HEADS_B)])
    return ob, s_fin, (ride0, ride1)


def _ffn_kernel(shift, carry_rows, x_ref, oa_ref, ob_ref, oc_ref, wo_ref, g_ref, wup_ref, cw_ref, wdn_ref, c0_ref,
                y_ref, last_ref, carry_ref, ext_ref, hid_ref):
    tm = x_ref.shape[0]

    @pl.when(pl.program_id(0) == 0)
    def _():
        carry_ref[...] = c0_ref[...]

    mix = _dot(oa_ref[...], wo_ref[0:WIDTH_A, :])
    mix = mix + _dot(ob_ref[...], wo_ref[WIDTH_A:WIDTH_A + WIDTH_B, :])
    mix = mix + _dot(oc_ref[...], wo_ref[WIDTH_A + WIDTH_B:, :])
    x = x_ref[...] + mix
    ms = jnp.mean(x * x, axis=-1, keepdims=True)
    xn = (x * lax.rsqrt(ms + EPS) * g_ref[...]).astype(BF16)
    tf = FF_TILE
    base = carry_rows
    for j in range(D_FF // tf):
        halves = []
        for half in range(2):
            cols = slice(half * D_FF + j * tf, half * D_FF + (j + 1) * tf)
            up = _dot(xn, wup_ref[:, cols])
            ext_ref[0:base, :] = carry_ref[:, cols]
            ext_ref[base:base + tm, :] = up
            carry_ref[:, cols] = ext_ref[tm:tm + base, :]
            cw = cw_ref[:, cols]
            halves.append(cw[0:1] * ext_ref[base - 2 * shift:base - 2 * shift + tm, :]
                          + cw[1:2] * ext_ref[base - shift:base - shift + tm, :] + cw[2:3] * up)
        hid_ref[:, j * tf:(j + 1) * tf] = (_silu(halves[0]) * halves[1]).astype(BF16)
    y_ref[...] = x + _dot(hid_ref[...], wdn_ref[...])
    last_ref[...] = carry_ref[...]


def _ffn(x, oa, ob, oc, w_out, g, w_up, conv_w, w_down, carry0, shift):
    rows = x.shape[0]
    tm = ROW_TILE
    carry_rows = carry0.shape[0]
    row = lambda w: pl.BlockSpec((tm, w), lambda i: (i, 0))
    once = lambda shape: pl.BlockSpec(shape, lambda i: (0,) * len(shape), pipeline_mode=pl.Buffered(1))
    return pl.pallas_call(
        functools.partial(_ffn_kernel, shift, carry_rows),
        grid=(rows // tm,),
        in_specs=[row(D_MODEL), row(WIDTH_A), row(WIDTH_B), row(WIDTH_C), once((D_MODEL, D_MODEL)),
                  once((1, D_MODEL)), once((D_MODEL, 2 * D_FF)), once((3, 2 * D_FF)),
                  once((D_FF, D_MODEL)), once((carry_rows, 2 * D_FF))],
        out_specs=[row(D_MODEL), _full((carry_rows, 2 * D_FF))],
        out_shape=[jax.ShapeDtypeStruct((rows, D_MODEL), F32), jax.ShapeDtypeStruct((carry_rows, 2 * D_FF), F32)],
        scratch_shapes=[pltpu.VMEM((carry_rows, 2 * D_FF), F32), pltpu.VMEM((tm + carry_rows, FF_TILE), F32),
                        pltpu.VMEM((tm, D_FF), BF16)],
        compiler_params=_params("arbitrary"),
        name=f"ffn_shift{shift}",
    )(x, oa, ob, oc, w_out, g, w_up, conv_w, w_down, carry0)


PAIR_ROWS = 16
NEW_ROWS = 16


def _sample_bias(wbuf, s_len):
    def table(key_pos):
        q_pos = wbuf + np.arange(s_len)
        dist = q_pos[:, None] - key_pos[None, :]
        mult = np.zeros(dist.shape)
        for window, dil in PATTERNS:
            mult += (dist >= 0) & (dist <= window) & (dist % dil == 0)
        logm = np.log(np.maximum(mult, 1))
        bias = -_alibi_slopes()[:, None, None] * np.maximum(dist, 0)[None] + logm[None]
        bias = np.where(mult[None] > 0, bias, NEG)
        out = np.zeros((N_HEADS_A, PAIR_ROWS // 2, key_pos.shape[0]))
        out[:, :s_len] = bias
        return out.reshape(N_HEADS_A // 2, PAIR_ROWS, -1)
    main = table(np.arange(wbuf))
    new = table(np.concatenate([wbuf + np.arange(s_len), np.full(NEW_ROWS - s_len, 10 ** 6)]))
    new[:, :, s_len:] = NEG
    return jnp.asarray(main, F32), jnp.asarray(new, F32)


def _sample_attn_units(q_ref, kn_ref, vn_ref, kt_ref, vt_ref, bm_ref, bn_ref, qc_ref, mkt_ref, mvt_ref, oa_ref, oc_ref):
    def window(b):
        scores = []
        for pair in range(N_HEADS_A // 2):
            rows = slice(pair * LANES, (pair + 1) * LANES)
            q = q_ref[b, pair]
            scores.append((_dot(q, kt_ref[b, rows, :].astype(BF16)) + bm_ref[pair],
                           _dot_nt(q, kn_ref[b, :, rows].astype(BF16)) + bn_ref[pair]))
        for pair, (s_main, s_new) in enumerate(scores):
            rows = slice(pair * LANES, (pair + 1) * LANES)
            m = jnp.maximum(jnp.max(s_main, axis=-1, keepdims=True), jnp.max(s_new, axis=-1, keepdims=True))
            p_main = jnp.exp(s_main - m)
            p_new = jnp.exp(s_new - m)
            den = jnp.sum(p_main, axis=-1, keepdims=True) + jnp.sum(p_new, axis=-1, keepdims=True)
            num = _dot_nt(p_main.astype(BF16), vt_ref[b, rows, :].astype(BF16))
            num = num + _dot(p_new.astype(BF16), vn_ref[b, :, rows].astype(BF16))
            oa_ref[b, pair] = num / den

    def memory(b):
        for pair in range(N_HEADS_C // 2):
            rows = slice(pair * LANES, (pair + 1) * LANES)
            q = qc_ref[b, pair]
            s = _dot(q, mkt_ref[b, rows, :].astype(BF16))
            p = jnp.exp(s - jnp.max(s, axis=-1, keepdims=True))
            num = _dot_nt(p.astype(BF16), mvt_ref[b, rows, :].astype(BF16))
            oc_ref[b, pair] = num / jnp.sum(p, axis=-1, keepdims=True)

    units = []
    for b in range(q_ref.shape[0]):
        units += [functools.partial(window, b), functools.partial(memory, b)]
    return units


def _run_share(units, stage, n_stages):
    for u in units[stage * len(units) // n_stages:(stage + 1) * len(units) // n_stages]:
        u()


def _pair_rows(q, s_len):
    nb, _, nh, hd = q.shape
    q = q.reshape(nb, s_len, nh // 2, 2, hd).transpose(0, 2, 3, 1, 4)
    q = jnp.pad(q, ((0, 0), (0, 0), (0, 0), (0, PAIR_ROWS // 2 - s_len), (0, 0)))
    own = jnp.eye(2, dtype=q.dtype)[None, None, :, None, :, None]
    return (q[:, :, :, :, None, :] * own).reshape(nb, nh // 2, PAIR_ROWS, 2 * hd)


def _unpair_rows(o, s_len):
    nb, npair = o.shape[:2]
    o = o.reshape(nb, npair, 2, PAIR_ROWS // 2, 2, HEAD_DIM)
    o = jnp.stack([o[:, :, half, :s_len, half, :] for half in range(2)], axis=2)
    return o.transpose(0, 3, 1, 2, 4).reshape(nb, s_len, npair * 2 * HEAD_DIM)


def _sample_attn_riders(qa, ka, va, cache_k, cache_v, qc, mem_k, mem_v, n_parts, steps):
    nb, s_len = qa.shape[:2]
    wbuf = cache_k.shape[1]
    n_mem = mem_k.shape[1]
    assert s_len <= PAIR_ROWS // 2 and nb % (n_parts * steps) == 0
    g = nb // (n_parts * steps)
    bm, bn = _sample_bias(wbuf, s_len)
    transposed = lambda a: a.transpose(0, 2, 3, 1).reshape(nb, a.shape[2] * a.shape[3], a.shape[1])
    pad_new = lambda a: jnp.pad(a, ((0, 0), (0, NEW_ROWS - s_len), (0, 0)))
    args = [_pair_rows(qa, s_len), pad_new(ka), pad_new(va), transposed(cache_k), transposed(cache_v), bm, bn,
            _pair_rows(qc, s_len), transposed(mem_k), transposed(mem_v)]
    riders = []
    for part in range(n_parts):
        per_b = lambda shape, part=part: pl.BlockSpec((g,) + shape, lambda i: (i + part * steps,) + (0,) * len(shape))
        out_b = lambda shape: pl.BlockSpec((g,) + shape, lambda i: (i,) + (0,) * len(shape))
        riders.append(dict(
            units=_sample_attn_units,
            in_specs=[per_b((N_HEADS_A // 2, PAIR_ROWS, LANES)), per_b((NEW_ROWS, WIDTH_A)), per_b((NEW_ROWS, WIDTH_A)),
                      per_b((WIDTH_A, wbuf)), per_b((WIDTH_A, wbuf)), _full(bm.shape), _full(bn.shape),
                      per_b((N_HEADS_C // 2, PAIR_ROWS, LANES)), per_b((WIDTH_C, n_mem)), per_b((WIDTH_C, n_mem))],
            out_specs=[out_b((N_HEADS_A // 2, PAIR_ROWS, LANES)), out_b((N_HEADS_C // 2, PAIR_ROWS, LANES))],
            out_shape=[jax.ShapeDtypeStruct((g * steps, N_HEADS_A // 2, PAIR_ROWS, LANES), F32),
                       jax.ShapeDtypeStruct((g * steps, N_HEADS_C // 2, PAIR_ROWS, LANES), F32)],
            args=args))

    def finish(rider_outs):
        oa = jnp.concatenate([o[0] for o in rider_outs], axis=0)
        oc = jnp.concatenate([o[1] for o in rider_outs], axis=0)
        return _unpair_rows(oa, s_len).astype(BF16), _unpair_rows(oc, s_len).astype(BF16)

    return riders, finish


def _sample_gdn_kernel(xe_ref, cw_ref, a_ref, b_ref, alog_ref, dtb_ref, gate_ref, gn_ref, s_ref, o_ref, s_out_ref,
                       q_scr, k_scr, v_scr):
    s_len = o_ref.shape[0]
    s_out_ref[...] = s_ref[...]
    for t in range(s_len):
        for part, scr in enumerate((q_scr, k_scr, v_scr)):
            y = sum(cw_ref[j, part] * xe_ref[t + j, part] for j in range(4))
            y = _silu(y)
            if part < 2:
                y = y * lax.rsqrt(jnp.sum(y * y, axis=0, keepdims=True) + EPS)
            if part == 0:
                y = y * (HEAD_DIM ** -0.5)
            scr[...] = y
        decay = jnp.exp(-jnp.exp(alog_ref[...]) * _softplus(a_ref[t] + dtb_ref[...]))
        beta = _sigmoid(b_ref[t])

        def read_body(dk, r):
            return r + k_scr[pl.ds(dk, 1), :] * s_out_ref[dk]

        r = lax.fori_loop(0, HEAD_DIM, read_body, jnp.zeros(v_scr.shape, F32), unroll=8)
        v_new = beta * (v_scr[...] - decay * r)

        def write_body(dk, o):
            s_new = decay * s_out_ref[dk] + k_scr[pl.ds(dk, 1), :] * v_new
            s_out_ref[dk] = s_new
            return o + q_scr[pl.ds(dk, 1), :] * s_new

        o = lax.fori_loop(0, HEAD_DIM, write_body, jnp.zeros(v_scr.shape, F32), unroll=8)
        ms = jnp.mean(o * o, axis=0, keepdims=True)
        o_ref[t] = o * lax.rsqrt(ms + EPS) * gn_ref[...] * _silu(gate_ref[t])


def _sample_gdn(xe, conv_w, a_b, b_b, a_log, dt_bias, gate, out_norm, state):
    nb, ext, _ = xe.shape
    s_len = ext - 3
    nh, hd = N_HEADS_B, HEAD_DIM
    xe_t = xe.reshape(nb, ext, 3, nh, hd).transpose(1, 2, 3, 4, 0)
    cw_t = jnp.broadcast_to(conv_w.reshape(4, 3, nh, hd)[..., None], (4, 3, nh, hd, nb))
    a_t = a_b.transpose(1, 2, 0).reshape(s_len, nh, 1, nb)
    b_t = b_b.transpose(1, 2, 0).reshape(s_len, nh, 1, nb)
    alog_t = jnp.broadcast_to(a_log.reshape(nh, 1, 1), (nh, 1, nb))
    dtb_t = jnp.broadcast_to(dt_bias.reshape(nh, 1, 1), (nh, 1, nb))
    gate_t = gate.reshape(nb, s_len, nh, hd).transpose(1, 2, 3, 0)
    gn_t = jnp.broadcast_to(out_norm.reshape(1, hd, 1), (1, hd, nb))
    s_t = state.transpose(1, 2, 3, 0)
    o_t, s_new = pl.pallas_call(
        _sample_gdn_kernel,
        grid=(nh,),
        in_specs=[pl.BlockSpec((ext, 3, None, hd, nb), lambda h: (0, 0, h, 0, 0)),
                  pl.BlockSpec((4, 3, None, hd, nb), lambda h: (0, 0, h, 0, 0)),
                  pl.BlockSpec((s_len, None, 1, nb), lambda h: (0, h, 0, 0)),
                  pl.BlockSpec((s_len, None, 1, nb), lambda h: (0, h, 0, 0)),
                  pl.BlockSpec((None, 1, nb), lambda h: (h, 0, 0)),
                  pl.BlockSpec((None, 1, nb), lambda h: (h, 0, 0)),
                  pl.BlockSpec((s_len, None, hd, nb), lambda h: (0, h, 0, 0)),
                  pl.BlockSpec((None, hd, nb), lambda h: (0, 0, 0)),
                  pl.BlockSpec((None, hd, hd, nb), lambda h: (h, 0, 0, 0))],
        out_specs=[pl.BlockSpec((s_len, None, hd, nb), lambda h: (0, h, 0, 0)),
                   pl.BlockSpec((None, hd, hd, nb), lambda h: (h, 0, 0, 0))],
        out_shape=[jax.ShapeDtypeStruct((s_len, nh, hd, nb), F32), jax.ShapeDtypeStruct((nh, hd, hd, nb), F32)],
        scratch_shapes=[pltpu.VMEM((hd, nb), F32)] * 3,
        compiler_params=_params("parallel"),
        name="sample_gdn",
    )(xe_t, cw_t, a_t, b_t, alog_t, dtb_t, gate_t, gn_t, s_t)
    ob = o_t.transpose(3, 0, 1, 2).reshape(nb * s_len, nh * hd).astype(BF16)
    return ob, s_new.transpose(3, 0, 1, 2)


def _permute_w_in(w_in):
    a_start = 3 * WIDTH_A + QKV_B + WIDTH_B
    qc_start = a_start + 2 * N_HEADS_B
    pad = jnp.zeros((w_in.shape[0], N_IN_PAD - w_in.shape[1]), w_in.dtype)
    return jnp.concatenate([w_in[:, :a_start], w_in[:, qc_start:], w_in[:, a_start:qc_start], pad], axis=1).astype(BF16)


def _lane_row(v, width=LANES):
    return jnp.zeros((1, width), F32).at[0, :v.shape[0]].set(v)


def kernel(x_prompt, x_sample, cache_win_k, cache_win_v, state_gdn, state_gdn_conv, state_ffn_conv, cache_mem_k,
           cache_mem_v, mem_prompt, norm1_g, w_in, q_norm_a, k_norm_a, conv_b_w, a_log_b, dt_bias_b, out_norm_b,
           mem_norm_g, w_mem_kv, q_norm_c, k_norm_c, w_out, norm2_g, w_up, conv_ffn_w, w_down):
    depth = norm1_g.shape[0]
    assert depth == 1 and x_prompt.shape[0] == 1
    l = 0
    t_p = x_prompt.shape[1]
    nb, s_len = x_sample.shape[:2]
    xp = x_prompt.reshape(t_p, D_MODEL)
    xs = x_sample.reshape(nb * s_len, D_MODEL)

    w_in_p = _permute_w_in(w_in[l])
    w_out_b = w_out[l].astype(BF16)
    w_up_b = w_up[l].astype(BF16)
    w_down_b = w_down[l].astype(BF16)
    g1 = norm1_g[l].reshape(1, D_MODEL)
    g2 = norm2_g[l].reshape(1, D_MODEL)
    gq = jnp.tile(q_norm_a[l], N_HEADS_A).reshape(1, WIDTH_A)
    gk = jnp.tile(k_norm_a[l], N_HEADS_A).reshape(1, WIDTH_A)
    gqc = jnp.tile(q_norm_c[l], N_HEADS_C).reshape(1, WIDTH_C)
    gkc = jnp.tile(k_norm_c[l], N_HEADS_C).reshape(1, WIDTH_C)
    gn_row = jnp.tile(out_norm_b[l], N_HEADS_B).reshape(1, WIDTH_B)

    rows_s = nb * s_len
    (qa_s,), _, _, (ka_s, va_s, qkv_s, gate_s, qc_s, ab_s) = _inproj(xs, g1, w_in_p, gq, gk, gqc, (1,), rows_s)
    b3 = lambda a: a.reshape(nb, s_len, a.shape[-1])
    heads = lambda a: a.reshape(nb, s_len, -1, HEAD_DIM)
    riders, finish_sample_attn = _sample_attn_riders(
        heads(qa_s), b3(ka_s), b3(va_s), cache_win_k[l], cache_win_v[l], heads(qc_s), cache_mem_k[l], cache_mem_v[l],
        n_parts=2, steps=t_p // ROW_TILE)

    n_keep = min(MAX_WINDOW, t_p)
    q_ds, k_ds, v_ds, (ka, va, qkv, gate, qc, ab) = _inproj(xp, g1, w_in_p, gq, gk, gqc, DILATIONS, n_keep)
    parts = [_attn_band(q_d, k_d, v_d, dil) for q_d, k_d, v_d, dil in zip(q_ds, k_ds, v_ds, DILATIONS)]
    mk, mv = _memkv(mem_prompt[0], mem_norm_g[l].reshape(1, D_MODEL), w_mem_kv[l].astype(BF16), gkc)
    oa, oc = _combine_mem([p[0] for p in parts], [p[1] for p in parts], DILATIONS, qc, mk, mv)
    ob, gdn_p, rider_outs = _gdn_prompt(qkv, ab, gate, conv_b_w[l], _lane_row(a_log_b[l]), _lane_row(dt_bias_b[l]),
                                        gn_row, riders)
    y_p, last_p = _ffn(xp, oa, ob, oc, w_out_b, g2, w_up_b, conv_ffn_w[l], w_down_b, jnp.zeros((8, 2 * D_FF), F32), 1)
    win_k_p = ka.reshape(1, 1, n_keep, N_HEADS_A, HEAD_DIM)
    win_v_p = va.reshape(1, 1, n_keep, N_HEADS_A, HEAD_DIM)
    gconv_p = qkv[t_p - 3:].reshape(1, 1, 3, QKV_B)
    fconv_p = last_p[6:8].reshape(1, 1, 2, 2 * D_FF)

    oa_s, oc_s = finish_sample_attn(rider_outs)
    xe_s = jnp.concatenate([state_gdn_conv[l], b3(qkv_s)], axis=1)
    ab3 = b3(ab_s)
    ob_s, gdn_s = _sample_gdn(xe_s, conv_b_w[l], ab3[..., 0:N_HEADS_B], ab3[..., N_HEADS_B:2 * N_HEADS_B], a_log_b[l],
                              dt_bias_b[l], b3(gate_s), out_norm_b[l], state_gdn[l])
    time_major = lambda a: a.reshape(nb, s_len, a.shape[-1]).transpose(1, 0, 2).reshape(rows_s, a.shape[-1])
    carry_s = state_ffn_conv[l].transpose(1, 0, 2).reshape(2 * nb, 2 * D_FF)
    y_t, last_s = _ffn(time_major(xs), time_major(oa_s), time_major(ob_s), time_major(oc_s), w_out_b, g2, w_up_b,
                       conv_ffn_w[l], w_down_b, carry_s, nb)
    y_s = y_t.reshape(s_len, nb, D_MODEL).transpose(1, 0, 2)
    fconv_s = last_s.reshape(2, nb, 2 * D_FF).transpose(1, 0, 2)[None]

    return (y_p.reshape(1, t_p, D_MODEL), y_s,
            win_k_p, win_v_p,
            ka_s.reshape(1, nb, s_len, N_HEADS_A, HEAD_DIM), va_s.reshape(1, nb, s_len, N_HEADS_A, HEAD_DIM),
            gdn_p[None, None], gdn_s[None],
            gconv_p, xe_s[:, -3:][None],
            fconv_p, fconv_s,
            mk.reshape(1, 1, N_MEM, N_HEADS_C, HEAD_DIM), mv.reshape(1, 1, N_MEM, N_HEADS_C, HEAD_DIM))
```

```python
import functools

import numpy as np
import jax
import jax.numpy as jnp
from jax import lax
from jax.experimental import pallas as pl
from jax.experimental.pallas import tpu as pltpu

F32 = jnp.float32
BF16 = jnp.bfloat16
EPS = 1e-6
NEG = -1e30

D_MODEL = 1024
HEAD_DIM = 64
N_HEADS_A = 8
PATTERNS = ((128, 1), (512, 4), (2048, 16))
DILATIONS = tuple(d for _, d in PATTERNS)
MAX_WINDOW = 2048
N_HEADS_B = 4
N_HEADS_C = 4
N_MEM = 256
D_FF = 2816
WIDTH_A = N_HEADS_A * HEAD_DIM
WIDTH_B = N_HEADS_B * HEAD_DIM
WIDTH_C = N_HEADS_C * HEAD_DIM
QKV_B = 3 * WIDTH_B
CHUNK = 64
NBAND = 128
BAND_BLOCKS = 4
ROW_TILE = 512
FF_TILE = 256
LANES = 128
VMEM_LIMIT = 56 * 1024 * 1024

SEG_QA, SEG_KA, SEG_VA = (0, 512), (512, 1024), (1024, 1536)
SEG_QKV, SEG_GATE, SEG_TAIL = (1536, 2304), (2304, 2560), (2560, 2824)
N_IN = 2824


def _params(*sem):
    return pltpu.CompilerParams(dimension_semantics=sem, vmem_limit_bytes=VMEM_LIMIT)


def _dot(a, b):
    return jnp.dot(a, b, preferred_element_type=F32)


def _dot_nt(a, b):
    return lax.dot_general(a, b, (((1,), (1,)), ((), ())), preferred_element_type=F32)


def _dot_tn(a, b):
    return lax.dot_general(a, b, (((0,), (0,)), ((), ())), preferred_element_type=F32)


def _dot_split(a, b, parts, left=False):
    acc = None
    rem = a
    for _ in range(parts):
        piece = rem.astype(BF16)
        term = _dot(b, piece) if left else _dot(piece, b)
        acc = term if acc is None else acc + term
        rem = rem - piece.astype(F32)
    return acc


def _sigmoid(x):
    return 1.0 / (1.0 + jnp.exp(-x))


def _silu(x):
    return x * _sigmoid(x)


def _softplus(x):
    return jnp.maximum(x, 0.0) + jnp.log1p(jnp.exp(-jnp.abs(x)))


def _group_ones(width, group=HEAD_DIM):
    i = np.arange(width)
    return jnp.asarray((i[:, None] // group) == (i[None, :] // group), BF16)


def _full(shape):
    nd = len(shape)
    return pl.BlockSpec(shape, lambda *_: (0,) * nd)


def _call_with_rider(host_kernel, rider, *, grid, in_specs, out_specs, out_shape, scratch_shapes, args, sem, name):
    if rider is None:
        rider = dict(units=None, in_specs=[], out_specs=[], out_shape=[], args=[])
    n_hi, n_ri, n_ho, n_ro = len(in_specs), len(rider["in_specs"]), len(out_specs), len(rider["out_specs"])

    def body(*refs):
        host_in, refs = refs[:n_hi], refs[n_hi:]
        rider_in, refs = refs[:n_ri], refs[n_ri:]
        host_out, refs = refs[:n_ho], refs[n_ho:]
        rider_out, scratch = refs[:n_ro], refs[n_ro:]
        units = rider["units"](*rider_in, *rider_out) if rider["units"] is not None else []
        host_kernel(units, *host_in, *host_out, *scratch)

    outs = pl.pallas_call(
        body, grid=grid, in_specs=list(in_specs) + rider["in_specs"], out_specs=list(out_specs) + rider["out_specs"],
        out_shape=list(out_shape) + rider["out_shape"], scratch_shapes=scratch_shapes,
        compiler_params=_params(sem), name=name,
    )(*args, *rider["args"])
    return outs[:n_ho], outs[n_ho:]


def _inproj_kernel(dils, x_ref, g1_ref, w_ref, gq_ref, gk_ref, gc_ref, ga_ref, gcc_ref, *refs):
    nd = len(dils)
    q_refs, k_refs, v_refs = refs[0:nd], refs[nd:2 * nd], refs[2 * nd:3 * nd]
    ka_ref, va_ref, qkv_ref, gate_ref, qc_ref, ab_ref, scr_ref = refs[3 * nd:]
    tm = x_ref.shape[0]
    x = x_ref[...]
    ms = jnp.mean(x * x, axis=-1, keepdims=True)
    xn = (x * lax.rsqrt(ms + EPS) * g1_ref[...]).astype(BF16)

    def seg(s):
        return _dot_nt(xn, w_ref[s[0]:s[1], :])

    def head_norm(z, ones_ref, gain):
        ss = _dot_split(z * z, ones_ref[...], 2) * (1.0 / HEAD_DIM)
        return z * lax.rsqrt(ss + EPS) * gain

    def emit(z, out_refs):
        n_tiles = WIDTH_A // LANES
        for c in range(n_tiles):
            scr_ref[c] = z[:, c * LANES:(c + 1) * LANES]
        for d, ref in zip(dils, out_refs):
            if d == 1:
                ref[...] = z.astype(BF16)
            else:
                for r in range(d):
                    for c in range(n_tiles):
                        col = r * WIDTH_A + c * LANES
                        ref[:, col:col + LANES] = scr_ref[c, pl.ds(r, tm // d, stride=d), :].astype(BF16)

    scale = HEAD_DIM ** -0.5
    emit(head_norm(seg(SEG_QA), ga_ref, gq_ref[...]) * scale, q_refs)
    ka = head_norm(seg(SEG_KA), ga_ref, gk_ref[...])
    ka_ref[...] = ka
    emit(ka, k_refs)
    va = seg(SEG_VA)
    va_ref[...] = va
    emit(va, v_refs)
    qkv_ref[...] = seg(SEG_QKV)
    gate_ref[...] = seg(SEG_GATE)
    tail = seg(SEG_TAIL)
    ab_ref[...] = tail[:, 0:LANES]
    qc = tail[:, 2 * N_HEADS_B:2 * N_HEADS_B + WIDTH_C]
    qc_ref[...] = (head_norm(qc, gcc_ref, gc_ref[...]) * scale).astype(BF16)


def _inproj(x, g1, w_in_p, gq, gk, gc, dils, keep_rows):
    rows = x.shape[0]
    tm = ROW_TILE
    nt = rows // tm
    skip = nt - keep_rows // tm
    row = lambda w: pl.BlockSpec((tm, w), lambda i: (i, 0))
    tail = pl.BlockSpec((tm, WIDTH_A), lambda i: (jnp.maximum(i - skip, 0), 0))
    dil_specs = [pl.BlockSpec((tm // d, d * WIDTH_A), lambda i: (i, 0)) for d in dils]
    dil_shapes = [jax.ShapeDtypeStruct((rows // d, d * WIDTH_A), BF16) for d in dils]
    outs = [(QKV_B, F32), (WIDTH_B, F32), (WIDTH_C, BF16), (LANES, F32)]
    res = pl.pallas_call(
        functools.partial(_inproj_kernel, dils),
        grid=(nt,),
        in_specs=[row(D_MODEL), _full((1, D_MODEL)), _full((N_IN, D_MODEL)), _full((1, WIDTH_A)),
                  _full((1, WIDTH_A)), _full((1, WIDTH_C)), _full((WIDTH_A, WIDTH_A)), _full((WIDTH_C, WIDTH_C))],
        out_specs=dil_specs * 3 + [tail, tail] + [row(w) for w, _ in outs],
        out_shape=dil_shapes * 3 + [jax.ShapeDtypeStruct((keep_rows, WIDTH_A), F32)] * 2
        + [jax.ShapeDtypeStruct((rows, w), dt) for w, dt in outs],
        scratch_shapes=[pltpu.VMEM((WIDTH_A // LANES, tm, LANES), F32)],
        compiler_params=_params("arbitrary"),
        name="inproj",
    )(x, g1, w_in_p, gq, gk, gc, _group_ones(WIDTH_A), _group_ones(WIDTH_C))
    nd = len(dils)
    return res[0:nd], res[nd:2 * nd], res[2 * nd:3 * nd], res[3 * nd:]


def _alibi_slopes():
    return np.exp2(-8.0 * np.arange(1, N_HEADS_A + 1, dtype=np.float64) / N_HEADS_A)


def _band_bias(dil):
    qi = np.arange(NBAND)[:, None]
    kj = np.arange(2 * NBAND)[None, :]
    delta = qi + NBAND - kj
    in_band = (delta >= 0) & (delta <= NBAND)
    bias = -_alibi_slopes()[:, None, None] * (delta * dil)[None].astype(np.float64)
    general = np.where(in_band[None], bias, NEG)
    first = np.where((in_band & (kj >= NBAND))[None], bias, NEG)
    return jnp.asarray(np.stack([general, first]), F32)


def _attn_band_kernel(q_ref, kp_ref, kc_ref, vp_ref, vc_ref, bias_ref, o_ref, lse_ref):
    lane = lax.broadcasted_iota(jnp.int32, (NBAND, LANES), 1)
    low = lane < HEAD_DIM
    ones = jnp.ones((2 * NBAND, LANES), BF16)
    for blk in range(BAND_BLOCKS):
        first = (pl.program_id(1) == 0).astype(jnp.int32) if blk == 0 else 0
        rows = slice(blk * NBAND, (blk + 1) * NBAND)
        lse = jnp.zeros((NBAND, LANES), F32)
        for pair in range(N_HEADS_A // 2):
            cols = slice(pair * LANES, (pair + 1) * LANES)
            q = q_ref[rows, cols]
            if blk == 0:
                k = jnp.concatenate([kp_ref[:, cols], kc_ref[rows, cols]], axis=0)
                v = jnp.concatenate([vp_ref[:, cols], vc_ref[rows, cols]], axis=0)
            else:
                k = kc_ref[(blk - 1) * NBAND:(blk + 1) * NBAND, cols]
                v = vc_ref[(blk - 1) * NBAND:(blk + 1) * NBAND, cols]
            v_ext = jnp.concatenate([v, ones], axis=1)
            outs = []
            for half in range(2):
                h = 2 * pair + half
                qm = jnp.where(low if half == 0 else ~low, q, jnp.zeros((), BF16))
                s = _dot_nt(qm, k) + bias_ref[first, h]
                m = jnp.max(s, axis=-1, keepdims=True)
                p = jnp.exp(s - m).astype(BF16)
                r = _dot(p, v_ext)
                den = r[:, LANES:]
                outs.append(r[:, :LANES] / den)
                lse = jnp.where(lane == h, m + jnp.log(den), lse)
            o_ref[rows, cols] = jnp.where(low, outs[0], outs[1]).astype(BF16)
        lse_ref[rows, :] = lse


def _attn_band(q_d, k_d, v_d, dil):
    rows = q_d.shape[0]
    step = BAND_BLOCKS * NBAND
    cur = lambda w: pl.BlockSpec((step, w), lambda r, n: (n, r))
    prev = lambda w: pl.BlockSpec((NBAND, w), lambda r, n: (jnp.maximum(n * BAND_BLOCKS - 1, 0), r))
    return pl.pallas_call(
        _attn_band_kernel,
        grid=(dil, rows // step),
        in_specs=[cur(WIDTH_A), prev(WIDTH_A), cur(WIDTH_A), prev(WIDTH_A), cur(WIDTH_A),
                  _full((2, N_HEADS_A, NBAND, 2 * NBAND))],
        out_specs=[cur(WIDTH_A), cur(LANES)],
        out_shape=[jax.ShapeDtypeStruct((rows, dil * WIDTH_A), BF16), jax.ShapeDtypeStruct((rows, dil * LANES), F32)],
        compiler_params=_params("parallel", "arbitrary"),
        name=f"attn_band_d{dil}",
    )(q_d, k_d, k_d, v_d, v_d, _band_bias(dil))


def _memkv_kernel(mem_ref, g_ref, w_ref, gk_ref, ones_ref, mk_ref, mv_ref):
    x = mem_ref[...]
    ms = jnp.mean(x * x, axis=-1, keepdims=True)
    xn = (x * lax.rsqrt(ms + EPS) * g_ref[...]).astype(BF16)
    zk = _dot(xn, w_ref[:, 0:WIDTH_C])
    ss = _dot_split(zk * zk, ones_ref[...], 2) * (1.0 / HEAD_DIM)
    mk_ref[...] = zk * lax.rsqrt(ss + EPS) * gk_ref[...]
    mv_ref[...] = _dot(xn, w_ref[:, WIDTH_C:2 * WIDTH_C])


def _memkv(mem, g, w, gk):
    n = mem.shape[0]
    return pl.pallas_call(
        _memkv_kernel,
        out_shape=[jax.ShapeDtypeStruct((n, WIDTH_C), F32)] * 2,
        compiler_params=pltpu.CompilerParams(vmem_limit_bytes=VMEM_LIMIT),
        name="memkv",
    )(mem, g, w, gk, _group_ones(WIDTH_C))


def _head_spread():
    e = np.zeros((LANES, WIDTH_A), np.float32)
    for h in range(N_HEADS_A):
        e[h, h * HEAD_DIM:(h + 1) * HEAD_DIM] = 1.0
    return jnp.asarray(e, BF16)


def _combine_mem_kernel(dils, *refs):
    nd = len(dils)
    o_refs, l_refs = refs[0:nd], refs[nd:2 * nd]
    qc_ref, mk_ref, mv_ref, e_ref, oa_ref, oc_ref, o_scr, l_scr = refs[2 * nd:]
    tm = oa_ref.shape[0]
    os_, ls = [], []
    for p, d in enumerate(dils):
        if d == 1:
            os_.append(o_refs[p][...].astype(F32))
            ls.append(l_refs[p][...])
            continue
        n_tiles = WIDTH_A // LANES
        for r in range(d):
            for c in range(n_tiles):
                col = r * WIDTH_A + c * LANES
                o_scr[p * n_tiles + c, pl.ds(r, tm // d, stride=d), :] = o_refs[p][:, col:col + LANES].astype(F32)
            l_scr[p, pl.ds(r, tm // d, stride=d), :] = l_refs[p][:, r * LANES:(r + 1) * LANES]
        os_.append(jnp.concatenate([o_scr[p * n_tiles + c] for c in range(n_tiles)], axis=1))
        ls.append(l_scr[p])
    m = functools.reduce(jnp.maximum, ls)
    es = [jnp.exp(l - m) for l in ls]
    tot = functools.reduce(lambda a, b: a + b, es)
    acc = None
    for o, e in zip(os_, es):
        term = o * _dot_split(e / tot, e_ref[...], 2)
        acc = term if acc is None else acc + term
    oa_ref[...] = acc.astype(BF16)

    q = qc_ref[...]
    mk = mk_ref[...].astype(BF16)
    mv = mv_ref[...].astype(BF16)
    for h in range(N_HEADS_C):
        sl = slice(h * HEAD_DIM, (h + 1) * HEAD_DIM)
        s = _dot_nt(q[:, sl], mk[:, sl])
        mx = jnp.max(s, axis=-1, keepdims=True)
        p = jnp.exp(s - mx)
        den_c = jnp.sum(p, axis=-1, keepdims=True)
        oc_ref[:, sl] = (_dot(p.astype(BF16), mv[:, sl]) / den_c).astype(BF16)


def _combine_mem(os_, lses, dils, qc, mk, mv):
    t = qc.shape[0]
    tm = ROW_TILE
    nd = len(dils)
    row = lambda w: pl.BlockSpec((tm, w), lambda i: (i, 0))
    o_specs = [pl.BlockSpec((tm // d, d * WIDTH_A), lambda i: (i, 0)) for d in dils]
    l_specs = [pl.BlockSpec((tm // d, d * LANES), lambda i: (i, 0)) for d in dils]
    return pl.pallas_call(
        functools.partial(_combine_mem_kernel, dils),
        grid=(t // tm,),
        in_specs=o_specs + l_specs + [row(WIDTH_C), _full((N_MEM, WIDTH_C)), _full((N_MEM, WIDTH_C)),
                                     _full((LANES, WIDTH_A))],
        out_specs=[row(WIDTH_A), row(WIDTH_C)],
        out_shape=[jax.ShapeDtypeStruct((t, WIDTH_A), BF16), jax.ShapeDtypeStruct((t, WIDTH_C), BF16)],
        scratch_shapes=[pltpu.VMEM((nd * WIDTH_A // LANES, tm, LANES), F32), pltpu.VMEM((nd, tm, LANES), F32)],
        compiler_params=_params("parallel"),
        name="combine_mem",
    )(*os_, *lses, qc, mk, mv, _head_spread())


def _head_lane_consts():
    e_g = np.zeros((LANES, WIDTH_B), np.float32)
    e_b = np.zeros((LANES, WIDTH_B), np.float32)
    for h in range(N_HEADS_B):
        e_g[h, h * HEAD_DIM:(h + 1) * HEAD_DIM] = 1.0
        e_b[N_HEADS_B + h, h * HEAD_DIM:(h + 1) * HEAD_DIM] = 1.0
    i = np.arange(CHUNK)
    ltri = (i[None, :] <= i[:, None]).astype(np.float32)
    return jnp.asarray(e_g, BF16), jnp.asarray(e_b, BF16), jnp.asarray(ltri, BF16), jnp.ones((CHUNK, CHUNK), BF16)


def _block_diag(x, mask):
    return jnp.where(mask, jnp.concatenate([x] * N_HEADS_B, axis=0), jnp.zeros((), x.dtype))


def _gdn_prep_kernel(units, x_ref, halo_ref, ab_ref, cw_ref, alog_ref, dtb_ref, ones_ref, eg_ref, eb_ref, ltri_ref, one64_ref,
                     m_ref, attn_ref, qd_ref, kd_ref, vb_ref, kbe_ref, egl_ref, xe_ref, m_scr):
    tm = x_ref.shape[0]
    nc = tm // CHUNK
    halo = jnp.where(pl.program_id(0) > 0, halo_ref[...], 0.0)
    xe_ref[0:8, :] = halo
    xe_ref[8:8 + tm, :] = x_ref[...]
    cw = cw_ref[...]
    y = cw[3:4] * xe_ref[8:8 + tm, :] + cw[2:3] * xe_ref[7:7 + tm, :] + cw[1:2] * xe_ref[6:6 + tm, :] \
        + cw[0:1] * xe_ref[5:5 + tm, :]
    y = _silu(y)

    def l2n(z):
        return z * lax.rsqrt(_dot_split(z * z, ones_ref[...], 2) + EPS)

    q = l2n(y[:, 0:WIDTH_B]) * (HEAD_DIM ** -0.5)
    k = l2n(y[:, WIDTH_B:2 * WIDTH_B])
    v = y[:, 2 * WIDTH_B:3 * WIDTH_B]
    ab = ab_ref[...]
    g = -jnp.exp(alog_ref[...]) * _softplus(ab + dtb_ref[...])
    gx = _dot_split(g, eg_ref[...], 2)
    bx = _dot_split(_sigmoid(ab), eb_ref[...], 2)

    row = lax.broadcasted_iota(jnp.int32, (CHUNK, WIDTH_B), 0)
    col = lax.broadcasted_iota(jnp.int32, (CHUNK, WIDTH_B), 1) % CHUNK
    diag = row == col
    r4 = lax.broadcasted_iota(jnp.int32, (WIDTH_B, WIDTH_B), 0) // HEAD_DIM
    c4 = lax.broadcasted_iota(jnp.int32, (WIDTH_B, WIDTH_B), 1) // HEAD_DIM
    bd_mask = r4 == c4
    for c in range(tm // CHUNK):
        sl = slice(c * CHUNK, (c + 1) * CHUNK)
        gc = _dot_split(gx[sl], ltri_ref[...], 2, left=True)
        gl = gc[CHUNK - 1:CHUNK, :]
        gcj = _dot_split(jnp.where(diag, gc, 0.0), one64_ref[...], 2, left=True)
        dec = jnp.exp(jnp.where(row >= col, gc - gcj, NEG))
        kc, qc, vc, bc = k[sl], q[sl], v[sl], bx[sl]
        kb = kc * bc
        lhs = jnp.concatenate([kb, qc], axis=0).astype(BF16)
        kkqk = _dot_nt(lhs, _block_diag(kc.astype(BF16), bd_mask))
        m_c = jnp.where(row > col, kkqk[0:CHUNK] * dec, 0.0)
        for lt in range(WIDTH_B // LANES):
            m_scr[pl.ds(lt * nc + c, CHUNK, stride=nc * WIDTH_B // LANES), :] = m_c[:, lt * LANES:(lt + 1) * LANES]
        attn_ref[sl, :] = (kkqk[CHUNK:2 * CHUNK] * dec).astype(BF16)
        egc = jnp.exp(gc)
        qd_ref[sl, :] = (qc * egc).astype(BF16)
        kd_ref[sl, :] = (kc * jnp.exp(gl - gc)).astype(BF16)
        vb_ref[sl, :] = (vc * bc).astype(BF16)
        kbe_ref[sl, :] = (kb * egc).astype(BF16)
        egl_ref[c:c + 1, :] = jnp.exp(gl)
        _run_share(units, c, nc)
    m_ref[...] = m_scr[...].reshape(m_ref.shape)


def _gdn_prep(qkv, ab, conv_w, alog_row, dtb_row, rider=None):
    t = qkv.shape[0]
    tm = ROW_TILE
    nc = tm // CHUNK
    assert nc == 8
    n_lt = WIDTH_B // LANES
    row = lambda w: pl.BlockSpec((tm, w), lambda i: (i, 0))
    m_spec = pl.BlockSpec((CHUNK, n_lt, nc, LANES), lambda i: (0, 0, i, 0))
    e_g, e_b, ltri, one64 = _head_lane_consts()
    outs = [BF16, BF16, BF16, BF16, BF16]
    return _call_with_rider(
        _gdn_prep_kernel, rider,
        grid=(t // tm,),
        in_specs=[row(QKV_B), pl.BlockSpec((8, QKV_B), lambda i: (jnp.maximum(i * (tm // 8) - 1, 0), 0)), row(LANES),
                  _full((4, QKV_B)), _full((1, LANES)), _full((1, LANES)), _full((WIDTH_B, WIDTH_B)),
                  _full((LANES, WIDTH_B)), _full((LANES, WIDTH_B)), _full((CHUNK, CHUNK)), _full((CHUNK, CHUNK))],
        out_specs=[m_spec] + [row(WIDTH_B)] * 5 + [pl.BlockSpec((nc, WIDTH_B), lambda i: (i, 0))],
        out_shape=[jax.ShapeDtypeStruct((CHUNK, n_lt, t // CHUNK, LANES), F32)]
        + [jax.ShapeDtypeStruct((t, WIDTH_B), dt) for dt in outs]
        + [jax.ShapeDtypeStruct((t // CHUNK, WIDTH_B), F32)],
        scratch_shapes=[pltpu.VMEM((tm + 8, QKV_B), F32), pltpu.VMEM((CHUNK * n_lt * nc, LANES), F32)],
        args=(qkv, qkv, ab, conv_w, alog_row, dtb_row, _group_ones(WIDTH_B), e_g, e_b, ltri, one64),
        sem="parallel", name="gdn_prep")


def _tinv_kernel(m_ref, out_ref, n_ref, mrow_ref, nrow_ref):
    nslots = n_ref.shape[2]
    heads_per_tile = LANES // CHUNK
    blk = pl.program_id(0)

    @pl.when(blk == 0)
    def _():
        n_ref[...] = jnp.zeros_like(n_ref)

    for ii in range(8):
        i = blk * 8 + ii
        for g in range(nslots // N_HEADS_B):
            for lt in range(WIDTH_B // LANES):
                xt = m_ref[ii, lt, g * LANES:(g + 1) * LANES, :].T
                for h2 in range(heads_per_tile):
                    slot = g * N_HEADS_B + lt * heads_per_tile + h2
                    mrow_ref[ii, pl.ds(slot, CHUNK, stride=nslots), :] = xt[h2 * CHUNK:(h2 + 1) * CHUNK, :]
        for kb in range(8):
            k0 = 8 * kb
            init = tuple(mrow_ref[ii, (k0 + kk) * nslots:(k0 + kk + 1) * nslots, :] for kk in range(8))

            def m_at(j, ii=ii):
                return mrow_ref[ii, pl.ds(pl.multiple_of(j * nslots, nslots), nslots), :]

            def body(jj, acc, k0=k0):
                j = k0 + 2 * jj
                m0, m1 = m_at(j), m_at(j + 1)
                return tuple(acc[kk] + m0 * n_ref[j, k0 + kk] + m1 * n_ref[j + 1, k0 + kk] for kk in range(8))

            n_cols = jnp.maximum(i - k0, 0)
            acc = lax.fori_loop(0, n_cols // 2, body, init)
            odd = n_cols % 2 == 1
            j_last = jnp.where(odd, i - 1, 0)
            m_last = jnp.where(odd, m_at(j_last), 0.0)
            acc = tuple(acc[kk] + m_last * n_ref[j_last, k0 + kk] for kk in range(8))
            for kk in range(8):
                val = jnp.where(k0 + kk < i, -acc[kk], 0.0)
                n_ref[i, k0 + kk] = val
                nrow_ref[(k0 + kk) * nslots:(k0 + kk + 1) * nslots, :] = val
        for g in range(nslots // N_HEADS_B):
            for lt in range(WIDTH_B // LANES):
                slot0 = g * N_HEADS_B + lt * heads_per_tile
                parts = [nrow_ref[pl.ds(slot0 + h2, CHUNK, stride=nslots), :] for h2 in range(heads_per_tile)]
                out_ref[ii, lt, g * LANES:(g + 1) * LANES, :] = jnp.concatenate(parts, axis=0).T


def _tinv(m4):
    _, n_lt, nchunk, _ = m4.shape
    assert nchunk % LANES == 0
    nslots = (nchunk // LANES) * N_HEADS_B
    blk = pl.BlockSpec((8, n_lt, nchunk, LANES), lambda i: (i, 0, 0, 0))
    return pl.pallas_call(
        _tinv_kernel,
        grid=(CHUNK // 8,),
        in_specs=[blk],
        out_specs=blk,
        out_shape=jax.ShapeDtypeStruct(m4.shape, F32),
        scratch_shapes=[pltpu.VMEM((CHUNK, CHUNK, nslots, LANES), F32), pltpu.VMEM((8, CHUNK * nslots, LANES), F32),
                        pltpu.VMEM((CHUNK * nslots, LANES), F32)],
        compiler_params=_params("arbitrary"),
        name="gdn_tinv",
    )(m4)


def _gdn_scan_kernel(units, n_ref, vb_ref, kbe_ref, attn_ref, qd_ref, kd_ref, egl_ref, gate_ref, gn_ref, ones_ref,
                     ob_ref, s_out_ref, s_ref, n_scr):
    tm = vb_ref.shape[0]
    nc = tm // CHUNK
    n_lt = WIDTH_B // LANES

    @pl.when(pl.program_id(0) == 0)
    def _():
        s_ref[...] = jnp.zeros_like(s_ref)

    n_scr[...] = n_ref[...].reshape(n_scr.shape)

    row = lax.broadcasted_iota(jnp.int32, (CHUNK, WIDTH_B), 0)
    col = lax.broadcasted_iota(jnp.int32, (CHUNK, WIDTH_B), 1) % CHUNK
    eye = (row == col).astype(F32)
    r4 = lax.broadcasted_iota(jnp.int32, (WIDTH_B, WIDTH_B), 0) // HEAD_DIM
    c4 = lax.broadcasted_iota(jnp.int32, (WIDTH_B, WIDTH_B), 1) // HEAD_DIM
    bd_mask = r4 == c4
    s = s_ref[...]
    for c in range(tm // CHUNK):
        sl = slice(c * CHUNK, (c + 1) * CHUNK)
        n_c = jnp.concatenate([n_scr[pl.ds(lt * nc + c, CHUNK, stride=nc * n_lt), :] for lt in range(n_lt)], axis=1)
        tinv = (n_c + eye).astype(BF16)
        u = _dot(tinv, _block_diag(vb_ref[sl, :], bd_mask))
        w = _dot(tinv, _block_diag(kbe_ref[sl, :], bd_mask))
        lhs = jnp.concatenate([w.astype(BF16), qd_ref[sl, :]], axis=0)
        ws = _dot(lhs, s.astype(BF16))
        v_new = (u - ws[0:CHUNK]).astype(BF16)
        o = ws[CHUNK:2 * CHUNK] + _dot(attn_ref[sl, :], _block_diag(v_new, bd_mask))
        s = s * egl_ref[c:c + 1, :] + jnp.where(bd_mask, _dot_tn(kd_ref[sl, :], v_new), 0.0)
        ms = _dot_split(o * o, ones_ref[...], 2) * (1.0 / HEAD_DIM)
        ob_ref[sl, :] = (o * lax.rsqrt(ms + EPS) * gn_ref[...] * _silu(gate_ref[sl, :])).astype(BF16)
        _run_share(units, c, nc)
    s_ref[...] = s
    s_out_ref[...] = s


def _gdn_scan(n4, vb, kbe, attn, qd, kd, egl, gate, gn_row, rider=None):
    t = vb.shape[0]
    tm = ROW_TILE
    nc = tm // CHUNK
    n_lt = WIDTH_B // LANES
    row = lambda w: pl.BlockSpec((tm, w), lambda i: (i, 0))
    return _call_with_rider(
        _gdn_scan_kernel, rider,
        grid=(t // tm,),
        in_specs=[pl.BlockSpec((CHUNK, n_lt, nc, LANES), lambda i: (0, 0, i, 0))] + [row(WIDTH_B)] * 5
        + [pl.BlockSpec((nc, WIDTH_B), lambda i: (i, 0)), row(WIDTH_B), _full((1, WIDTH_B)), _full((WIDTH_B, WIDTH_B))],
        out_specs=[row(WIDTH_B), _full((WIDTH_B, WIDTH_B))],
        out_shape=[jax.ShapeDtypeStruct((t, WIDTH_B), BF16), jax.ShapeDtypeStruct((WIDTH_B, WIDTH_B), F32)],
        scratch_shapes=[pltpu.VMEM((WIDTH_B, WIDTH_B), F32), pltpu.VMEM((CHUNK * n_lt * nc, LANES), F32)],
        args=(n4, vb, kbe, attn, qd, kd, egl, gate, gn_row, _group_ones(WIDTH_B)),
        sem="arbitrary", name="gdn_scan")


def _gdn_prompt(qkv, ab, gate, conv_w, alog_row, dtb_row, gn_row, riders=(None, None)):
    (m4, attn, qd, kd, vb, kbe, egl), ride0 = _gdn_prep(qkv, ab, conv_w, alog_row, dtb_row, riders[0])
    (ob, s_bd), ride1 = _gdn_scan(_tinv(m4), vb, kbe, attn, qd, kd, egl, gate, gn_row, riders[1])
    s_fin = jnp.stack([s_bd[h * HEAD_DIM:(h + 1) * HEAD_DIM, h * HEAD_DIM:(h + 1) * HEAD_DIM] for h in range(N_HEADS_B)])
    return ob, s_fin, (ride0, ride1)


def _ffn_kernel(shift, carry_rows, x_ref, oa_ref, ob_ref, oc_ref, wo_ref, g_ref, wup_ref, cw_ref, wdn_ref, c0_ref,
                y_ref, last_ref, carry_ref, ext_ref, hid_ref):
    tm = x_ref.shape[0]

    @pl.when(pl.program_id(0) == 0)
    def _():
        carry_ref[...] = c0_ref[...]

    mix = _dot(oa_ref[...], wo_ref[0:WIDTH_A, :])
    mix = mix + _dot(ob_ref[...], wo_ref[WIDTH_A:WIDTH_A + WIDTH_B, :])
    mix = mix + _dot(oc_ref[...], wo_ref[WIDTH_A + WIDTH_B:, :])
    x = x_ref[...] + mix
    ms = jnp.mean(x * x, axis=-1, keepdims=True)
    xn = (x * lax.rsqrt(ms + EPS) * g_ref[...]).astype(BF16)
    tf = FF_TILE
    base = carry_rows
    for j in range(D_FF // tf):
        halves = []
        for half in range(2):
            cols = slice(half * D_FF + j * tf, half * D_FF + (j + 1) * tf)
            up = _dot(xn, wup_ref[:, cols])
            ext_ref[0:base, :] = carry_ref[:, cols]
            ext_ref[base:base + tm, :] = up
            carry_ref[:, cols] = ext_ref[tm:tm + base, :]
            cw = cw_ref[:, cols]
            halves.append(cw[0:1] * ext_ref[base - 2 * shift:base - 2 * shift + tm, :]
                          + cw[1:2] * ext_ref[base - shift:base - shift + tm, :] + cw[2:3] * up)
        hid_ref[:, j * tf:(j + 1) * tf] = (_silu(halves[0]) * halves[1]).astype(BF16)
    y_ref[...] = x + _dot(hid_ref[...], wdn_ref[...])
    last_ref[...] = carry_ref[...]


def _ffn(x, oa, ob, oc, w_out, g, w_up, conv_w, w_down, carry0, shift):
    rows = x.shape[0]
    tm = ROW_TILE
    carry_rows = carry0.shape[0]
    row = lambda w: pl.BlockSpec((tm, w), lambda i: (i, 0))
    once = lambda shape: pl.BlockSpec(shape, lambda i: (0,) * len(shape), pipeline_mode=pl.Buffered(1))
    return pl.pallas_call(
        functools.partial(_ffn_kernel, shift, carry_rows),
        grid=(rows // tm,),
        in_specs=[row(D_MODEL), row(WIDTH_A), row(WIDTH_B), row(WIDTH_C), once((D_MODEL, D_MODEL)),
                  once((1, D_MODEL)), once((D_MODEL, 2 * D_FF)), once((3, 2 * D_FF)),
                  once((D_FF, D_MODEL)), once((carry_rows, 2 * D_FF))],
        out_specs=[row(D_MODEL), _full((carry_rows, 2 * D_FF))],
        out_shape=[jax.ShapeDtypeStruct((rows, D_MODEL), F32), jax.ShapeDtypeStruct((carry_rows, 2 * D_FF), F32)],
        scratch_shapes=[pltpu.VMEM((carry_rows, 2 * D_FF), F32), pltpu.VMEM((tm + carry_rows, FF_TILE), F32),
                        pltpu.VMEM((tm, D_FF), BF16)],
        compiler_params=_params("arbitrary"),
        name=f"ffn_shift{shift}",
    )(x, oa, ob, oc, w_out, g, w_up, conv_w, w_down, carry0)


PAIR_ROWS = 16
NEW_ROWS = 16


def _sample_bias(wbuf, s_len):
    def table(key_pos):
        q_pos = wbuf + np.arange(s_len)
        dist = q_pos[:, None] - key_pos[None, :]
        mult = np.zeros(dist.shape)
        for window, dil in PATTERNS:
            mult += (dist >= 0) & (dist <= window) & (dist % dil == 0)
        logm = np.log(np.maximum(mult, 1))
        bias = -_alibi_slopes()[:, None, None] * np.maximum(dist, 0)[None] + logm[None]
        bias = np.where(mult[None] > 0, bias, NEG)
        out = np.zeros((N_HEADS_A, PAIR_ROWS // 2, key_pos.shape[0]))
        out[:, :s_len] = bias
        return out.reshape(N_HEADS_A // 2, PAIR_ROWS, -1)
    main = table(np.arange(wbuf))
    new = table(np.concatenate([wbuf + np.arange(s_len), np.full(NEW_ROWS - s_len, 10 ** 6)]))
    new[:, :, s_len:] = NEG
    return jnp.asarray(main, F32), jnp.asarray(new, F32)


def _sample_attn_units(q_ref, kn_ref, vn_ref, kt_ref, vt_ref, bm_ref, bn_ref, qc_ref, mkt_ref, mvt_ref, oa_ref, oc_ref):
    def window(b):
        scores = []
        for pair in range(N_HEADS_A // 2):
            rows = slice(pair * LANES, (pair + 1) * LANES)
            q = q_ref[b, pair]
            scores.append((_dot(q, kt_ref[b, rows, :].astype(BF16)) + bm_ref[pair],
                           _dot_nt(q, kn_ref[b, :, rows].astype(BF16)) + bn_ref[pair]))
        for pair, (s_main, s_new) in enumerate(scores):
            rows = slice(pair * LANES, (pair + 1) * LANES)
            m = jnp.maximum(jnp.max(s_main, axis=-1, keepdims=True), jnp.max(s_new, axis=-1, keepdims=True))
            p_main = jnp.exp(s_main - m)
            p_new = jnp.exp(s_new - m)
            den = jnp.sum(p_main, axis=-1, keepdims=True) + jnp.sum(p_new, axis=-1, keepdims=True)
            num = _dot_nt(p_main.astype(BF16), vt_ref[b, rows, :].astype(BF16))
            num = num + _dot(p_new.astype(BF16), vn_ref[b, :, rows].astype(BF16))
            oa_ref[b, pair] = num / den

    def memory(b):
        for pair in range(N_HEADS_C // 2):
            rows = slice(pair * LANES, (pair + 1) * LANES)
            q = qc_ref[b, pair]
            s = _dot(q, mkt_ref[b, rows, :].astype(BF16))
            p = jnp.exp(s - jnp.max(s, axis=-1, keepdims=True))
            num = _dot_nt(p.astype(BF16), mvt_ref[b, rows, :].astype(BF16))
            oc_ref[b, pair] = num / jnp.sum(p, axis=-1, keepdims=True)

    units = []
    for b in range(q_ref.shape[0]):
        units += [functools.partial(window, b), functools.partial(memory, b)]
    return units


def _run_share(units, stage, n_stages):
    for u in units[stage * len(units) // n_stages:(stage + 1) * len(units) // n_stages]:
        u()


def _pair_rows(q, s_len):
    nb, _, nh, hd = q.shape
    q = q.reshape(nb, s_len, nh // 2, 2, hd).transpose(0, 2, 3, 1, 4)
    q = jnp.pad(q, ((0, 0), (0, 0), (0, 0), (0, PAIR_ROWS // 2 - s_len), (0, 0)))
    own = jnp.eye(2, dtype=q.dtype)[None, None, :, None, :, None]
    return (q[:, :, :, :, None, :] * own).reshape(nb, nh // 2, PAIR_ROWS, 2 * hd)


def _unpair_rows(o, s_len):
    nb, npair = o.shape[:2]
    o = o.reshape(nb, npair, 2, PAIR_ROWS // 2, 2, HEAD_DIM)
    o = jnp.stack([o[:, :, half, :s_len, half, :] for half in range(2)], axis=2)
    return o.transpose(0, 3, 1, 2, 4).reshape(nb, s_len, npair * 2 * HEAD_DIM)


def _sample_attn_riders(qa, ka, va, cache_k, cache_v, qc, mem_k, mem_v, n_parts, steps):
    nb, s_len = qa.shape[:2]
    wbuf = cache_k.shape[1]
    n_mem = mem_k.shape[1]
    assert s_len <= PAIR_ROWS // 2 and nb % (n_parts * steps) == 0
    g = nb // (n_parts * steps)
    bm, bn = _sample_bias(wbuf, s_len)
    transposed = lambda a: a.transpose(0, 2, 3, 1).reshape(nb, a.shape[2] * a.shape[3], a.shape[1])
    pad_new = lambda a: jnp.pad(a, ((0, 0), (0, NEW_ROWS - s_len), (0, 0)))
    args = [_pair_rows(qa, s_len), pad_new(ka), pad_new(va), transposed(cache_k), transposed(cache_v), bm, bn,
            _pair_rows(qc, s_len), transposed(mem_k), transposed(mem_v)]
    riders = []
    for part in range(n_parts):
        per_b = lambda shape, part=part: pl.BlockSpec((g,) + shape, lambda i: (i + part * steps,) + (0,) * len(shape))
        out_b = lambda shape: pl.BlockSpec((g,) + shape, lambda i: (i,) + (0,) * len(shape))
        riders.append(dict(
            units=_sample_attn_units,
            in_specs=[per_b((N_HEADS_A // 2, PAIR_ROWS, LANES)), per_b((NEW_ROWS, WIDTH_A)), per_b((NEW_ROWS, WIDTH_A)),
                      per_b((WIDTH_A, wbuf)), per_b((WIDTH_A, wbuf)), _full(bm.shape), _full(bn.shape),
                      per_b((N_HEADS_C // 2, PAIR_ROWS, LANES)), per_b((WIDTH_C, n_mem)), per_b((WIDTH_C, n_mem))],
            out_specs=[out_b((N_HEADS_A // 2, PAIR_ROWS, LANES)), out_b((N_HEADS_C // 2, PAIR_ROWS, LANES))],
            out_shape=[jax.ShapeDtypeStruct((g * steps, N_HEADS_A // 2, PAIR_ROWS, LANES), F32),
                       jax.ShapeDtypeStruct((g * steps, N_HEADS_C // 2, PAIR_ROWS, LANES), F32)],
            args=args))

    def finish(rider_outs):
        oa = jnp.concatenate([o[0] for o in rider_outs], axis=0)
        oc = jnp.concatenate([o[1] for o in rider_outs], axis=0)
        return _unpair_rows(oa, s_len).astype(BF16), _unpair_rows(oc, s_len).astype(BF16)

    return riders, finish


def _sample_gdn_kernel(xe_ref, cw_ref, a_ref, b_ref, alog_ref, dtb_ref, gate_ref, gn_ref, s_ref, o_ref, s_out_ref,
                       q_scr, k_scr, v_scr):
    s_len = o_ref.shape[0]
    s_out_ref[...] = s_ref[...]
    for t in range(s_len):
        for part, scr in enumerate((q_scr, k_scr, v_scr)):
            y = sum(cw_ref[j, part] * xe_ref[t + j, part] for j in range(4))
            y = _silu(y)
            if part < 2:
                y = y * lax.rsqrt(jnp.sum(y * y, axis=0, keepdims=True) + EPS)
            if part == 0:
                y = y * (HEAD_DIM ** -0.5)
            scr[...] = y
        decay = jnp.exp(-jnp.exp(alog_ref[...]) * _softplus(a_ref[t] + dtb_ref[...]))
        beta = _sigmoid(b_ref[t])

        def read_body(dk, r):
            return r + k_scr[pl.ds(dk, 1), :] * s_out_ref[dk]

        r = lax.fori_loop(0, HEAD_DIM, read_body, jnp.zeros(v_scr.shape, F32), unroll=8)
        v_new = beta * (v_scr[...] - decay * r)

        def write_body(dk, o):
            s_new = decay * s_out_ref[dk] + k_scr[pl.ds(dk, 1), :] * v_new
            s_out_ref[dk] = s_new
            return o + q_scr[pl.ds(dk, 1), :] * s_new

        o = lax.fori_loop(0, HEAD_DIM, write_body, jnp.zeros(v_scr.shape, F32), unroll=8)
        ms = jnp.mean(o * o, axis=0, keepdims=True)
        o_ref[t] = o * lax.rsqrt(ms + EPS) * gn_ref[...] * _silu(gate_ref[t])


def _sample_gdn(xe, conv_w, a_b, b_b, a_log, dt_bias, gate, out_norm, state):
    nb, ext, _ = xe.shape
    s_len = ext - 3
    nh, hd = N_HEADS_B, HEAD_DIM
    xe_t = xe.reshape(nb, ext, 3, nh, hd).transpose(1, 2, 3, 4, 0)
    cw_t = jnp.broadcast_to(conv_w.reshape(4, 3, nh, hd)[..., None], (4, 3, nh, hd, nb))
    a_t = a_b.transpose(1, 2, 0).reshape(s_len, nh, 1, nb)
    b_t = b_b.transpose(1, 2, 0).reshape(s_len, nh, 1, nb)
    alog_t = jnp.broadcast_to(a_log.reshape(nh, 1, 1), (nh, 1, nb))
    dtb_t = jnp.broadcast_to(dt_bias.reshape(nh, 1, 1), (nh, 1, nb))
    gate_t = gate.reshape(nb, s_len, nh, hd).transpose(1, 2, 3, 0)
    gn_t = jnp.broadcast_to(out_norm.reshape(1, hd, 1), (1, hd, nb))
    s_t = state.transpose(1, 2, 3, 0)
    o_t, s_new = pl.pallas_call(
        _sample_gdn_kernel,
        grid=(nh,),
        in_specs=[pl.BlockSpec((ext, 3, None, hd, nb), lambda h: (0, 0, h, 0, 0)),
                  pl.BlockSpec((4, 3, None, hd, nb), lambda h: (0, 0, h, 0, 0)),
                  pl.BlockSpec((s_len, None, 1, nb), lambda h: (0, h, 0, 0)),
                  pl.BlockSpec((s_len, None, 1, nb), lambda h: (0, h, 0, 0)),
                  pl.BlockSpec((None, 1, nb), lambda h: (h, 0, 0)),
                  pl.BlockSpec((None, 1, nb), lambda h: (h, 0, 0)),
                  pl.BlockSpec((s_len, None, hd, nb), lambda h: (0, h, 0, 0)),
                  pl.BlockSpec((None, hd, nb), lambda h: (0, 0, 0)),
                  pl.BlockSpec((None, hd, hd, nb), lambda h: (h, 0, 0, 0))],
        out_specs=[pl.BlockSpec((s_len, None, hd, nb), lambda h: (0, h, 0, 0)),
                   pl.BlockSpec((None, hd, hd, nb), lambda h: (h, 0, 0, 0))],
        out_shape=[jax.ShapeDtypeStruct((s_len, nh, hd, nb), F32), jax.ShapeDtypeStruct((nh, hd, hd, nb), F32)],
        scratch_shapes=[pltpu.VMEM((hd, nb), F32)] * 3,
        compiler_params=_params("parallel"),
        name="sample_gdn",
    )(xe_t, cw_t, a_t, b_t, alog_t, dtb_t, gate_t, gn_t, s_t)
    ob = o_t.transpose(3, 0, 1, 2).reshape(nb * s_len, nh * hd).astype(BF16)
    return ob, s_new.transpose(3, 0, 1, 2)


def _lane_row(v, width=LANES):
    return jnp.zeros((1, width), F32).at[0, :v.shape[0]].set(v)


def kernel(x_prompt, x_sample, cache_win_k, cache_win_v, state_gdn, state_gdn_conv, state_ffn_conv, cache_mem_k,
           cache_mem_v, mem_prompt, norm1_g, w_in, q_norm_a, k_norm_a, conv_b_w, a_log_b, dt_bias_b, out_norm_b,
           mem_norm_g, w_mem_kv, q_norm_c, k_norm_c, w_out, norm2_g, w_up, conv_ffn_w, w_down):
    depth = norm1_g.shape[0]
    assert depth == 1 and x_prompt.shape[0] == 1
    l = 0
    t_p = x_prompt.shape[1]
    nb, s_len = x_sample.shape[:2]
    xp = x_prompt.reshape(t_p, D_MODEL)
    xs = x_sample.reshape(nb * s_len, D_MODEL)

    w_in_p = w_in[l].T.astype(BF16)
    w_out_b = w_out[l].astype(BF16)
    w_up_b = w_up[l].astype(BF16)
    w_down_b = w_down[l].astype(BF16)
    g1 = norm1_g[l].reshape(1, D_MODEL)
    g2 = norm2_g[l].reshape(1, D_MODEL)
    gq = jnp.tile(q_norm_a[l], N_HEADS_A).reshape(1, WIDTH_A)
    gk = jnp.tile(k_norm_a[l], N_HEADS_A).reshape(1, WIDTH_A)
    gqc = jnp.tile(q_norm_c[l], N_HEADS_C).reshape(1, WIDTH_C)
    gkc = jnp.tile(k_norm_c[l], N_HEADS_C).reshape(1, WIDTH_C)
    gn_row = jnp.tile(out_norm_b[l], N_HEADS_B).reshape(1, WIDTH_B)

    rows_s = nb * s_len
    (qa_s,), _, _, (ka_s, va_s, qkv_s, gate_s, qc_s, ab_s) = _inproj(xs, g1, w_in_p, gq, gk, gqc, (1,), rows_s)
    b3 = lambda a: a.reshape(nb, s_len, a.shape[-1])
    heads = lambda a: a.reshape(nb, s_len, -1, HEAD_DIM)
    riders, finish_sample_attn = _sample_attn_riders(
        heads(qa_s), b3(ka_s), b3(va_s), cache_win_k[l], cache_win_v[l], heads(qc_s), cache_mem_k[l], cache_mem_v[l],
        n_parts=2, steps=t_p // ROW_TILE)

    n_keep = min(MAX_WINDOW, t_p)
    q_ds, k_ds, v_ds, (ka, va, qkv, gate, qc, ab) = _inproj(xp, g1, w_in_p, gq, gk, gqc, DILATIONS, n_keep)
    parts = [_attn_band(q_d, k_d, v_d, dil) for q_d, k_d, v_d, dil in zip(q_ds, k_ds, v_ds, DILATIONS)]
    mk, mv = _memkv(mem_prompt[0], mem_norm_g[l].reshape(1, D_MODEL), w_mem_kv[l].astype(BF16), gkc)
    oa, oc = _combine_mem([p[0] for p in parts], [p[1] for p in parts], DILATIONS, qc, mk, mv)
    ob, gdn_p, rider_outs = _gdn_prompt(qkv, ab, gate, conv_b_w[l], _lane_row(a_log_b[l]), _lane_row(dt_bias_b[l]),
                                        gn_row, riders)
    y_p, last_p = _ffn(xp, oa, ob, oc, w_out_b, g2, w_up_b, conv_ffn_w[l], w_down_b, jnp.zeros((8, 2 * D_FF), F32), 1)
    win_k_p = ka.reshape(1, 1, n_keep, N_HEADS_A, HEAD_DIM)
    win_v_p = va.reshape(1, 1, n_keep, N_HEADS_A, HEAD_DIM)
    gconv_p = qkv[t_p - 3:].reshape(1, 1, 3, QKV_B)
    fconv_p = last_p[6:8].reshape(1, 1, 2, 2 * D_FF)

    oa_s, oc_s = finish_sample_attn(rider_outs)
    xe_s = jnp.concatenate([state_gdn_conv[l], b3(qkv_s)], axis=1)
    ab3 = b3(ab_s)
    ob_s, gdn_s = _sample_gdn(xe_s, conv_b_w[l], ab3[..., 0:N_HEADS_B], ab3[..., N_HEADS_B:2 * N_HEADS_B], a_log_b[l],
                              dt_bias_b[l], b3(gate_s), out_norm_b[l], state_gdn[l])
    time_major = lambda a: a.reshape(nb, s_len, a.shape[-1]).transpose(1, 0, 2).reshape(rows_s, a.shape[-1])
    carry_s = state_ffn_conv[l].transpose(1, 0, 2).reshape(2 * nb, 2 * D_FF)
    y_t, last_s = _ffn(time_major(xs), time_major(oa_s), time_major(ob_s), time_major(oc_s), w_out_b, g2, w_up_b,
                       conv_ffn_w[l], w_down_b, carry_s, nb)
    y_s = y_t.reshape(s_len, nb, D_MODEL).transpose(1, 0, 2)
    fconv_s = last_s.reshape(2, nb, 2 * D_FF).transpose(1, 0, 2)[None]

    return (y_p.reshape(1, t_p, D_MODEL), y_s,
            win_k_p, win_v_p,
            ka_s.reshape(1, nb, s_len, N_HEADS_A, HEAD_DIM), va_s.reshape(1, nb, s_len, N_HEADS_A, HEAD_DIM),
            gdn_p[None, None], gdn_s[None],
            gconv_p, xe_s[:, -3:][None],
            fconv_p, fconv_s,
            mk.reshape(1, 1, N_MEM, N_HEADS_C, HEAD_DIM), mv.reshape(1, 1, N_MEM, N_HEADS_C, HEAD_DIM))
```

```python
import functools

import numpy as np
import jax
import jax.numpy as jnp
from jax import lax
from jax.experimental import pallas as pl
from jax.experimental.pallas import tpu as pltpu

F32 = jnp.float32
BF16 = jnp.bfloat16
EPS = 1e-6
NEG = -1e30
LOG2E = 1.4426950408889634

D_MODEL = 1024
HEAD_DIM = 64
N_HEADS_A = 8
PATTERNS = ((128, 1), (512, 4), (2048, 16))
DILATIONS = tuple(d for _, d in PATTERNS)
MAX_WINDOW = 2048
N_HEADS_B = 4
N_HEADS_C = 4
N_MEM = 256
D_FF = 2816
WIDTH_A = N_HEADS_A * HEAD_DIM
WIDTH_B = N_HEADS_B * HEAD_DIM
WIDTH_C = N_HEADS_C * HEAD_DIM
QKV_B = 3 * WIDTH_B
CHUNK = 64
NBAND = 128
BAND_BLOCKS = 4
ROW_TILE = 512
FFN_ROW_TILE = 512
FF_TILE = 256
LANES = 128
VMEM_LIMIT = 56 * 1024 * 1024

SEG_QA, SEG_KA, SEG_VA = (0, 512), (512, 1024), (1024, 1536)
SEG_QKV, SEG_GATE, SEG_TAIL = (1536, 2304), (2304, 2560), (2560, 2824)
N_IN = 2824


def _params(*sem):
    return pltpu.CompilerParams(dimension_semantics=sem, vmem_limit_bytes=VMEM_LIMIT)


def _dot(a, b):
    return jnp.dot(a, b, preferred_element_type=F32)


def _dot_nt(a, b):
    return lax.dot_general(a, b, (((1,), (1,)), ((), ())), preferred_element_type=F32)


def _dot_tn(a, b):
    return lax.dot_general(a, b, (((0,), (0,)), ((), ())), preferred_element_type=F32)


def _dot_split(a, b, parts, left=False):
    acc = None
    rem = a
    for _ in range(parts):
        piece = rem.astype(BF16)
        term = _dot(b, piece) if left else _dot(piece, b)
        acc = term if acc is None else acc + term
        rem = rem - piece.astype(F32)
    return acc


def _sigmoid(x):
    return 1.0 / (1.0 + jnp.exp(-x))


def _silu(x):
    return x * _sigmoid(x)


def _softplus(x):
    return jnp.maximum(x, 0.0) + jnp.log1p(jnp.exp(-jnp.abs(x)))


def _group_ones(width, group=HEAD_DIM):
    i = np.arange(width)
    return jnp.asarray((i[:, None] // group) == (i[None, :] // group), BF16)


def _full(shape):
    nd = len(shape)
    return pl.BlockSpec(shape, lambda *_: (0,) * nd)


def _call_with_rider(host_kernel, rider, *, grid, in_specs, out_specs, out_shape, scratch_shapes, args, sem, name):
    if rider is None:
        rider = dict(units=None, in_specs=[], out_specs=[], out_shape=[], args=[])
    n_hi, n_ri, n_ho, n_ro = len(in_specs), len(rider["in_specs"]), len(out_specs), len(rider["out_specs"])

    def body(*refs):
        host_in, refs = refs[:n_hi], refs[n_hi:]
        rider_in, refs = refs[:n_ri], refs[n_ri:]
        host_out, refs = refs[:n_ho], refs[n_ho:]
        rider_out, scratch = refs[:n_ro], refs[n_ro:]
        units = rider["units"](*rider_in, *rider_out) if rider["units"] is not None else []
        host_kernel(units, *host_in, *host_out, *scratch)

    outs = pl.pallas_call(
        body, grid=grid, in_specs=list(in_specs) + rider["in_specs"], out_specs=list(out_specs) + rider["out_specs"],
        out_shape=list(out_shape) + rider["out_shape"], scratch_shapes=scratch_shapes,
        compiler_params=_params(sem), name=name,
    )(*args, *rider["args"])
    return outs[:n_ho], outs[n_ho:]


def _inproj_kernel(dils, scale, x_ref, g1_ref, w_ref, gq_ref, gk_ref, gc_ref, ga_ref, gcc_ref, *refs):
    nd = len(dils)
    q_refs, k_refs, v_refs = refs[0:nd], refs[nd:2 * nd], refs[2 * nd:3 * nd]
    ka_ref, va_ref, qkv_ref, gate_ref, qc_ref, ab_ref, scr_ref = refs[3 * nd:]
    tm = x_ref.shape[0]
    x = x_ref[...]
    ms = jnp.mean(x * x, axis=-1, keepdims=True)
    xn = (x * lax.rsqrt(ms + EPS) * g1_ref[...]).astype(BF16)

    def seg(s):
        return _dot_nt(xn, w_ref[s[0]:s[1], :])

    def head_norm(z, ones_ref, gain):
        ss = _dot_split(z * z, ones_ref[...], 2) * (1.0 / HEAD_DIM)
        return z * lax.rsqrt(ss + EPS) * gain

    def emit(z, out_refs):
        n_tiles = WIDTH_A // LANES
        for c in range(n_tiles):
            scr_ref[c] = z[:, c * LANES:(c + 1) * LANES]
        for d, ref in zip(dils, out_refs):
            if d == 1:
                ref[...] = z.astype(BF16)
            else:
                for r in range(d):
                    for c in range(n_tiles):
                        col = r * WIDTH_A + c * LANES
                        ref[:, col:col + LANES] = scr_ref[c, pl.ds(r, tm // d, stride=d), :].astype(BF16)

    emit(head_norm(seg(SEG_QA), ga_ref, gq_ref[...]) * scale, q_refs)
    ka = head_norm(seg(SEG_KA), ga_ref, gk_ref[...])
    ka_ref[...] = ka
    emit(ka, k_refs)
    va = seg(SEG_VA)
    va_ref[...] = va
    emit(va, v_refs)
    qkv_ref[...] = seg(SEG_QKV)
    gate_ref[...] = seg(SEG_GATE)
    tail = seg(SEG_TAIL)
    ab_ref[...] = tail[:, 0:LANES]
    qc = tail[:, 2 * N_HEADS_B:2 * N_HEADS_B + WIDTH_C]
    qc_ref[...] = (head_norm(qc, gcc_ref, gc_ref[...]) * scale).astype(BF16)


def _inproj(x, g1, w_in_p, gq, gk, gc, dils, keep_rows, scale):
    rows = x.shape[0]
    tm = ROW_TILE
    nt = rows // tm
    skip = nt - keep_rows // tm
    row = lambda w: pl.BlockSpec((tm, w), lambda i: (i, 0))
    tail = pl.BlockSpec((tm, WIDTH_A), lambda i: (jnp.maximum(i - skip, 0), 0))
    dil_specs = [pl.BlockSpec((tm // d, d * WIDTH_A), lambda i: (i, 0)) for d in dils]
    dil_shapes = [jax.ShapeDtypeStruct((rows // d, d * WIDTH_A), BF16) for d in dils]
    outs = [(QKV_B, F32), (WIDTH_B, F32), (WIDTH_C, BF16), (LANES, F32)]
    res = pl.pallas_call(
        functools.partial(_inproj_kernel, dils, scale),
        grid=(nt,),
        in_specs=[row(D_MODEL), _full((1, D_MODEL)), _full((N_IN, D_MODEL)), _full((1, WIDTH_A)),
                  _full((1, WIDTH_A)), _full((1, WIDTH_C)), _full((WIDTH_A, WIDTH_A)), _full((WIDTH_C, WIDTH_C))],
        out_specs=dil_specs * 3 + [tail, tail] + [row(w) for w, _ in outs],
        out_shape=dil_shapes * 3 + [jax.ShapeDtypeStruct((keep_rows, WIDTH_A), F32)] * 2
        + [jax.ShapeDtypeStruct((rows, w), dt) for w, dt in outs],
        scratch_shapes=[pltpu.VMEM((WIDTH_A // LANES, tm, LANES), F32)],
        compiler_params=_params("arbitrary"),
        name="inproj",
    )(x, g1, w_in_p, gq, gk, gc, _group_ones(WIDTH_A), _group_ones(WIDTH_C))
    nd = len(dils)
    return res[0:nd], res[nd:2 * nd], res[2 * nd:3 * nd], res[3 * nd:]


def _alibi_slopes():
    return np.exp2(-8.0 * np.arange(1, N_HEADS_A + 1, dtype=np.float64) / N_HEADS_A)


def _band_bias(dil):
    qi = np.arange(NBAND)[:, None]
    kj = np.arange(2 * NBAND)[None, :]
    delta = qi + NBAND - kj
    in_band = (delta >= 0) & (delta <= NBAND)
    bias = -_alibi_slopes()[:, None, None] * (delta * dil)[None].astype(np.float64)
    general = np.where(in_band[None], bias, NEG)
    first = np.where((in_band & (kj >= NBAND))[None], bias, NEG)
    return jnp.asarray(np.stack([general, first]) * LOG2E, F32)


def _attn_band_kernel(q_ref, kp_ref, kc_ref, vp_ref, vc_ref, bias_ref, o_ref, lse_ref):
    lane = lax.broadcasted_iota(jnp.int32, (NBAND, LANES), 1)
    low = lane < HEAD_DIM
    ones = jnp.ones((2 * NBAND, LANES), BF16)
    for blk in range(BAND_BLOCKS):
        first = (pl.program_id(1) == 0).astype(jnp.int32) if blk == 0 else 0
        rows = slice(blk * NBAND, (blk + 1) * NBAND)
        lse = jnp.zeros((NBAND, LANES), F32)
        for pair in range(N_HEADS_A // 2):
            cols = slice(pair * LANES, (pair + 1) * LANES)
            q = q_ref[rows, cols]
            if blk == 0:
                k = jnp.concatenate([kp_ref[:, cols], kc_ref[rows, cols]], axis=0)
                v = jnp.concatenate([vp_ref[:, cols], vc_ref[rows, cols]], axis=0)
            else:
                k = kc_ref[(blk - 1) * NBAND:(blk + 1) * NBAND, cols]
                v = vc_ref[(blk - 1) * NBAND:(blk + 1) * NBAND, cols]
            v_ext = jnp.concatenate([v, ones], axis=1)
            outs = []
            for half in range(2):
                h = 2 * pair + half
                qm = jnp.where(low if half == 0 else ~low, q, jnp.zeros((), BF16))
                s = _dot_nt(qm, k) + bias_ref[first, h]
                m = jnp.max(s, axis=-1, keepdims=True)
                p = jnp.exp2(s - m).astype(BF16)
                r = _dot(p, v_ext)
                outs.append(r[:, :LANES])
                lse = jnp.where(lane == h, m, lse)
                lse = jnp.where(lane == N_HEADS_A + h, r[:, LANES:], lse)
            o_ref[rows, cols] = jnp.where(low, outs[0], outs[1]).astype(BF16)
        lse_ref[rows, :] = lse


def _attn_band(q_d, k_d, v_d, dil):
    rows = q_d.shape[0]
    step = BAND_BLOCKS * NBAND
    cur = lambda w: pl.BlockSpec((step, w), lambda r, n: (n, r))
    prev = lambda w: pl.BlockSpec((NBAND, w), lambda r, n: (jnp.maximum(n * BAND_BLOCKS - 1, 0), r))
    return pl.pallas_call(
        _attn_band_kernel,
        grid=(dil, rows // step),
        in_specs=[cur(WIDTH_A), prev(WIDTH_A), cur(WIDTH_A), prev(WIDTH_A), cur(WIDTH_A),
                  _full((2, N_HEADS_A, NBAND, 2 * NBAND))],
        out_specs=[cur(WIDTH_A), cur(LANES)],
        out_shape=[jax.ShapeDtypeStruct((rows, dil * WIDTH_A), BF16), jax.ShapeDtypeStruct((rows, dil * LANES), F32)],
        compiler_params=_params("parallel", "arbitrary"),
        name=f"attn_band_d{dil}",
    )(q_d, k_d, k_d, v_d, v_d, _band_bias(dil))


def _memkv_kernel(mem_ref, g_ref, w_ref, gk_ref, ones_ref, mk_ref, mv_ref):
    x = mem_ref[...]
    ms = jnp.mean(x * x, axis=-1, keepdims=True)
    xn = (x * lax.rsqrt(ms + EPS) * g_ref[...]).astype(BF16)
    zk = _dot(xn, w_ref[:, 0:WIDTH_C])
    ss = _dot_split(zk * zk, ones_ref[...], 2) * (1.0 / HEAD_DIM)
    mk_ref[...] = zk * lax.rsqrt(ss + EPS) * gk_ref[...]
    mv_ref[...] = _dot(xn, w_ref[:, WIDTH_C:2 * WIDTH_C])


def _memkv(mem, g, w, gk):
    n = mem.shape[0]
    return pl.pallas_call(
        _memkv_kernel,
        out_shape=[jax.ShapeDtypeStruct((n, WIDTH_C), F32)] * 2,
        compiler_params=pltpu.CompilerParams(vmem_limit_bytes=VMEM_LIMIT),
        name="memkv",
    )(mem, g, w, gk, _group_ones(WIDTH_C))


def _head_spread():
    e = np.zeros((LANES, WIDTH_A), np.float32)
    for h in range(N_HEADS_A):
        e[h, h * HEAD_DIM:(h + 1) * HEAD_DIM] = 1.0
    return jnp.asarray(e, BF16)


def _combine_mem_kernel(dils, *refs):
    nd = len(dils)
    o_refs, l_refs = refs[0:nd], refs[nd:2 * nd]
    qc_ref, mk_ref, mv_ref, e_ref, oa_ref, oc_ref, o_scr, l_scr = refs[2 * nd:]
    tm = oa_ref.shape[0]
    os_, ls = [], []
    for p, d in enumerate(dils):
        if d == 1:
            os_.append(o_refs[p][...].astype(F32))
            ls.append(l_refs[p][...])
            continue
        n_tiles = WIDTH_A // LANES
        for r in range(d):
            for c in range(n_tiles):
                col = r * WIDTH_A + c * LANES
                o_scr[p * n_tiles + c, pl.ds(r, tm // d, stride=d), :] = o_refs[p][:, col:col + LANES].astype(F32)
            l_scr[p, pl.ds(r, tm // d, stride=d), :] = l_refs[p][:, r * LANES:(r + 1) * LANES]
        os_.append(jnp.concatenate([o_scr[p * n_tiles + c] for c in range(n_tiles)], axis=1))
        ls.append(l_scr[p])
    head_lane = lax.broadcasted_iota(jnp.int32, (tm, LANES), 1) < N_HEADS_A
    m = functools.reduce(jnp.maximum, ls)
    es = [jnp.exp2(l - m) for l in ls]
    dens = [pltpu.roll(l, LANES - N_HEADS_A, 1) for l in ls]
    tot = functools.reduce(lambda a, b: a + b, [d * e for d, e in zip(dens, es)])
    tot = jnp.where(head_lane, tot, 1.0)
    acc = None
    for o, e in zip(os_, es):
        term = o * _dot_split(e / tot, e_ref[...], 2)
        acc = term if acc is None else acc + term
    oa_ref[...] = acc.astype(BF16)

    lane = lax.broadcasted_iota(jnp.int32, (tm, LANES), 1)
    low = lane < HEAD_DIM
    ones = jnp.ones((N_MEM, LANES), BF16)
    for pair in range(N_HEADS_C // 2):
        cols = slice(pair * LANES, (pair + 1) * LANES)
        q = qc_ref[:, cols]
        mk = mk_ref[:, cols].astype(BF16)
        v_ext = jnp.concatenate([mv_ref[:, cols].astype(BF16), ones], axis=1)
        outs = []
        for half in range(2):
            qm = jnp.where(low if half == 0 else ~low, q, jnp.zeros((), BF16))
            s = _dot_nt(qm, mk)
            p = jnp.exp2(s - jnp.max(s, axis=-1, keepdims=True)).astype(BF16)
            r = _dot(p, v_ext)
            outs.append(r[:, :LANES] / r[:, LANES:])
        oc_ref[:, cols] = jnp.where(low, outs[0], outs[1]).astype(BF16)


def _combine_mem(os_, lses, dils, qc, mk, mv):
    t = qc.shape[0]
    tm = ROW_TILE
    nd = len(dils)
    row = lambda w: pl.BlockSpec((tm, w), lambda i: (i, 0))
    o_specs = [pl.BlockSpec((tm // d, d * WIDTH_A), lambda i: (i, 0)) for d in dils]
    l_specs = [pl.BlockSpec((tm // d, d * LANES), lambda i: (i, 0)) for d in dils]
    return pl.pallas_call(
        functools.partial(_combine_mem_kernel, dils),
        grid=(t // tm,),
        in_specs=o_specs + l_specs + [row(WIDTH_C), _full((N_MEM, WIDTH_C)), _full((N_MEM, WIDTH_C)),
                                     _full((LANES, WIDTH_A))],
        out_specs=[row(WIDTH_A), row(WIDTH_C)],
        out_shape=[jax.ShapeDtypeStruct((t, WIDTH_A), BF16), jax.ShapeDtypeStruct((t, WIDTH_C), BF16)],
        scratch_shapes=[pltpu.VMEM((nd * WIDTH_A // LANES, tm, LANES), F32), pltpu.VMEM((nd, tm, LANES), F32)],
        compiler_params=_params("parallel"),
        name="combine_mem",
    )(*os_, *lses, qc, mk, mv, _head_spread())


def _head_lane_consts():
    e_g = np.zeros((LANES, WIDTH_B), np.float32)
    e_b = np.zeros((LANES, WIDTH_B), np.float32)
    for h in range(N_HEADS_B):
        e_g[h, h * HEAD_DIM:(h + 1) * HEAD_DIM] = 1.0
        e_b[N_HEADS_B + h, h * HEAD_DIM:(h + 1) * HEAD_DIM] = 1.0
    i = np.arange(CHUNK)
    ltri = (i[None, :] <= i[:, None]).astype(np.float32)
    return jnp.asarray(e_g, BF16), jnp.asarray(e_b, BF16), jnp.asarray(ltri, BF16), jnp.ones((CHUNK, CHUNK), BF16)


def _block_diag(x, mask):
    return jnp.where(mask, jnp.concatenate([x] * N_HEADS_B, axis=0), jnp.zeros((), x.dtype))


def _gdn_prep_kernel(units, x_ref, halo_ref, ab_ref, cw_ref, alog_ref, dtb_ref, ones_ref, eg_ref, eb_ref, ltri_ref, one64_ref,
                     m_ref, attn_ref, qd_ref, kd_ref, vb_ref, kbe_ref, egl_ref, xe_ref, m_scr):
    tm = x_ref.shape[0]
    nc = tm // CHUNK
    halo = jnp.where(pl.program_id(0) > 0, halo_ref[...], 0.0)
    xe_ref[0:8, :] = halo
    xe_ref[8:8 + tm, :] = x_ref[...]
    cw = cw_ref[...]
    y = cw[3:4] * xe_ref[8:8 + tm, :] + cw[2:3] * xe_ref[7:7 + tm, :] + cw[1:2] * xe_ref[6:6 + tm, :] \
        + cw[0:1] * xe_ref[5:5 + tm, :]
    y = _silu(y)

    def l2n(z):
        return z * lax.rsqrt(_dot_split(z * z, ones_ref[...], 2) + EPS)

    q = l2n(y[:, 0:WIDTH_B]) * (HEAD_DIM ** -0.5)
    k = l2n(y[:, WIDTH_B:2 * WIDTH_B])
    v = y[:, 2 * WIDTH_B:3 * WIDTH_B]
    ab = ab_ref[...]
    g = -jnp.exp(alog_ref[...]) * _softplus(ab + dtb_ref[...])
    gx = _dot_split(g, eg_ref[...], 2)
    bx = _dot_split(_sigmoid(ab), eb_ref[...], 2)

    row = lax.broadcasted_iota(jnp.int32, (CHUNK, WIDTH_B), 0)
    col = lax.broadcasted_iota(jnp.int32, (CHUNK, WIDTH_B), 1) % CHUNK
    diag = row == col
    r4 = lax.broadcasted_iota(jnp.int32, (WIDTH_B, WIDTH_B), 0) // HEAD_DIM
    c4 = lax.broadcasted_iota(jnp.int32, (WIDTH_B, WIDTH_B), 1) // HEAD_DIM
    bd_mask = r4 == c4
    for c in range(tm // CHUNK):
        sl = slice(c * CHUNK, (c + 1) * CHUNK)
        gc = _dot_split(gx[sl], ltri_ref[...], 2, left=True)
        gl = gc[CHUNK - 1:CHUNK, :]
        gcj = _dot_split(jnp.where(diag, gc, 0.0), one64_ref[...], 2, left=True)
        dec = jnp.exp(jnp.where(row >= col, gc - gcj, NEG))
        kc, qc, vc, bc = k[sl], q[sl], v[sl], bx[sl]
        kb = kc * bc
        lhs = jnp.concatenate([kb, qc], axis=0).astype(BF16)
        kkqk = _dot_nt(lhs, _block_diag(kc.astype(BF16), bd_mask))
        m_c = jnp.where(row > col, kkqk[0:CHUNK] * dec, 0.0)
        for lt in range(WIDTH_B // LANES):
            m_scr[pl.ds(lt * nc + c, CHUNK, stride=nc * WIDTH_B // LANES), :] = m_c[:, lt * LANES:(lt + 1) * LANES]
        attn_ref[sl, :] = (kkqk[CHUNK:2 * CHUNK] * dec).astype(BF16)
        egc = jnp.exp(gc)
        qd_ref[sl, :] = (qc * egc).astype(BF16)
        kd_ref[sl, :] = (kc * jnp.exp(gl - gc)).astype(BF16)
        vb_ref[sl, :] = (vc * bc).astype(BF16)
        kbe_ref[sl, :] = (kb * egc).astype(BF16)
        egl_ref[c:c + 1, :] = jnp.exp(gl)
        _run_share(units, c, nc)
    m_ref[...] = m_scr[...].reshape(m_ref.shape)


def _gdn_prep(qkv, ab, conv_w, alog_row, dtb_row, rider=None):
    t = qkv.shape[0]
    tm = ROW_TILE
    nc = tm // CHUNK
    assert nc == 8
    n_lt = WIDTH_B // LANES
    row = lambda w: pl.BlockSpec((tm, w), lambda i: (i, 0))
    m_spec = pl.BlockSpec((CHUNK, n_lt, nc, LANES), lambda i: (0, 0, i, 0))
    e_g, e_b, ltri, one64 = _head_lane_consts()
    outs = [BF16, BF16, BF16, BF16, BF16]
    return _call_with_rider(
        _gdn_prep_kernel, rider,
        grid=(t // tm,),
        in_specs=[row(QKV_B), pl.BlockSpec((8, QKV_B), lambda i: (jnp.maximum(i * (tm // 8) - 1, 0), 0)), row(LANES),
                  _full((4, QKV_B)), _full((1, LANES)), _full((1, LANES)), _full((WIDTH_B, WIDTH_B)),
                  _full((LANES, WIDTH_B)), _full((LANES, WIDTH_B)), _full((CHUNK, CHUNK)), _full((CHUNK, CHUNK))],
        out_specs=[m_spec] + [row(WIDTH_B)] * 5 + [pl.BlockSpec((nc, WIDTH_B), lambda i: (i, 0))],
        out_shape=[jax.ShapeDtypeStruct((CHUNK, n_lt, t // CHUNK, LANES), F32)]
        + [jax.ShapeDtypeStruct((t, WIDTH_B), dt) for dt in outs]
        + [jax.ShapeDtypeStruct((t // CHUNK, WIDTH_B), F32)],
        scratch_shapes=[pltpu.VMEM((tm + 8, QKV_B), F32), pltpu.VMEM((CHUNK * n_lt * nc, LANES), F32)],
        args=(qkv, qkv, ab, conv_w, alog_row, dtb_row, _group_ones(WIDTH_B), e_g, e_b, ltri, one64),
        sem="parallel", name="gdn_prep")


def _tinv_kernel(m_ref, out_ref, n_ref, mrow_ref, nrow_ref):
    nslots = n_ref.shape[2]
    heads_per_tile = LANES // CHUNK
    blk = pl.program_id(0)

    @pl.when(blk == 0)
    def _():
        n_ref[...] = jnp.zeros_like(n_ref)

    for ii in range(8):
        i = blk * 8 + ii
        for g in range(nslots // N_HEADS_B):
            for lt in range(WIDTH_B // LANES):
                xt = m_ref[ii, lt, g * LANES:(g + 1) * LANES, :].T
                for h2 in range(heads_per_tile):
                    slot = g * N_HEADS_B + lt * heads_per_tile + h2
                    mrow_ref[ii, pl.ds(slot, CHUNK, stride=nslots), :] = xt[h2 * CHUNK:(h2 + 1) * CHUNK, :]
        for kb in range(8):
            k0 = 8 * kb
            init = tuple(mrow_ref[ii, (k0 + kk) * nslots:(k0 + kk + 1) * nslots, :] for kk in range(8))

            def m_at(j, ii=ii):
                return mrow_ref[ii, pl.ds(pl.multiple_of(j * nslots, nslots), nslots), :]

            def body(jj, acc, k0=k0):
                j = k0 + 2 * jj
                m0, m1 = m_at(j), m_at(j + 1)
                return tuple(acc[kk] + m0 * n_ref[j, k0 + kk] + m1 * n_ref[j + 1, k0 + kk] for kk in range(8))

            n_cols = jnp.maximum(i - k0, 0)
            acc = lax.fori_loop(0, n_cols // 2, body, init)
            odd = n_cols % 2 == 1
            j_last = jnp.where(odd, i - 1, 0)
            m_last = jnp.where(odd, m_at(j_last), 0.0)
            acc = tuple(acc[kk] + m_last * n_ref[j_last, k0 + kk] for kk in range(8))
            for kk in range(8):
                val = jnp.where(k0 + kk < i, -acc[kk], 0.0)
                n_ref[i, k0 + kk] = val
                nrow_ref[(k0 + kk) * nslots:(k0 + kk + 1) * nslots, :] = val
        for g in range(nslots // N_HEADS_B):
            for lt in range(WIDTH_B // LANES):
                slot0 = g * N_HEADS_B + lt * heads_per_tile
                parts = [nrow_ref[pl.ds(slot0 + h2, CHUNK, stride=nslots), :] for h2 in range(heads_per_tile)]
                out_ref[ii, lt, g * LANES:(g + 1) * LANES, :] = jnp.concatenate(parts, axis=0).T


def _tinv(m4):
    _, n_lt, nchunk, _ = m4.shape
    assert nchunk % LANES == 0
    nslots = (nchunk // LANES) * N_HEADS_B
    blk = pl.BlockSpec((8, n_lt, nchunk, LANES), lambda i: (i, 0, 0, 0))
    return pl.pallas_call(
        _tinv_kernel,
        grid=(CHUNK // 8,),
        in_specs=[blk],
        out_specs=blk,
        out_shape=jax.ShapeDtypeStruct(m4.shape, F32),
        scratch_shapes=[pltpu.VMEM((CHUNK, CHUNK, nslots, LANES), F32), pltpu.VMEM((8, CHUNK * nslots, LANES), F32),
                        pltpu.VMEM((CHUNK * nslots, LANES), F32)],
        compiler_params=_params("arbitrary"),
        name="gdn_tinv",
    )(m4)


def _gdn_scan_kernel(units, n_ref, vb_ref, kbe_ref, attn_ref, qd_ref, kd_ref, egl_ref, gate_ref, gn_ref, ones_ref,
                     ob_ref, s_out_ref, s_ref, n_scr):
    tm = vb_ref.shape[0]
    nc = tm // CHUNK
    n_lt = WIDTH_B // LANES

    @pl.when(pl.program_id(0) == 0)
    def _():
        s_ref[...] = jnp.zeros_like(s_ref)

    n_scr[...] = n_ref[...].reshape(n_scr.shape)

    row = lax.broadcasted_iota(jnp.int32, (CHUNK, WIDTH_B), 0)
    col = lax.broadcasted_iota(jnp.int32, (CHUNK, WIDTH_B), 1) % CHUNK
    eye = (row == col).astype(F32)
    r4 = lax.broadcasted_iota(jnp.int32, (WIDTH_B, WIDTH_B), 0) // HEAD_DIM
    c4 = lax.broadcasted_iota(jnp.int32, (WIDTH_B, WIDTH_B), 1) // HEAD_DIM
    bd_mask = r4 == c4
    s = s_ref[...]
    for c in range(tm // CHUNK):
        sl = slice(c * CHUNK, (c + 1) * CHUNK)
        n_c = jnp.concatenate([n_scr[pl.ds(lt * nc + c, CHUNK, stride=nc * n_lt), :] for lt in range(n_lt)], axis=1)
        tinv = (n_c + eye).astype(BF16)
        u = _dot(tinv, _block_diag(vb_ref[sl, :], bd_mask))
        w = _dot(tinv, _block_diag(kbe_ref[sl, :], bd_mask))
        lhs = jnp.concatenate([w.astype(BF16), qd_ref[sl, :]], axis=0)
        ws = _dot(lhs, s.astype(BF16))
        v_new = (u - ws[0:CHUNK]).astype(BF16)
        o = ws[CHUNK:2 * CHUNK] + _dot(attn_ref[sl, :], _block_diag(v_new, bd_mask))
        s = s * egl_ref[c:c + 1, :] + jnp.where(bd_mask, _dot_tn(kd_ref[sl, :], v_new), 0.0)
        ms = _dot_split(o * o, ones_ref[...], 2) * (1.0 / HEAD_DIM)
        ob_ref[sl, :] = (o * lax.rsqrt(ms + EPS) * gn_ref[...] * _silu(gate_ref[sl, :])).astype(BF16)
        _run_share(units, c, nc)
    s_ref[...] = s
    s_out_ref[...] = s


def _gdn_scan(n4, vb, kbe, attn, qd, kd, egl, gate, gn_row, rider=None):
    t = vb.shape[0]
    tm = ROW_TILE
    nc = tm // CHUNK
    n_lt = WIDTH_B // LANES
    row = lambda w: pl.BlockSpec((tm, w), lambda i: (i, 0))
    return _call_with_rider(
        _gdn_scan_kernel, rider,
        grid=(t // tm,),
        in_specs=[pl.BlockSpec((CHUNK, n_lt, nc, LANES), lambda i: (0, 0, i, 0))] + [row(WIDTH_B)] * 5
        + [pl.BlockSpec((nc, WIDTH_B), lambda i: (i, 0)), row(WIDTH_B), _full((1, WIDTH_B)), _full((WIDTH_B, WIDTH_B))],
        out_specs=[row(WIDTH_B), _full((WIDTH_B, WIDTH_B))],
        out_shape=[jax.ShapeDtypeStruct((t, WIDTH_B), BF16), jax.ShapeDtypeStruct((WIDTH_B, WIDTH_B), F32)],
        scratch_shapes=[pltpu.VMEM((WIDTH_B, WIDTH_B), F32), pltpu.VMEM((CHUNK * n_lt * nc, LANES), F32)],
        args=(n4, vb, kbe, attn, qd, kd, egl, gate, gn_row, _group_ones(WIDTH_B)),
        sem="arbitrary", name="gdn_scan")


def _gdn_prompt(qkv, ab, gate, conv_w, alog_row, dtb_row, gn_row, riders=(None, None)):
    (m4, attn, qd, kd, vb, kbe, egl), ride0 = _gdn_prep(qkv, ab, conv_w, alog_row, dtb_row, riders[0])
    (ob, s_bd), ride1 = _gdn_scan(_tinv(m4), vb, kbe, attn, qd, kd, egl, gate, gn_row, riders[1])
    s_fin = jnp.stack([s_bd[h * HEAD_DIM:(h + 1) * HEAD_DIM, h * HEAD_DIM:(h + 1) * HEAD_DIM] for h in range(N_HEADS_B)])
    return ob, s_fin, (ride0, ride1)


def _ffn_kernel(shift, carry_rows, x_ref, oa_ref, ob_ref, oc_ref, wo_ref, g_ref, wup_ref, cw_ref, wdn_ref, c0_ref,
                y_ref, last_ref, carry_ref, ext_ref, hid_ref):
    tm = x_ref.shape[0]

    @pl.when(pl.program_id(0) == 0)
    def _():
        carry_ref[...] = c0_ref[...]

    mix = _dot(oa_ref[...], wo_ref[0:WIDTH_A, :])
    mix = mix + _dot(ob_ref[...], wo_ref[WIDTH_A:WIDTH_A + WIDTH_B, :])
    mix = mix + _dot(oc_ref[...], wo_ref[WIDTH_A + WIDTH_B:, :])
    x = x_ref[...] + mix
    ms = jnp.mean(x * x, axis=-1, keepdims=True)
    xn = (x * lax.rsqrt(ms + EPS) * g_ref[...]).astype(BF16)
    tf = FF_TILE
    base = carry_rows
    for j in range(D_FF // tf):
        halves = []
        for half in range(2):
            cols = slice(half * D_FF + j * tf, half * D_FF + (j + 1) * tf)
            up = _dot(xn, wup_ref[:, cols])
            ext_ref[0:base, :] = carry_ref[:, cols]
            ext_ref[base:base + tm, :] = up
            carry_ref[:, cols] = ext_ref[tm:tm + base, :]
            cw = cw_ref[:, cols]
            halves.append(cw[0:1] * ext_ref[base - 2 * shift:base - 2 * shift + tm, :]
                          + cw[1:2] * ext_ref[base - shift:base - shift + tm, :] + cw[2:3] * up)
        hid_ref[:, j * tf:(j + 1) * tf] = (_silu(halves[0]) * halves[1]).astype(BF16)
    y_ref[...] = x + _dot(hid_ref[...], wdn_ref[...])
    last_ref[...] = carry_ref[...]


def _ffn(x, oa, ob, oc, w_out, g, w_up, conv_w, w_down, carry0, shift, tm):
    rows = x.shape[0]
    carry_rows = carry0.shape[0]
    row = lambda w: pl.BlockSpec((tm, w), lambda i: (i, 0))
    once = lambda shape: pl.BlockSpec(shape, lambda i: (0,) * len(shape), pipeline_mode=pl.Buffered(1))
    return pl.pallas_call(
        functools.partial(_ffn_kernel, shift, carry_rows),
        grid=(rows // tm,),
        in_specs=[row(D_MODEL), row(WIDTH_A), row(WIDTH_B), row(WIDTH_C), once((D_MODEL, D_MODEL)),
                  once((1, D_MODEL)), once((D_MODEL, 2 * D_FF)), once((3, 2 * D_FF)),
                  once((D_FF, D_MODEL)), once((carry_rows, 2 * D_FF))],
        out_specs=[row(D_MODEL), _full((carry_rows, 2 * D_FF))],
        out_shape=[jax.ShapeDtypeStruct((rows, D_MODEL), F32), jax.ShapeDtypeStruct((carry_rows, 2 * D_FF), F32)],
        scratch_shapes=[pltpu.VMEM((carry_rows, 2 * D_FF), F32), pltpu.VMEM((tm + carry_rows, FF_TILE), F32),
                        pltpu.VMEM((tm, D_FF), BF16)],
        compiler_params=_params("arbitrary"),
        name=f"ffn_shift{shift}",
    )(x, oa, ob, oc, w_out, g, w_up, conv_w, w_down, carry0)


PAIR_ROWS = 16
NEW_ROWS = 16


def _sample_bias(wbuf, s_len):
    def table(key_pos):
        q_pos = wbuf + np.arange(s_len)
        dist = q_pos[:, None] - key_pos[None, :]
        mult = np.zeros(dist.shape)
        for window, dil in PATTERNS:
            mult += (dist >= 0) & (dist <= window) & (dist % dil == 0)
        logm = np.log(np.maximum(mult, 1))
        bias = -_alibi_slopes()[:, None, None] * np.maximum(dist, 0)[None] + logm[None]
        bias = np.where(mult[None] > 0, bias, NEG)
        out = np.zeros((N_HEADS_A, PAIR_ROWS // 2, key_pos.shape[0]))
        out[:, :s_len] = bias
        return out.reshape(N_HEADS_A // 2, PAIR_ROWS, -1)
    main = table(np.arange(wbuf))
    new = table(np.concatenate([wbuf + np.arange(s_len), np.full(NEW_ROWS - s_len, 10 ** 6)]))
    new[:, :, s_len:] = NEG
    return jnp.asarray(main, F32), jnp.asarray(new, F32)


def _sample_attn_units(q_ref, kn_ref, vn_ref, kt_ref, vt_ref, bm_ref, bn_ref, qc_ref, mkt_ref, mvt_ref, oa_ref, oc_ref):
    def times_values(p, vt):
        return _dot_nt(p.astype(BF16), vt.astype(BF16))

    def window(b):
        scores = []
        for pair in range(N_HEADS_A // 2):
            rows = slice(pair * LANES, (pair + 1) * LANES)
            q = q_ref[b, pair]
            scores.append((_dot(q, kt_ref[b, rows, :].astype(BF16)) + bm_ref[pair],
                           _dot_nt(q, kn_ref[b, :, rows].astype(BF16)) + bn_ref[pair]))
        for pair, (s_main, s_new) in enumerate(scores):
            rows = slice(pair * LANES, (pair + 1) * LANES)
            m = jnp.maximum(jnp.max(s_main, axis=-1, keepdims=True), jnp.max(s_new, axis=-1, keepdims=True))
            p_main = jnp.exp(s_main - m)
            p_new = jnp.exp(s_new - m)
            den = jnp.sum(p_main, axis=-1, keepdims=True) + jnp.sum(p_new, axis=-1, keepdims=True)
            num = times_values(p_main, vt_ref[b, rows, :])
            num = num + _dot(p_new.astype(BF16), vn_ref[b, :, rows].astype(BF16))
            oa_ref[b, pair] = num / den

    def memory(b):
        for pair in range(N_HEADS_C // 2):
            rows = slice(pair * LANES, (pair + 1) * LANES)
            q = qc_ref[b, pair]
            s = _dot(q, mkt_ref[b, rows, :].astype(BF16))
            p = jnp.exp(s - jnp.max(s, axis=-1, keepdims=True))
            oc_ref[b, pair] = times_values(p, mvt_ref[b, rows, :]) / jnp.sum(p, axis=-1, keepdims=True)

    units = []
    for b in range(q_ref.shape[0]):
        units += [functools.partial(window, b), functools.partial(memory, b)]
    return units


def _run_share(units, stage, n_stages):
    for u in units[stage * len(units) // n_stages:(stage + 1) * len(units) // n_stages]:
        u()


def _pair_rows(q, s_len):
    nb, _, nh, hd = q.shape
    q = q.reshape(nb, s_len, nh // 2, 2, hd).transpose(0, 2, 3, 1, 4)
    q = jnp.pad(q, ((0, 0), (0, 0), (0, 0), (0, PAIR_ROWS // 2 - s_len), (0, 0)))
    own = jnp.eye(2, dtype=q.dtype)[None, None, :, None, :, None]
    return (q[:, :, :, :, None, :] * own).reshape(nb, nh // 2, PAIR_ROWS, 2 * hd)


def _unpair_rows(o, s_len):
    nb, npair = o.shape[:2]
    o = o.reshape(nb, npair, 2, PAIR_ROWS // 2, 2, HEAD_DIM)
    o = jnp.stack([o[:, :, half, :s_len, half, :] for half in range(2)], axis=2)
    return o.transpose(0, 3, 1, 2, 4).reshape(nb, s_len, npair * 2 * HEAD_DIM)


def _sample_attn_riders(qa, ka, va, cache_k, cache_v, qc, mem_k, mem_v, n_parts, steps):
    nb, s_len = qa.shape[:2]
    wbuf = cache_k.shape[1]
    n_mem = mem_k.shape[1]
    assert s_len <= PAIR_ROWS // 2 and nb % (n_parts * steps) == 0
    g = nb // (n_parts * steps)
    bm, bn = _sample_bias(wbuf, s_len)
    transposed = lambda a: a.transpose(0, 2, 3, 1).reshape(nb, a.shape[2] * a.shape[3], a.shape[1])
    pad_new = lambda a: jnp.pad(a, ((0, 0), (0, NEW_ROWS - s_len), (0, 0)))
    args = [_pair_rows(qa, s_len), pad_new(ka), pad_new(va), transposed(cache_k), transposed(cache_v), bm, bn,
            _pair_rows(qc, s_len), transposed(mem_k), transposed(mem_v)]
    riders = []
    for part in range(n_parts):
        per_b = lambda shape, part=part: pl.BlockSpec((g,) + shape, lambda i: (i + part * steps,) + (0,) * len(shape))
        out_b = lambda shape: pl.BlockSpec((g,) + shape, lambda i: (i,) + (0,) * len(shape))
        riders.append(dict(
            units=_sample_attn_units,
            in_specs=[per_b((N_HEADS_A // 2, PAIR_ROWS, LANES)), per_b((NEW_ROWS, WIDTH_A)), per_b((NEW_ROWS, WIDTH_A)),
                      per_b((WIDTH_A, wbuf)), per_b((WIDTH_A, wbuf)), _full(bm.shape), _full(bn.shape),
                      per_b((N_HEADS_C // 2, PAIR_ROWS, LANES)), per_b((WIDTH_C, n_mem)), per_b((WIDTH_C, n_mem))],
            out_specs=[out_b((N_HEADS_A // 2, PAIR_ROWS, LANES)), out_b((N_HEADS_C // 2, PAIR_ROWS, LANES))],
            out_shape=[jax.ShapeDtypeStruct((g * steps, N_HEADS_A // 2, PAIR_ROWS, LANES), F32),
                       jax.ShapeDtypeStruct((g * steps, N_HEADS_C // 2, PAIR_ROWS, LANES), F32)],
            args=args))

    def finish(rider_outs):
        oa = jnp.concatenate([o[0] for o in rider_outs], axis=0)
        oc = jnp.concatenate([o[1] for o in rider_outs], axis=0)
        return _unpair_rows(oa, s_len).astype(BF16), _unpair_rows(oc, s_len).astype(BF16)

    return riders, finish


def _sample_gdn_kernel(xe_ref, cw_ref, a_ref, b_ref, alog_ref, dtb_ref, gate_ref, gn_ref, s_ref, o_ref, s_out_ref,
                       q_scr, k_scr, v_scr):
    s_len = o_ref.shape[0]
    s_out_ref[...] = s_ref[...]
    for t in range(s_len):
        for part, scr in enumerate((q_scr, k_scr, v_scr)):
            y = sum(cw_ref[j, part] * xe_ref[t + j, part] for j in range(4))
            y = _silu(y)
            if part < 2:
                y = y * lax.rsqrt(jnp.sum(y * y, axis=0, keepdims=True) + EPS)
            if part == 0:
                y = y * (HEAD_DIM ** -0.5)
            scr[...] = y
        decay = jnp.exp(-jnp.exp(alog_ref[...]) * _softplus(a_ref[t] + dtb_ref[...]))
        beta = _sigmoid(b_ref[t])

        def read_body(dk, r):
            return r + k_scr[pl.ds(dk, 1), :] * s_out_ref[dk]

        r = lax.fori_loop(0, HEAD_DIM, read_body, jnp.zeros(v_scr.shape, F32), unroll=8)
        v_new = beta * (v_scr[...] - decay * r)

        def write_body(dk, o):
            s_new = decay * s_out_ref[dk] + k_scr[pl.ds(dk, 1), :] * v_new
            s_out_ref[dk] = s_new
            return o + q_scr[pl.ds(dk, 1), :] * s_new

        o = lax.fori_loop(0, HEAD_DIM, write_body, jnp.zeros(v_scr.shape, F32), unroll=8)
        ms = jnp.mean(o * o, axis=0, keepdims=True)
        o_ref[t] = o * lax.rsqrt(ms + EPS) * gn_ref[...] * _silu(gate_ref[t])


def _sample_gdn(xe, conv_w, a_b, b_b, a_log, dt_bias, gate, out_norm, state):
    nb, ext, _ = xe.shape
    s_len = ext - 3
    nh, hd = N_HEADS_B, HEAD_DIM
    xe_t = xe.reshape(nb, ext, 3, nh, hd).transpose(1, 2, 3, 4, 0)
    cw_t = jnp.broadcast_to(conv_w.reshape(4, 3, nh, hd)[..., None], (4, 3, nh, hd, nb))
    a_t = a_b.transpose(1, 2, 0).reshape(s_len, nh, 1, nb)
    b_t = b_b.transpose(1, 2, 0).reshape(s_len, nh, 1, nb)
    alog_t = jnp.broadcast_to(a_log.reshape(nh, 1, 1), (nh, 1, nb))
    dtb_t = jnp.broadcast_to(dt_bias.reshape(nh, 1, 1), (nh, 1, nb))
    gate_t = gate.reshape(nb, s_len, nh, hd).transpose(1, 2, 3, 0)
    gn_t = jnp.broadcast_to(out_norm.reshape(1, hd, 1), (1, hd, nb))
    s_t = state.transpose(1, 2, 3, 0)
    o_t, s_new = pl.pallas_call(
        _sample_gdn_kernel,
        grid=(nh,),
        in_specs=[pl.BlockSpec((ext, 3, None, hd, nb), lambda h: (0, 0, h, 0, 0)),
                  pl.BlockSpec((4, 3, None, hd, nb), lambda h: (0, 0, h, 0, 0)),
                  pl.BlockSpec((s_len, None, 1, nb), lambda h: (0, h, 0, 0)),
                  pl.BlockSpec((s_len, None, 1, nb), lambda h: (0, h, 0, 0)),
                  pl.BlockSpec((None, 1, nb), lambda h: (h, 0, 0)),
                  pl.BlockSpec((None, 1, nb), lambda h: (h, 0, 0)),
                  pl.BlockSpec((s_len, None, hd, nb), lambda h: (0, h, 0, 0)),
                  pl.BlockSpec((None, hd, nb), lambda h: (0, 0, 0)),
                  pl.BlockSpec((None, hd, hd, nb), lambda h: (h, 0, 0, 0))],
        out_specs=[pl.BlockSpec((s_len, None, hd, nb), lambda h: (0, h, 0, 0)),
                   pl.BlockSpec((None, hd, hd, nb), lambda h: (h, 0, 0, 0))],
        out_shape=[jax.ShapeDtypeStruct((s_len, nh, hd, nb), F32), jax.ShapeDtypeStruct((nh, hd, hd, nb), F32)],
        scratch_shapes=[pltpu.VMEM((hd, nb), F32)] * 3,
        compiler_params=_params("parallel"),
        name="sample_gdn",
    )(xe_t, cw_t, a_t, b_t, alog_t, dtb_t, gate_t, gn_t, s_t)
    ob = o_t.transpose(3, 0, 1, 2).reshape(nb * s_len, nh * hd).astype(BF16)
    return ob, s_new.transpose(3, 0, 1, 2)


def _lane_row(v, width=LANES):
    return jnp.zeros((1, width), F32).at[0, :v.shape[0]].set(v)


def kernel(x_prompt, x_sample, cache_win_k, cache_win_v, state_gdn, state_gdn_conv, state_ffn_conv, cache_mem_k,
           cache_mem_v, mem_prompt, norm1_g, w_in, q_norm_a, k_norm_a, conv_b_w, a_log_b, dt_bias_b, out_norm_b,
           mem_norm_g, w_mem_kv, q_norm_c, k_norm_c, w_out, norm2_g, w_up, conv_ffn_w, w_down):
    depth = norm1_g.shape[0]
    assert depth == 1 and x_prompt.shape[0] == 1
    l = 0
    t_p = x_prompt.shape[1]
    nb, s_len = x_sample.shape[:2]
    xp = x_prompt.reshape(t_p, D_MODEL)
    xs = x_sample.reshape(nb * s_len, D_MODEL)

    w_in_p = w_in[l].T.astype(BF16)
    w_out_b = w_out[l].astype(BF16)
    w_up_b = w_up[l].astype(BF16)
    w_down_b = w_down[l].astype(BF16)
    g1 = norm1_g[l].reshape(1, D_MODEL)
    g2 = norm2_g[l].reshape(1, D_MODEL)
    gq = jnp.tile(q_norm_a[l], N_HEADS_A).reshape(1, WIDTH_A)
    gk = jnp.tile(k_norm_a[l], N_HEADS_A).reshape(1, WIDTH_A)
    gqc = jnp.tile(q_norm_c[l], N_HEADS_C).reshape(1, WIDTH_C)
    gkc = jnp.tile(k_norm_c[l], N_HEADS_C).reshape(1, WIDTH_C)
    gn_row = jnp.tile(out_norm_b[l], N_HEADS_B).reshape(1, WIDTH_B)

    rows_s = nb * s_len
    (qa_s,), _, _, (ka_s, va_s, qkv_s, gate_s, qc_s, ab_s) = _inproj(xs, g1, w_in_p, gq, gk, gqc, (1,), rows_s,
                                                                         HEAD_DIM ** -0.5)
    b3 = lambda a: a.reshape(nb, s_len, a.shape[-1])
    heads = lambda a: a.reshape(nb, s_len, -1, HEAD_DIM)
    riders, finish_sample_attn = _sample_attn_riders(
        heads(qa_s), b3(ka_s), b3(va_s), cache_win_k[l], cache_win_v[l], heads(qc_s), cache_mem_k[l], cache_mem_v[l],
        n_parts=2, steps=t_p // ROW_TILE)

    n_keep = min(MAX_WINDOW, t_p)
    q_ds, k_ds, v_ds, (ka, va, qkv, gate, qc, ab) = _inproj(xp, g1, w_in_p, gq, gk, gqc, DILATIONS, n_keep,
                                                            HEAD_DIM ** -0.5 * LOG2E)
    parts = [_attn_band(q_d, k_d, v_d, dil) for q_d, k_d, v_d, dil in zip(q_ds, k_ds, v_ds, DILATIONS)]
    mk, mv = _memkv(mem_prompt[0], mem_norm_g[l].reshape(1, D_MODEL), w_mem_kv[l].astype(BF16), gkc)
    oa, oc = _combine_mem([p[0] for p in parts], [p[1] for p in parts], DILATIONS, qc, mk, mv)
    ob, gdn_p, rider_outs = _gdn_prompt(qkv, ab, gate, conv_b_w[l], _lane_row(a_log_b[l]), _lane_row(dt_bias_b[l]),
                                        gn_row, riders)
    y_p, last_p = _ffn(xp, oa, ob, oc, w_out_b, g2, w_up_b, conv_ffn_w[l], w_down_b, jnp.zeros((8, 2 * D_FF), F32), 1,
                       FFN_ROW_TILE)
    win_k_p = ka.reshape(1, 1, n_keep, N_HEADS_A, HEAD_DIM)
    win_v_p = va.reshape(1, 1, n_keep, N_HEADS_A, HEAD_DIM)
    gconv_p = qkv[t_p - 3:].reshape(1, 1, 3, QKV_B)
    fconv_p = last_p[6:8].reshape(1, 1, 2, 2 * D_FF)

    oa_s, oc_s = finish_sample_attn(rider_outs)
    xe_s = jnp.concatenate([state_gdn_conv[l], b3(qkv_s)], axis=1)
    ab3 = b3(ab_s)
    ob_s, gdn_s = _sample_gdn(xe_s, conv_b_w[l], ab3[..., 0:N_HEADS_B], ab3[..., N_HEADS_B:2 * N_HEADS_B], a_log_b[l],
                              dt_bias_b[l], b3(gate_s), out_norm_b[l], state_gdn[l])
    time_major = lambda a: a.reshape(nb, s_len, a.shape[-1]).transpose(1, 0, 2).reshape(rows_s, a.shape[-1])
    carry_s = state_ffn_conv[l].transpose(1, 0, 2).reshape(2 * nb, 2 * D_FF)
    y_t, last_s = _ffn(time_major(xs), time_major(oa_s), time_major(ob_s), time_major(oc_s), w_out_b, g2, w_up_b,
                       conv_ffn_w[l], w_down_b, carry_s, nb, rows_s)
    y_s = y_t.reshape(s_len, nb, D_MODEL).transpose(1, 0, 2)
    fconv_s = last_s.reshape(2, nb, 2 * D_FF).transpose(1, 0, 2)[None]

    return (y_p.reshape(1, t_p, D_MODEL), y_s,
            win_k_p, win_v_p,
            ka_s.reshape(1, nb, s_len, N_HEADS_A, HEAD_DIM), va_s.reshape(1, nb, s_len, N_HEADS_A, HEAD_DIM),
            gdn_p[None, None], gdn_s[None],
            gconv_p, xe_s[:, -3:][None],
            fconv_p, fconv_s,
            mk.reshape(1, 1, N_MEM, N_HEADS_C, HEAD_DIM), mv.reshape(1, 1, N_MEM, N_HEADS_C, HEAD_DIM))
```

```python
import functools

import numpy as np
import jax
import jax.numpy as jnp
from jax import lax
from jax.experimental import pallas as pl
from jax.experimental.pallas import tpu as pltpu

F32 = jnp.float32
BF16 = jnp.bfloat16
EPS = 1e-6
NEG = -1e30
LOG2E = 1.4426950408889634
SQ_PARTS = 1

D_MODEL = 1024
HEAD_DIM = 64
N_HEADS_A = 8
PATTERNS = ((128, 1), (512, 4), (2048, 16))
DILATIONS = tuple(d for _, d in PATTERNS)
MAX_WINDOW = 2048
N_HEADS_B = 4
N_HEADS_C = 4
N_MEM = 256
D_FF = 2816
WIDTH_A = N_HEADS_A * HEAD_DIM
WIDTH_B = N_HEADS_B * HEAD_DIM
WIDTH_C = N_HEADS_C * HEAD_DIM
QKV_B = 3 * WIDTH_B
CHUNK = 64
NBAND = 128
BAND_BLOCKS = 16
COMBINE_ROW_TILE = 1024
ROW_TILE = 512
FFN_ROW_TILE = 512
FF_TILE = 256
LANES = 128
VMEM_LIMIT = 56 * 1024 * 1024

SEG_QA, SEG_KA, SEG_VA = (0, 512), (512, 1024), (1024, 1536)
SEG_QKV, SEG_GATE, SEG_TAIL = (1536, 2304), (2304, 2560), (2560, 2824)
N_IN = 2824


def _params(*sem):
    return pltpu.CompilerParams(dimension_semantics=sem, vmem_limit_bytes=VMEM_LIMIT)


def _dot(a, b):
    return jnp.dot(a, b, preferred_element_type=F32)


def _dot_nt(a, b):
    return lax.dot_general(a, b, (((1,), (1,)), ((), ())), preferred_element_type=F32)


def _dot_tn(a, b):
    return lax.dot_general(a, b, (((0,), (0,)), ((), ())), preferred_element_type=F32)


def _dot_split(a, b, parts, left=False):
    acc = None
    rem = a
    for _ in range(parts):
        piece = rem.astype(BF16)
        term = _dot(b, piece) if left else _dot(piece, b)
        acc = term if acc is None else acc + term
        rem = rem - piece.astype(F32)
    return acc


def _sigmoid(x):
    return 1.0 / (1.0 + jnp.exp(-x))


def _silu(x):
    return x * _sigmoid(x)


def _softplus(x):
    return jnp.maximum(x, 0.0) + jnp.log1p(jnp.exp(-jnp.abs(x)))


def _group_ones(width, group=HEAD_DIM):
    i = np.arange(width)
    return jnp.asarray((i[:, None] // group) == (i[None, :] // group), BF16)


def _full(shape):
    nd = len(shape)
    return pl.BlockSpec(shape, lambda *_: (0,) * nd)


def _call_with_rider(host_kernel, rider, *, grid, in_specs, out_specs, out_shape, scratch_shapes, args, sem, name):
    if rider is None:
        rider = dict(units=None, in_specs=[], out_specs=[], out_shape=[], args=[])
    n_hi, n_ri, n_ho, n_ro = len(in_specs), len(rider["in_specs"]), len(out_specs), len(rider["out_specs"])

    def body(*refs):
        host_in, refs = refs[:n_hi], refs[n_hi:]
        rider_in, refs = refs[:n_ri], refs[n_ri:]
        host_out, refs = refs[:n_ho], refs[n_ho:]
        rider_out, scratch = refs[:n_ro], refs[n_ro:]
        units = rider["units"](*rider_in, *rider_out) if rider["units"] is not None else []
        host_kernel(units, *host_in, *host_out, *scratch)

    outs = pl.pallas_call(
        body, grid=grid, in_specs=list(in_specs) + rider["in_specs"], out_specs=list(out_specs) + rider["out_specs"],
        out_shape=list(out_shape) + rider["out_shape"], scratch_shapes=scratch_shapes,
        compiler_params=_params(sem), name=name,
    )(*args, *rider["args"])
    return outs[:n_ho], outs[n_ho:]


def _inproj_kernel(dils, scale, x_ref, g1_ref, w_ref, gq_ref, gk_ref, gc_ref, ga_ref, gcc_ref, *refs):
    nd = len(dils)
    q_refs, k_refs, v_refs = refs[0:nd], refs[nd:2 * nd], refs[2 * nd:3 * nd]
    ka_ref, va_ref, qkv_ref, gate_ref, qc_ref, ab_ref, scr_ref = refs[3 * nd:]
    tm = x_ref.shape[0]
    x = x_ref[...]
    ms = jnp.mean(x * x, axis=-1, keepdims=True)
    xn = (x * lax.rsqrt(ms + EPS) * g1_ref[...]).astype(BF16)

    def seg(s):
        return _dot_nt(xn, w_ref[s[0]:s[1], :])

    def head_norm(z, ones_ref, gain):
        ss = _dot_split(z * z, ones_ref[...], SQ_PARTS) * (1.0 / HEAD_DIM)
        return z * lax.rsqrt(ss + EPS) * gain

    def emit(z, out_refs):
        n_tiles = WIDTH_A // LANES
        for c in range(n_tiles):
            scr_ref[c] = z[:, c * LANES:(c + 1) * LANES]
        for d, ref in zip(dils, out_refs):
            if d == 1:
                ref[...] = z.astype(BF16)
            else:
                for r in range(d):
                    for c in range(n_tiles):
                        col = r * WIDTH_A + c * LANES
                        ref[:, col:col + LANES] = scr_ref[c, pl.ds(r, tm // d, stride=d), :].astype(BF16)

    emit(head_norm(seg(SEG_QA), ga_ref, gq_ref[...]) * scale, q_refs)
    ka = head_norm(seg(SEG_KA), ga_ref, gk_ref[...])
    ka_ref[...] = ka
    emit(ka, k_refs)
    va = seg(SEG_VA)
    va_ref[...] = va
    emit(va, v_refs)
    qkv_ref[...] = seg(SEG_QKV)
    gate_ref[...] = seg(SEG_GATE)
    tail = seg(SEG_TAIL)
    ab_ref[...] = tail[:, 0:LANES]
    qc = tail[:, 2 * N_HEADS_B:2 * N_HEADS_B + WIDTH_C]
    qc_ref[...] = (head_norm(qc, gcc_ref, gc_ref[...]) * scale).astype(BF16)


def _inproj(x, g1, w_in_p, gq, gk, gc, dils, keep_rows, scale):
    rows = x.shape[0]
    tm = ROW_TILE
    nt = rows // tm
    skip = nt - keep_rows // tm
    row = lambda w: pl.BlockSpec((tm, w), lambda i: (i, 0))
    tail = pl.BlockSpec((tm, WIDTH_A), lambda i: (jnp.maximum(i - skip, 0), 0))
    dil_specs = [pl.BlockSpec((tm // d, d * WIDTH_A), lambda i: (i, 0)) for d in dils]
    dil_shapes = [jax.ShapeDtypeStruct((rows // d, d * WIDTH_A), BF16) for d in dils]
    outs = [(QKV_B, F32), (WIDTH_B, F32), (WIDTH_C, BF16), (LANES, F32)]
    res = pl.pallas_call(
        functools.partial(_inproj_kernel, dils, scale),
        grid=(nt,),
        in_specs=[row(D_MODEL), _full((1, D_MODEL)), _full((N_IN, D_MODEL)), _full((1, WIDTH_A)),
                  _full((1, WIDTH_A)), _full((1, WIDTH_C)), _full((WIDTH_A, WIDTH_A)), _full((WIDTH_C, WIDTH_C))],
        out_specs=dil_specs * 3 + [tail, tail] + [row(w) for w, _ in outs],
        out_shape=dil_shapes * 3 + [jax.ShapeDtypeStruct((keep_rows, WIDTH_A), F32)] * 2
        + [jax.ShapeDtypeStruct((rows, w), dt) for w, dt in outs],
        scratch_shapes=[pltpu.VMEM((WIDTH_A // LANES, tm, LANES), F32)],
        compiler_params=_params("arbitrary"),
        name="inproj",
    )(x, g1, w_in_p, gq, gk, gc, _group_ones(WIDTH_A), _group_ones(WIDTH_C))
    nd = len(dils)
    return res[0:nd], res[nd:2 * nd], res[2 * nd:3 * nd], res[3 * nd:]


def _alibi_slopes():
    return np.exp2(-8.0 * np.arange(1, N_HEADS_A + 1, dtype=np.float64) / N_HEADS_A)


def _band_bias(dil):
    qi = np.arange(NBAND)[:, None]
    kj = np.arange(2 * NBAND)[None, :]
    delta = qi + NBAND - kj
    in_band = (delta >= 0) & (delta <= NBAND)
    bias = -_alibi_slopes()[:, None, None] * (delta * dil)[None].astype(np.float64)
    general = np.where(in_band[None], bias, NEG)
    first = np.where((in_band & (kj >= NBAND))[None], bias, NEG)
    return jnp.asarray(np.stack([general, first]) * LOG2E, F32)


def _attn_band_kernel(q_ref, kp_ref, kc_ref, vp_ref, vc_ref, bias_ref, o_ref, lse_ref):
    lane = lax.broadcasted_iota(jnp.int32, (NBAND, LANES), 1)
    low = lane < HEAD_DIM
    ones = jnp.ones((2 * NBAND, LANES), BF16)
    for blk in range(q_ref.shape[0] // NBAND):
        first = (pl.program_id(1) == 0).astype(jnp.int32) if blk == 0 else 0
        rows = slice(blk * NBAND, (blk + 1) * NBAND)
        lse = jnp.zeros((NBAND, LANES), F32)
        for pair in range(N_HEADS_A // 2):
            cols = slice(pair * LANES, (pair + 1) * LANES)
            q = q_ref[rows, cols]
            if blk == 0:
                k = jnp.concatenate([kp_ref[:, cols], kc_ref[rows, cols]], axis=0)
                v = jnp.concatenate([vp_ref[:, cols], vc_ref[rows, cols]], axis=0)
            else:
                k = kc_ref[(blk - 1) * NBAND:(blk + 1) * NBAND, cols]
                v = vc_ref[(blk - 1) * NBAND:(blk + 1) * NBAND, cols]
            v_ext = jnp.concatenate([v, ones], axis=1)
            outs = []
            for half in range(2):
                h = 2 * pair + half
                qm = jnp.where(low if half == 0 else ~low, q, jnp.zeros((), BF16))
                s = _dot_nt(qm, k) + bias_ref[first, h]
                m = jnp.max(s, axis=-1, keepdims=True)
                p = jnp.exp2(s - m).astype(BF16)
                r = _dot(p, v_ext)
                outs.append(r[:, :LANES])
                lse = jnp.where(lane == h, m, lse)
                lse = jnp.where(lane == N_HEADS_A + h, r[:, LANES:], lse)
            o_ref[rows, cols] = jnp.where(low, outs[0], outs[1]).astype(BF16)
        lse_ref[rows, :] = lse


def _attn_band(q_d, k_d, v_d, dil):
    rows = q_d.shape[0]
    blocks = min(BAND_BLOCKS, rows // NBAND)
    step = blocks * NBAND
    cur = lambda w: pl.BlockSpec((step, w), lambda r, n: (n, r))
    prev = lambda w: pl.BlockSpec((NBAND, w), lambda r, n: (jnp.maximum(n * blocks - 1, 0), r))
    return pl.pallas_call(
        _attn_band_kernel,
        grid=(dil, rows // step),
        in_specs=[cur(WIDTH_A), prev(WIDTH_A), cur(WIDTH_A), prev(WIDTH_A), cur(WIDTH_A),
                  _full((2, N_HEADS_A, NBAND, 2 * NBAND))],
        out_specs=[cur(WIDTH_A), cur(LANES)],
        out_shape=[jax.ShapeDtypeStruct((rows, dil * WIDTH_A), BF16), jax.ShapeDtypeStruct((rows, dil * LANES), F32)],
        compiler_params=_params("parallel", "arbitrary"),
        name=f"attn_band_d{dil}",
    )(q_d, k_d, k_d, v_d, v_d, _band_bias(dil))


def _memkv_kernel(mem_ref, g_ref, w_ref, gk_ref, ones_ref, mk_ref, mv_ref):
    x = mem_ref[...]
    ms = jnp.mean(x * x, axis=-1, keepdims=True)
    xn = (x * lax.rsqrt(ms + EPS) * g_ref[...]).astype(BF16)
    zk = _dot(xn, w_ref[:, 0:WIDTH_C])
    ss = _dot_split(zk * zk, ones_ref[...], SQ_PARTS) * (1.0 / HEAD_DIM)
    mk_ref[...] = zk * lax.rsqrt(ss + EPS) * gk_ref[...]
    mv_ref[...] = _dot(xn, w_ref[:, WIDTH_C:2 * WIDTH_C])


def _memkv(mem, g, w, gk):
    n = mem.shape[0]
    return pl.pallas_call(
        _memkv_kernel,
        out_shape=[jax.ShapeDtypeStruct((n, WIDTH_C), F32)] * 2,
        compiler_params=pltpu.CompilerParams(vmem_limit_bytes=VMEM_LIMIT),
        name="memkv",
    )(mem, g, w, gk, _group_ones(WIDTH_C))


def _head_spread():
    e = np.zeros((LANES, WIDTH_A), np.float32)
    for h in range(N_HEADS_A):
        e[h, h * HEAD_DIM:(h + 1) * HEAD_DIM] = 1.0
    return jnp.asarray(e, BF16)


def _combine_mem_kernel(dils, *refs):
    nd = len(dils)
    o_refs, l_refs = refs[0:nd], refs[nd:2 * nd]
    qc_ref, mk_ref, mv_ref, e_ref, oa_ref, oc_ref, o_scr, l_scr = refs[2 * nd:]
    tm = oa_ref.shape[0]
    os_, ls = [], []
    for p, d in enumerate(dils):
        if d == 1:
            os_.append(o_refs[p][...].astype(F32))
            ls.append(l_refs[p][...])
            continue
        n_tiles = WIDTH_A // LANES
        for r in range(d):
            for c in range(n_tiles):
                col = r * WIDTH_A + c * LANES
                o_scr[p * n_tiles + c, pl.ds(r, tm // d, stride=d), :] = o_refs[p][:, col:col + LANES].astype(F32)
            l_scr[p, pl.ds(r, tm // d, stride=d), :] = l_refs[p][:, r * LANES:(r + 1) * LANES]
        os_.append(jnp.concatenate([o_scr[p * n_tiles + c] for c in range(n_tiles)], axis=1))
        ls.append(l_scr[p])
    head_lane = lax.broadcasted_iota(jnp.int32, (tm, LANES), 1) < N_HEADS_A
    m = functools.reduce(jnp.maximum, ls)
    es = [jnp.exp2(l - m) for l in ls]
    dens = [pltpu.roll(l, LANES - N_HEADS_A, 1) for l in ls]
    tot = functools.reduce(lambda a, b: a + b, [d * e for d, e in zip(dens, es)])
    tot = jnp.where(head_lane, tot, 1.0)
    acc = None
    for o, e in zip(os_, es):
        term = o * _dot_split(e / tot, e_ref[...], 2)
        acc = term if acc is None else acc + term
    oa_ref[...] = acc.astype(BF16)

    lane = lax.broadcasted_iota(jnp.int32, (tm, LANES), 1)
    low = lane < HEAD_DIM
    ones = jnp.ones((N_MEM, LANES), BF16)
    for pair in range(N_HEADS_C // 2):
        cols = slice(pair * LANES, (pair + 1) * LANES)
        q = qc_ref[:, cols]
        mk = mk_ref[:, cols].astype(BF16)
        v_ext = jnp.concatenate([mv_ref[:, cols].astype(BF16), ones], axis=1)
        outs = []
        for half in range(2):
            qm = jnp.where(low if half == 0 else ~low, q, jnp.zeros((), BF16))
            s = _dot_nt(qm, mk)
            p = jnp.exp2(s - jnp.max(s, axis=-1, keepdims=True)).astype(BF16)
            r = _dot(p, v_ext)
            outs.append(r[:, :LANES] / r[:, LANES:])
        oc_ref[:, cols] = jnp.where(low, outs[0], outs[1]).astype(BF16)


def _combine_mem(os_, lses, dils, qc, mk, mv):
    t = qc.shape[0]
    tm = COMBINE_ROW_TILE
    nd = len(dils)
    row = lambda w: pl.BlockSpec((tm, w), lambda i: (i, 0))
    o_specs = [pl.BlockSpec((tm // d, d * WIDTH_A), lambda i: (i, 0)) for d in dils]
    l_specs = [pl.BlockSpec((tm // d, d * LANES), lambda i: (i, 0)) for d in dils]
    return pl.pallas_call(
        functools.partial(_combine_mem_kernel, dils),
        grid=(t // tm,),
        in_specs=o_specs + l_specs + [row(WIDTH_C), _full((N_MEM, WIDTH_C)), _full((N_MEM, WIDTH_C)),
                                     _full((LANES, WIDTH_A))],
        out_specs=[row(WIDTH_A), row(WIDTH_C)],
        out_shape=[jax.ShapeDtypeStruct((t, WIDTH_A), BF16), jax.ShapeDtypeStruct((t, WIDTH_C), BF16)],
        scratch_shapes=[pltpu.VMEM((nd * WIDTH_A // LANES, tm, LANES), F32), pltpu.VMEM((nd, tm, LANES), F32)],
        compiler_params=_params("parallel"),
        name="combine_mem",
    )(*os_, *lses, qc, mk, mv, _head_spread())


def _head_lane_consts():
    e_g = np.zeros((LANES, WIDTH_B), np.float32)
    e_b = np.zeros((LANES, WIDTH_B), np.float32)
    for h in range(N_HEADS_B):
        e_g[h, h * HEAD_DIM:(h + 1) * HEAD_DIM] = 1.0
        e_b[N_HEADS_B + h, h * HEAD_DIM:(h + 1) * HEAD_DIM] = 1.0
    i = np.arange(CHUNK)
    ltri = (i[None, :] <= i[:, None]).astype(np.float32)
    return jnp.asarray(e_g, BF16), jnp.asarray(e_b, BF16), jnp.asarray(ltri, BF16), jnp.ones((CHUNK, CHUNK), BF16)


def _block_diag(x, mask):
    return jnp.where(mask, jnp.concatenate([x] * N_HEADS_B, axis=0), jnp.zeros((), x.dtype))


def _gdn_prep_kernel(units, x_ref, halo_ref, ab_ref, cw_ref, alog_ref, dtb_ref, ones_ref, eg_ref, eb_ref, ltri_ref, one64_ref,
                     m_ref, attn_ref, qd_ref, kd_ref, vb_ref, kbe_ref, egl_ref, xe_ref, m_scr):
    tm = x_ref.shape[0]
    nc = tm // CHUNK
    halo = jnp.where(pl.program_id(0) > 0, halo_ref[...], 0.0)
    xe_ref[0:8, :] = halo
    xe_ref[8:8 + tm, :] = x_ref[...]
    cw = cw_ref[...]
    y = cw[3:4] * xe_ref[8:8 + tm, :] + cw[2:3] * xe_ref[7:7 + tm, :] + cw[1:2] * xe_ref[6:6 + tm, :] \
        + cw[0:1] * xe_ref[5:5 + tm, :]
    y = _silu(y)

    def l2n(z):
        return z * lax.rsqrt(_dot_split(z * z, ones_ref[...], SQ_PARTS) + EPS)

    q = l2n(y[:, 0:WIDTH_B]) * (HEAD_DIM ** -0.5)
    k = l2n(y[:, WIDTH_B:2 * WIDTH_B])
    v = y[:, 2 * WIDTH_B:3 * WIDTH_B]
    ab = ab_ref[...]
    g = -jnp.exp(alog_ref[...]) * _softplus(ab + dtb_ref[...])
    gx = _dot_split(g, eg_ref[...], 2)
    bx = _dot_split(_sigmoid(ab), eb_ref[...], 2)

    row = lax.broadcasted_iota(jnp.int32, (CHUNK, WIDTH_B), 0)
    col = lax.broadcasted_iota(jnp.int32, (CHUNK, WIDTH_B), 1) % CHUNK
    diag = row == col
    r4 = lax.broadcasted_iota(jnp.int32, (WIDTH_B, WIDTH_B), 0) // HEAD_DIM
    c4 = lax.broadcasted_iota(jnp.int32, (WIDTH_B, WIDTH_B), 1) // HEAD_DIM
    bd_mask = r4 == c4
    for c in range(tm // CHUNK):
        sl = slice(c * CHUNK, (c + 1) * CHUNK)
        gc = _dot_split(gx[sl], ltri_ref[...], 2, left=True)
        gl = gc[CHUNK - 1:CHUNK, :]
        gcj = _dot_split(jnp.where(diag, gc, 0.0), one64_ref[...], 2, left=True)
        dec = jnp.exp(jnp.where(row >= col, gc - gcj, NEG))
        kc, qc, vc, bc = k[sl], q[sl], v[sl], bx[sl]
        kb = kc * bc
        lhs = jnp.concatenate([kb, qc], axis=0).astype(BF16)
        kkqk = _dot_nt(lhs, _block_diag(kc.astype(BF16), bd_mask))
        m_c = jnp.where(row > col, kkqk[0:CHUNK] * dec, 0.0)
        for lt in range(WIDTH_B // LANES):
            m_scr[pl.ds(lt * nc + c, CHUNK, stride=nc * WIDTH_B // LANES), :] = m_c[:, lt * LANES:(lt + 1) * LANES]
        attn_ref[sl, :] = (kkqk[CHUNK:2 * CHUNK] * dec).astype(BF16)
        egc = jnp.exp(gc)
        qd_ref[sl, :] = (qc * egc).astype(BF16)
        kd_ref[sl, :] = (kc * jnp.exp(gl - gc)).astype(BF16)
        vb_ref[sl, :] = (vc * bc).astype(BF16)
        kbe_ref[sl, :] = (kb * egc).astype(BF16)
        egl_ref[c:c + 1, :] = jnp.exp(gl)
        _run_share(units, c, nc)
    m_ref[...] = m_scr[...].reshape(m_ref.shape)


def _gdn_prep(qkv, ab, conv_w, alog_row, dtb_row, rider=None):
    t = qkv.shape[0]
    tm = ROW_TILE
    nc = tm // CHUNK
    assert nc == 8
    n_lt = WIDTH_B // LANES
    row = lambda w: pl.BlockSpec((tm, w), lambda i: (i, 0))
    m_spec = pl.BlockSpec((CHUNK, n_lt, nc, LANES), lambda i: (0, 0, i, 0))
    e_g, e_b, ltri, one64 = _head_lane_consts()
    outs = [BF16, BF16, BF16, BF16, BF16]
    return _call_with_rider(
        _gdn_prep_kernel, rider,
        grid=(t // tm,),
        in_specs=[row(QKV_B), pl.BlockSpec((8, QKV_B), lambda i: (jnp.maximum(i * (tm // 8) - 1, 0), 0)), row(LANES),
                  _full((4, QKV_B)), _full((1, LANES)), _full((1, LANES)), _full((WIDTH_B, WIDTH_B)),
                  _full((LANES, WIDTH_B)), _full((LANES, WIDTH_B)), _full((CHUNK, CHUNK)), _full((CHUNK, CHUNK))],
        out_specs=[m_spec] + [row(WIDTH_B)] * 5 + [pl.BlockSpec((nc, WIDTH_B), lambda i: (i, 0))],
        out_shape=[jax.ShapeDtypeStruct((CHUNK, n_lt, t // CHUNK, LANES), F32)]
        + [jax.ShapeDtypeStruct((t, WIDTH_B), dt) for dt in outs]
        + [jax.ShapeDtypeStruct((t // CHUNK, WIDTH_B), F32)],
        scratch_shapes=[pltpu.VMEM((tm + 8, QKV_B), F32), pltpu.VMEM((CHUNK * n_lt * nc, LANES), F32)],
        args=(qkv, qkv, ab, conv_w, alog_row, dtb_row, _group_ones(WIDTH_B), e_g, e_b, ltri, one64),
        sem="parallel", name="gdn_prep")


def _tinv_kernel(m_ref, out_ref, n_ref, mrow_ref, nrow_ref):
    nslots = n_ref.shape[2]
    heads_per_tile = LANES // CHUNK
    blk = pl.program_id(0)

    @pl.when(blk == 0)
    def _():
        n_ref[...] = jnp.zeros_like(n_ref)

    for ii in range(8):
        i = blk * 8 + ii
        for g in range(nslots // N_HEADS_B):
            for lt in range(WIDTH_B // LANES):
                xt = m_ref[ii, lt, g * LANES:(g + 1) * LANES, :].T
                for h2 in range(heads_per_tile):
                    slot = g * N_HEADS_B + lt * heads_per_tile + h2
                    mrow_ref[ii, pl.ds(slot, CHUNK, stride=nslots), :] = xt[h2 * CHUNK:(h2 + 1) * CHUNK, :]
        for kb in range(8):
            k0 = 8 * kb
            init = tuple(mrow_ref[ii, (k0 + kk) * nslots:(k0 + kk + 1) * nslots, :] for kk in range(8))

            def m_at(j, ii=ii):
                return mrow_ref[ii, pl.ds(pl.multiple_of(j * nslots, nslots), nslots), :]

            def body(jj, acc, k0=k0):
                j = k0 + 2 * jj
                m0, m1 = m_at(j), m_at(j + 1)
                return tuple(acc[kk] + m0 * n_ref[j, k0 + kk] + m1 * n_ref[j + 1, k0 + kk] for kk in range(8))

            n_cols = jnp.maximum(i - k0, 0)
            acc = lax.fori_loop(0, n_cols // 2, body, init)
            odd = n_cols % 2 == 1
            j_last = jnp.where(odd, i - 1, 0)
            m_last = jnp.where(odd, m_at(j_last), 0.0)
            acc = tuple(acc[kk] + m_last * n_ref[j_last, k0 + kk] for kk in range(8))
            for kk in range(8):
                val = jnp.where(k0 + kk < i, -acc[kk], 0.0)
                n_ref[i, k0 + kk] = val
                nrow_ref[(k0 + kk) * nslots:(k0 + kk + 1) * nslots, :] = val
        for g in range(nslots // N_HEADS_B):
            for lt in range(WIDTH_B // LANES):
                slot0 = g * N_HEADS_B + lt * heads_per_tile
                parts = [nrow_ref[pl.ds(slot0 + h2, CHUNK, stride=nslots), :] for h2 in range(heads_per_tile)]
                out_ref[ii, lt, g * LANES:(g + 1) * LANES, :] = jnp.concatenate(parts, axis=0).T


def _tinv(m4):
    _, n_lt, nchunk, _ = m4.shape
    assert nchunk % LANES == 0
    nslots = (nchunk // LANES) * N_HEADS_B
    blk = pl.BlockSpec((8, n_lt, nchunk, LANES), lambda i: (i, 0, 0, 0))
    return pl.pallas_call(
        _tinv_kernel,
        grid=(CHUNK // 8,),
        in_specs=[blk],
        out_specs=blk,
        out_shape=jax.ShapeDtypeStruct(m4.shape, F32),
        scratch_shapes=[pltpu.VMEM((CHUNK, CHUNK, nslots, LANES), F32), pltpu.VMEM((8, CHUNK * nslots, LANES), F32),
                        pltpu.VMEM((CHUNK * nslots, LANES), F32)],
        compiler_params=_params("arbitrary"),
        name="gdn_tinv",
    )(m4)


def _gdn_scan_kernel(units, n_ref, vb_ref, kbe_ref, attn_ref, qd_ref, kd_ref, egl_ref, gate_ref, gn_ref, ones_ref,
                     ob_ref, s_out_ref, s_ref, n_scr):
    tm = vb_ref.shape[0]
    nc = tm // CHUNK
    n_lt = WIDTH_B // LANES

    @pl.when(pl.program_id(0) == 0)
    def _():
        s_ref[...] = jnp.zeros_like(s_ref)

    n_scr[...] = n_ref[...].reshape(n_scr.shape)

    row = lax.broadcasted_iota(jnp.int32, (CHUNK, WIDTH_B), 0)
    col = lax.broadcasted_iota(jnp.int32, (CHUNK, WIDTH_B), 1) % CHUNK
    eye = (row == col).astype(F32)
    r4 = lax.broadcasted_iota(jnp.int32, (WIDTH_B, WIDTH_B), 0) // HEAD_DIM
    c4 = lax.broadcasted_iota(jnp.int32, (WIDTH_B, WIDTH_B), 1) // HEAD_DIM
    bd_mask = r4 == c4
    s = s_ref[...]
    for c in range(tm // CHUNK):
        sl = slice(c * CHUNK, (c + 1) * CHUNK)
        n_c = jnp.concatenate([n_scr[pl.ds(lt * nc + c, CHUNK, stride=nc * n_lt), :] for lt in range(n_lt)], axis=1)
        tinv = (n_c + eye).astype(BF16)
        u = _dot(tinv, _block_diag(vb_ref[sl, :], bd_mask))
        w = _dot(tinv, _block_diag(kbe_ref[sl, :], bd_mask))
        lhs = jnp.concatenate([w.astype(BF16), qd_ref[sl, :]], axis=0)
        ws = _dot(lhs, s.astype(BF16))
        v_new = (u - ws[0:CHUNK]).astype(BF16)
        o = ws[CHUNK:2 * CHUNK] + _dot(attn_ref[sl, :], _block_diag(v_new, bd_mask))
        s = s * egl_ref[c:c + 1, :] + jnp.where(bd_mask, _dot_tn(kd_ref[sl, :], v_new), 0.0)
        ms = _dot_split(o * o, ones_ref[...], SQ_PARTS) * (1.0 / HEAD_DIM)
        ob_ref[sl, :] = (o * lax.rsqrt(ms + EPS) * gn_ref[...] * _silu(gate_ref[sl, :])).astype(BF16)
        _run_share(units, c, nc)
    s_ref[...] = s
    s_out_ref[...] = s


def _gdn_scan(n4, vb, kbe, attn, qd, kd, egl, gate, gn_row, rider=None):
    t = vb.shape[0]
    tm = ROW_TILE
    nc = tm // CHUNK
    n_lt = WIDTH_B // LANES
    row = lambda w: pl.BlockSpec((tm, w), lambda i: (i, 0))
    return _call_with_rider(
        _gdn_scan_kernel, rider,
        grid=(t // tm,),
        in_specs=[pl.BlockSpec((CHUNK, n_lt, nc, LANES), lambda i: (0, 0, i, 0))] + [row(WIDTH_B)] * 5
        + [pl.BlockSpec((nc, WIDTH_B), lambda i: (i, 0)), row(WIDTH_B), _full((1, WIDTH_B)), _full((WIDTH_B, WIDTH_B))],
        out_specs=[row(WIDTH_B), _full((WIDTH_B, WIDTH_B))],
        out_shape=[jax.ShapeDtypeStruct((t, WIDTH_B), BF16), jax.ShapeDtypeStruct((WIDTH_B, WIDTH_B), F32)],
        scratch_shapes=[pltpu.VMEM((WIDTH_B, WIDTH_B), F32), pltpu.VMEM((CHUNK * n_lt * nc, LANES), F32)],
        args=(n4, vb, kbe, attn, qd, kd, egl, gate, gn_row, _group_ones(WIDTH_B)),
        sem="arbitrary", name="gdn_scan")


def _gdn_prompt(qkv, ab, gate, conv_w, alog_row, dtb_row, gn_row, riders=(None, None)):
    (m4, attn, qd, kd, vb, kbe, egl), ride0 = _gdn_prep(qkv, ab, conv_w, alog_row, dtb_row, riders[0])
    (ob, s_bd), ride1 = _gdn_scan(_tinv(m4), vb, kbe, attn, qd, kd, egl, gate, gn_row, riders[1])
    s_fin = jnp.stack([s_bd[h * HEAD_DIM:(h + 1) * HEAD_DIM, h * HEAD_DIM:(h + 1) * HEAD_DIM] for h in range(N_HEADS_B)])
    return ob, s_fin, (ride0, ride1)


def _ffn_kernel(shift, carry_rows, x_ref, oa_ref, ob_ref, oc_ref, wo_ref, g_ref, wup_ref, cw_ref, wdn_ref, c0_ref,
                y_ref, last_ref, carry_ref, ext_ref, hid_ref):
    tm = x_ref.shape[0]

    @pl.when(pl.program_id(0) == 0)
    def _():
        carry_ref[...] = c0_ref[...]

    mix = _dot(oa_ref[...], wo_ref[0:WIDTH_A, :])
    mix = mix + _dot(ob_ref[...], wo_ref[WIDTH_A:WIDTH_A + WIDTH_B, :])
    mix = mix + _dot(oc_ref[...], wo_ref[WIDTH_A + WIDTH_B:, :])
    x = x_ref[...] + mix
    ms = jnp.mean(x * x, axis=-1, keepdims=True)
    xn = (x * lax.rsqrt(ms + EPS) * g_ref[...]).astype(BF16)
    tf = FF_TILE
    base = carry_rows
    for j in range(D_FF // tf):
        halves = []
        for half in range(2):
            cols = slice(half * D_FF + j * tf, half * D_FF + (j + 1) * tf)
            up = _dot(xn, wup_ref[:, cols])
            ext_ref[0:base, :] = carry_ref[:, cols]
            ext_ref[base:base + tm, :] = up
            carry_ref[:, cols] = ext_ref[tm:tm + base, :]
            cw = cw_ref[:, cols]
            halves.append(cw[0:1] * ext_ref[base - 2 * shift:base - 2 * shift + tm, :]
                          + cw[1:2] * ext_ref[base - shift:base - shift + tm, :] + cw[2:3] * up)
        hid_ref[:, j * tf:(j + 1) * tf] = (_silu(halves[0]) * halves[1]).astype(BF16)
    y_ref[...] = x + _dot(hid_ref[...], wdn_ref[...])
    last_ref[...] = carry_ref[...]


def _ffn(x, oa, ob, oc, w_out, g, w_up, conv_w, w_down, carry0, shift, tm):
    rows = x.shape[0]
    carry_rows = carry0.shape[0]
    row = lambda w: pl.BlockSpec((tm, w), lambda i: (i, 0))
    once = lambda shape: pl.BlockSpec(shape, lambda i: (0,) * len(shape), pipeline_mode=pl.Buffered(1))
    return pl.pallas_call(
        functools.partial(_ffn_kernel, shift, carry_rows),
        grid=(rows // tm,),
        in_specs=[row(D_MODEL), row(WIDTH_A), row(WIDTH_B), row(WIDTH_C), once((D_MODEL, D_MODEL)),
                  once((1, D_MODEL)), once((D_MODEL, 2 * D_FF)), once((3, 2 * D_FF)),
                  once((D_FF, D_MODEL)), once((carry_rows, 2 * D_FF))],
        out_specs=[row(D_MODEL), _full((carry_rows, 2 * D_FF))],
        out_shape=[jax.ShapeDtypeStruct((rows, D_MODEL), F32), jax.ShapeDtypeStruct((carry_rows, 2 * D_FF), F32)],
        scratch_shapes=[pltpu.VMEM((carry_rows, 2 * D_FF), F32), pltpu.VMEM((tm + carry_rows, FF_TILE), F32),
                        pltpu.VMEM((tm, D_FF), BF16)],
        compiler_params=_params("arbitrary"),
        name=f"ffn_shift{shift}",
    )(x, oa, ob, oc, w_out, g, w_up, conv_w, w_down, carry0)


PAIR_ROWS = 16
NEW_ROWS = 16


def _sample_bias(wbuf, s_len):
    def table(key_pos):
        q_pos = wbuf + np.arange(s_len)
        dist = q_pos[:, None] - key_pos[None, :]
        mult = np.zeros(dist.shape)
        for window, dil in PATTERNS:
            mult += (dist >= 0) & (dist <= window) & (dist % dil == 0)
        logm = np.log(np.maximum(mult, 1))
        bias = -_alibi_slopes()[:, None, None] * np.maximum(dist, 0)[None] + logm[None]
        bias = np.where(mult[None] > 0, bias, NEG)
        out = np.zeros((N_HEADS_A, PAIR_ROWS // 2, key_pos.shape[0]))
        out[:, :s_len] = bias
        return out.reshape(N_HEADS_A // 2, PAIR_ROWS, -1)
    main = table(np.arange(wbuf))
    new = table(np.concatenate([wbuf + np.arange(s_len), np.full(NEW_ROWS - s_len, 10 ** 6)]))
    new[:, :, s_len:] = NEG
    return jnp.asarray(main, F32), jnp.asarray(new, F32)


def _sample_attn_units(q_ref, kn_ref, vn_ref, kt_ref, vt_ref, bm_ref, bn_ref, qc_ref, mkt_ref, mvt_ref, oa_ref, oc_ref):
    def times_values(p, vt):
        return _dot_nt(p.astype(BF16), vt.astype(BF16))

    def window(b):
        scores = []
        for pair in range(N_HEADS_A // 2):
            rows = slice(pair * LANES, (pair + 1) * LANES)
            q = q_ref[b, pair]
            scores.append((_dot(q, kt_ref[b, rows, :].astype(BF16)) + bm_ref[pair],
                           _dot_nt(q, kn_ref[b, :, rows].astype(BF16)) + bn_ref[pair]))
        for pair, (s_main, s_new) in enumerate(scores):
            rows = slice(pair * LANES, (pair + 1) * LANES)
            m = jnp.maximum(jnp.max(s_main, axis=-1, keepdims=True), jnp.max(s_new, axis=-1, keepdims=True))
            p_main = jnp.exp(s_main - m)
            p_new = jnp.exp(s_new - m)
            den = jnp.sum(p_main, axis=-1, keepdims=True) + jnp.sum(p_new, axis=-1, keepdims=True)
            num = times_values(p_main, vt_ref[b, rows, :])
            num = num + _dot(p_new.astype(BF16), vn_ref[b, :, rows].astype(BF16))
            oa_ref[b, pair] = num / den

    def memory(b):
        for pair in range(N_HEADS_C // 2):
            rows = slice(pair * LANES, (pair + 1) * LANES)
            q = qc_ref[b, pair]
            s = _dot(q, mkt_ref[b, rows, :].astype(BF16))
            p = jnp.exp(s - jnp.max(s, axis=-1, keepdims=True))
            oc_ref[b, pair] = times_values(p, mvt_ref[b, rows, :]) / jnp.sum(p, axis=-1, keepdims=True)

    units = []
    for b in range(q_ref.shape[0]):
        units += [functools.partial(window, b), functools.partial(memory, b)]
    return units


def _run_share(units, stage, n_stages):
    for u in units[stage * len(units) // n_stages:(stage + 1) * len(units) // n_stages]:
        u()


def _pair_rows(q, s_len):
    nb, _, nh, hd = q.shape
    q = q.reshape(nb, s_len, nh // 2, 2, hd).transpose(0, 2, 3, 1, 4)
    q = jnp.pad(q, ((0, 0), (0, 0), (0, 0), (0, PAIR_ROWS // 2 - s_len), (0, 0)))
    own = jnp.eye(2, dtype=q.dtype)[None, None, :, None, :, None]
    return (q[:, :, :, :, None, :] * own).reshape(nb, nh // 2, PAIR_ROWS, 2 * hd)


def _unpair_rows(o, s_len):
    nb, npair = o.shape[:2]
    o = o.reshape(nb, npair, 2, PAIR_ROWS // 2, 2, HEAD_DIM)
    o = jnp.stack([o[:, :, half, :s_len, half, :] for half in range(2)], axis=2)
    return o.transpose(0, 3, 1, 2, 4).reshape(nb, s_len, npair * 2 * HEAD_DIM)


def _sample_attn_riders(qa, ka, va, cache_k, cache_v, qc, mem_k, mem_v, n_parts, steps):
    nb, s_len = qa.shape[:2]
    wbuf = cache_k.shape[1]
    n_mem = mem_k.shape[1]
    assert s_len <= PAIR_ROWS // 2 and nb % (n_parts * steps) == 0
    g = nb // (n_parts * steps)
    bm, bn = _sample_bias(wbuf, s_len)
    transposed = lambda a: a.transpose(0, 2, 3, 1).reshape(nb, a.shape[2] * a.shape[3], a.shape[1])
    pad_new = lambda a: jnp.pad(a, ((0, 0), (0, NEW_ROWS - s_len), (0, 0)))
    args = [_pair_rows(qa, s_len), pad_new(ka), pad_new(va), transposed(cache_k), transposed(cache_v), bm, bn,
            _pair_rows(qc, s_len), transposed(mem_k), transposed(mem_v)]
    riders = []
    for part in range(n_parts):
        per_b = lambda shape, part=part: pl.BlockSpec((g,) + shape, lambda i: (i + part * steps,) + (0,) * len(shape))
        out_b = lambda shape: pl.BlockSpec((g,) + shape, lambda i: (i,) + (0,) * len(shape))
        riders.append(dict(
            units=_sample_attn_units,
            in_specs=[per_b((N_HEADS_A // 2, PAIR_ROWS, LANES)), per_b((NEW_ROWS, WIDTH_A)), per_b((NEW_ROWS, WIDTH_A)),
                      per_b((WIDTH_A, wbuf)), per_b((WIDTH_A, wbuf)), _full(bm.shape), _full(bn.shape),
                      per_b((N_HEADS_C // 2, PAIR_ROWS, LANES)), per_b((WIDTH_C, n_mem)), per_b((WIDTH_C, n_mem))],
            out_specs=[out_b((N_HEADS_A // 2, PAIR_ROWS, LANES)), out_b((N_HEADS_C // 2, PAIR_ROWS, LANES))],
            out_shape=[jax.ShapeDtypeStruct((g * steps, N_HEADS_A // 2, PAIR_ROWS, LANES), F32),
                       jax.ShapeDtypeStruct((g * steps, N_HEADS_C // 2, PAIR_ROWS, LANES), F32)],
            args=args))

    def finish(rider_outs):
        oa = jnp.concatenate([o[0] for o in rider_outs], axis=0)
        oc = jnp.concatenate([o[1] for o in rider_outs], axis=0)
        return _unpair_rows(oa, s_len).astype(BF16), _unpair_rows(oc, s_len).astype(BF16)

    return riders, finish


def _sample_gdn_kernel(xe_ref, cw_ref, a_ref, b_ref, alog_ref, dtb_ref, gate_ref, gn_ref, s_ref, o_ref, s_out_ref,
                       q_scr, k_scr, v_scr):
    s_len = o_ref.shape[0]
    s_out_ref[...] = s_ref[...]
    for t in range(s_len):
        for part, scr in enumerate((q_scr, k_scr, v_scr)):
            y = sum(cw_ref[j, part] * xe_ref[t + j, part] for j in range(4))
            y = _silu(y)
            if part < 2:
                y = y * lax.rsqrt(jnp.sum(y * y, axis=0, keepdims=True) + EPS)
            if part == 0:
                y = y * (HEAD_DIM ** -0.5)
            scr[...] = y
        decay = jnp.exp(-jnp.exp(alog_ref[...]) * _softplus(a_ref[t] + dtb_ref[...]))
        beta = _sigmoid(b_ref[t])

        def read_body(dk, r):
            return r + k_scr[pl.ds(dk, 1), :] * s_out_ref[dk]

        r = lax.fori_loop(0, HEAD_DIM, read_body, jnp.zeros(v_scr.shape, F32), unroll=8)
        v_new = beta * (v_scr[...] - decay * r)

        def write_body(dk, o):
            s_new = decay * s_out_ref[dk] + k_scr[pl.ds(dk, 1), :] * v_new
            s_out_ref[dk] = s_new
            return o + q_scr[pl.ds(dk, 1), :] * s_new

        o = lax.fori_loop(0, HEAD_DIM, write_body, jnp.zeros(v_scr.shape, F32), unroll=8)
        ms = jnp.mean(o * o, axis=0, keepdims=True)
        o_ref[t] = o * lax.rsqrt(ms + EPS) * gn_ref[...] * _silu(gate_ref[t])


def _sample_gdn(xe, conv_w, a_b, b_b, a_log, dt_bias, gate, out_norm, state):
    nb, ext, _ = xe.shape
    s_len = ext - 3
    nh, hd = N_HEADS_B, HEAD_DIM
    xe_t = xe.reshape(nb, ext, 3, nh, hd).transpose(1, 2, 3, 4, 0)
    cw_t = jnp.broadcast_to(conv_w.reshape(4, 3, nh, hd)[..., None], (4, 3, nh, hd, nb))
    a_t = a_b.transpose(1, 2, 0).reshape(s_len, nh, 1, nb)
    b_t = b_b.transpose(1, 2, 0).reshape(s_len, nh, 1, nb)
    alog_t = jnp.broadcast_to(a_log.reshape(nh, 1, 1), (nh, 1, nb))
    dtb_t = jnp.broadcast_to(dt_bias.reshape(nh, 1, 1), (nh, 1, nb))
    gate_t = gate.reshape(nb, s_len, nh, hd).transpose(1, 2, 3, 0)
    gn_t = jnp.broadcast_to(out_norm.reshape(1, hd, 1), (1, hd, nb))
    s_t = state.transpose(1, 2, 3, 0)
    o_t, s_new = pl.pallas_call(
        _sample_gdn_kernel,
        grid=(nh,),
        in_specs=[pl.BlockSpec((ext, 3, None, hd, nb), lambda h: (0, 0, h, 0, 0)),
                  pl.BlockSpec((4, 3, None, hd, nb), lambda h: (0, 0, h, 0, 0)),
                  pl.BlockSpec((s_len, None, 1, nb), lambda h: (0, h, 0, 0)),
                  pl.BlockSpec((s_len, None, 1, nb), lambda h: (0, h, 0, 0)),
                  pl.BlockSpec((None, 1, nb), lambda h: (h, 0, 0)),
                  pl.BlockSpec((None, 1, nb), lambda h: (h, 0, 0)),
                  pl.BlockSpec((s_len, None, hd, nb), lambda h: (0, h, 0, 0)),
                  pl.BlockSpec((None, hd, nb), lambda h: (0, 0, 0)),
                  pl.BlockSpec((None, hd, hd, nb), lambda h: (h, 0, 0, 0))],
        out_specs=[pl.BlockSpec((s_len, None, hd, nb), lambda h: (0, h, 0, 0)),
                   pl.BlockSpec((None, hd, hd, nb), lambda h: (h, 0, 0, 0))],
        out_shape=[jax.ShapeDtypeStruct((s_len, nh, hd, nb), F32), jax.ShapeDtypeStruct((nh, hd, hd, nb), F32)],
        scratch_shapes=[pltpu.VMEM((hd, nb), F32)] * 3,
        compiler_params=_params("parallel"),
        name="sample_gdn",
    )(xe_t, cw_t, a_t, b_t, alog_t, dtb_t, gate_t, gn_t, s_t)
    ob = o_t.transpose(3, 0, 1, 2).reshape(nb * s_len, nh * hd).astype(BF16)
    return ob, s_new.transpose(3, 0, 1, 2)


def _lane_row(v, width=LANES):
    return jnp.zeros((1, width), F32).at[0, :v.shape[0]].set(v)


def kernel(x_prompt, x_sample, cache_win_k, cache_win_v, state_gdn, state_gdn_conv, state_ffn_conv, cache_mem_k,
           cache_mem_v, mem_prompt, norm1_g, w_in, q_norm_a, k_norm_a, conv_b_w, a_log_b, dt_bias_b, out_norm_b,
           mem_norm_g, w_mem_kv, q_norm_c, k_norm_c, w_out, norm2_g, w_up, conv_ffn_w, w_down):
    depth = norm1_g.shape[0]
    assert depth == 1 and x_prompt.shape[0] == 1
    l = 0
    t_p = x_prompt.shape[1]
    nb, s_len = x_sample.shape[:2]
    xp = x_prompt.reshape(t_p, D_MODEL)
    xs = x_sample.reshape(nb * s_len, D_MODEL)

    w_in_p = w_in[l].T.astype(BF16)
    w_out_b = w_out[l].astype(BF16)
    w_up_b = w_up[l].astype(BF16)
    w_down_b = w_down[l].astype(BF16)
    g1 = norm1_g[l].reshape(1, D_MODEL)
    g2 = norm2_g[l].reshape(1, D_MODEL)
    gq = jnp.tile(q_norm_a[l], N_HEADS_A).reshape(1, WIDTH_A)
    gk = jnp.tile(k_norm_a[l], N_HEADS_A).reshape(1, WIDTH_A)
    gqc = jnp.tile(q_norm_c[l], N_HEADS_C).reshape(1, WIDTH_C)
    gkc = jnp.tile(k_norm_c[l], N_HEADS_C).reshape(1, WIDTH_C)
    gn_row = jnp.tile(out_norm_b[l], N_HEADS_B).reshape(1, WIDTH_B)

    rows_s = nb * s_len
    (qa_s,), _, _, (ka_s, va_s, qkv_s, gate_s, qc_s, ab_s) = _inproj(xs, g1, w_in_p, gq, gk, gqc, (1,), rows_s,
                                                                         HEAD_DIM ** -0.5)
    b3 = lambda a: a.reshape(nb, s_len, a.shape[-1])
    heads = lambda a: a.reshape(nb, s_len, -1, HEAD_DIM)
    riders, finish_sample_attn = _sample_attn_riders(
        heads(qa_s), b3(ka_s), b3(va_s), cache_win_k[l], cache_win_v[l], heads(qc_s), cache_mem_k[l], cache_mem_v[l],
        n_parts=2, steps=t_p // ROW_TILE)

    n_keep = min(MAX_WINDOW, t_p)
    q_ds, k_ds, v_ds, (ka, va, qkv, gate, qc, ab) = _inproj(xp, g1, w_in_p, gq, gk, gqc, DILATIONS, n_keep,
                                                            HEAD_DIM ** -0.5 * LOG2E)
    parts = [_attn_band(q_d, k_d, v_d, dil) for q_d, k_d, v_d, dil in zip(q_ds, k_ds, v_ds, DILATIONS)]
    mk, mv = _memkv(mem_prompt[0], mem_norm_g[l].reshape(1, D_MODEL), w_mem_kv[l].astype(BF16), gkc)
    oa, oc = _combine_mem([p[0] for p in parts], [p[1] for p in parts], DILATIONS, qc, mk, mv)
    ob, gdn_p, rider_outs = _gdn_prompt(qkv, ab, gate, conv_b_w[l], _lane_row(a_log_b[l]), _lane_row(dt_bias_b[l]),
                                        gn_row, riders)
    y_p, last_p = _ffn(xp, oa, ob, oc, w_out_b, g2, w_up_b, conv_ffn_w[l], w_down_b, jnp.zeros((8, 2 * D_FF), F32), 1,
                       FFN_ROW_TILE)
    win_k_p = ka.reshape(1, 1, n_keep, N_HEADS_A, HEAD_DIM)
    win_v_p = va.reshape(1, 1, n_keep, N_HEADS_A, HEAD_DIM)
    gconv_p = qkv[t_p - 3:].reshape(1, 1, 3, QKV_B)
    fconv_p = last_p[6:8].reshape(1, 1, 2, 2 * D_FF)

    oa_s, oc_s = finish_sample_attn(rider_outs)
    xe_s = jnp.concatenate([state_gdn_conv[l], b3(qkv_s)], axis=1)
    ab3 = b3(ab_s)
    ob_s, gdn_s = _sample_gdn(xe_s, conv_b_w[l], ab3[..., 0:N_HEADS_B], ab3[..., N_HEADS_B:2 * N_HEADS_B], a_log_b[l],
                              dt_bias_b[l], b3(gate_s), out_norm_b[l], state_gdn[l])
    time_major = lambda a: a.reshape(nb, s_len, a.shape[-1]).transpose(1, 0, 2).reshape(rows_s, a.shape[-1])
    carry_s = state_ffn_conv[l].transpose(1, 0, 2).reshape(2 * nb, 2 * D_FF)
    y_t, last_s = _ffn(time_major(xs), time_major(oa_s), time_major(ob_s), time_major(oc_s), w_out_b, g2, w_up_b,
                       conv_ffn_w[l], w_down_b, carry_s, nb, rows_s)
    y_s = y_t.reshape(s_len, nb, D_MODEL).transpose(1, 0, 2)
    fconv_s = last_s.reshape(2, nb, 2 * D_FF).transpose(1, 0, 2)[None]

    return (y_p.reshape(1, t_p, D_MODEL), y_s,
            win_k_p, win_v_p,
            ka_s.reshape(1, nb, s_len, N_HEADS_A, HEAD_DIM), va_s.reshape(1, nb, s_len, N_HEADS_A, HEAD_DIM),
            gdn_p[None, None], gdn_s[None],
            gconv_p, xe_s[:, -3:][None],
            fconv_p, fconv_s,
            mk.reshape(1, 1, N_MEM, N_HEADS_C, HEAD_DIM), mv.reshape(1, 1, N_MEM, N_HEADS_C, HEAD_DIM))
```

```python
import functools

import numpy as np
import jax
import jax.numpy as jnp
from jax import lax
from jax.experimental import pallas as pl
from jax.experimental.pallas import tpu as pltpu

F32 = jnp.float32
BF16 = jnp.bfloat16
EPS = 1e-6
NEG = -1e30
LOG2E = 1.4426950408889634
SQ_PARTS = 1

D_MODEL = 1024
HEAD_DIM = 64
N_HEADS_A = 8
PATTERNS = ((128, 1), (512, 4), (2048, 16))
DILATIONS = tuple(d for _, d in PATTERNS)
MAX_WINDOW = 2048
N_HEADS_B = 4
N_HEADS_C = 4
N_MEM = 256
D_FF = 2816
WIDTH_A = N_HEADS_A * HEAD_DIM
WIDTH_B = N_HEADS_B * HEAD_DIM
WIDTH_C = N_HEADS_C * HEAD_DIM
QKV_B = 3 * WIDTH_B
CHUNK = 64
NBAND = 128
BAND_BLOCKS = 16
COMBINE_ROW_TILE = 1024
ROW_TILE = 512
FFN_ROW_TILE = 512
FF_TILE = 256
LANES = 128
VMEM_LIMIT = 56 * 1024 * 1024

SEG_QA, SEG_KA, SEG_VA = (0, 512), (512, 1024), (1024, 1536)
SEG_QKV, SEG_GATE, SEG_TAIL = (1536, 2304), (2304, 2560), (2560, 2824)
N_IN = 2824


def _params(*sem):
    return pltpu.CompilerParams(dimension_semantics=sem, vmem_limit_bytes=VMEM_LIMIT)


def _dot(a, b):
    return jnp.dot(a, b, preferred_element_type=F32)


def _dot_nt(a, b):
    return lax.dot_general(a, b, (((1,), (1,)), ((), ())), preferred_element_type=F32)


def _dot_tn(a, b):
    return lax.dot_general(a, b, (((0,), (0,)), ((), ())), preferred_element_type=F32)


def _dot_split(a, b, parts, left=False):
    acc = None
    rem = a
    for _ in range(parts):
        piece = rem.astype(BF16)
        term = _dot(b, piece) if left else _dot(piece, b)
        acc = term if acc is None else acc + term
        rem = rem - piece.astype(F32)
    return acc


def _sigmoid(x):
    return 1.0 / (1.0 + jnp.exp(-x))


def _silu(x):
    return x * _sigmoid(x)


def _softplus(x):
    return jnp.maximum(x, 0.0) + jnp.log1p(jnp.exp(-jnp.abs(x)))


def _group_ones(width, group=HEAD_DIM):
    i = np.arange(width)
    return jnp.asarray((i[:, None] // group) == (i[None, :] // group), BF16)


def _full(shape):
    nd = len(shape)
    return pl.BlockSpec(shape, lambda *_: (0,) * nd)


def _call_with_rider(host_kernel, rider, *, grid, in_specs, out_specs, out_shape, scratch_shapes, args, sem, name):
    if rider is None:
        rider = dict(units=None, in_specs=[], out_specs=[], out_shape=[], args=[])
    n_hi, n_ri, n_ho, n_ro = len(in_specs), len(rider["in_specs"]), len(out_specs), len(rider["out_specs"])

    def body(*refs):
        host_in, refs = refs[:n_hi], refs[n_hi:]
        rider_in, refs = refs[:n_ri], refs[n_ri:]
        host_out, refs = refs[:n_ho], refs[n_ho:]
        rider_out, scratch = refs[:n_ro], refs[n_ro:]
        units = rider["units"](*rider_in, *rider_out) if rider["units"] is not None else []
        host_kernel(units, *host_in, *host_out, *scratch)

    outs = pl.pallas_call(
        body, grid=grid, in_specs=list(in_specs) + rider["in_specs"], out_specs=list(out_specs) + rider["out_specs"],
        out_shape=list(out_shape) + rider["out_shape"], scratch_shapes=scratch_shapes,
        compiler_params=_params(sem), name=name,
    )(*args, *rider["args"])
    return outs[:n_ho], outs[n_ho:]


def _inproj_kernel(dils, scale, x_ref, g1_ref, w_ref, gq_ref, gk_ref, gc_ref, ga_ref, gcc_ref, *refs):
    nd = len(dils)
    q_refs, k_refs, v_refs = refs[0:nd], refs[nd:2 * nd], refs[2 * nd:3 * nd]
    ka_ref, va_ref, qkv_ref, gate_ref, qc_ref, ab_ref, scr_ref = refs[3 * nd:]
    tm = x_ref.shape[0]
    x = x_ref[...]
    ms = jnp.mean(x * x, axis=-1, keepdims=True)
    xn = (x * lax.rsqrt(ms + EPS) * g1_ref[...]).astype(BF16)

    def seg(s):
        return _dot_nt(xn, w_ref[s[0]:s[1], :])

    def head_norm(z, ones_ref, gain):
        ss = _dot_split(z * z, ones_ref[...], SQ_PARTS) * (1.0 / HEAD_DIM)
        return z * lax.rsqrt(ss + EPS) * gain

    def emit(z, out_refs):
        n_tiles = WIDTH_A // LANES
        for c in range(n_tiles):
            scr_ref[c] = z[:, c * LANES:(c + 1) * LANES]
        for d, ref in zip(dils, out_refs):
            if d == 1:
                ref[...] = z.astype(BF16)
            else:
                for r in range(d):
                    for c in range(n_tiles):
                        col = r * WIDTH_A + c * LANES
                        ref[:, col:col + LANES] = scr_ref[c, pl.ds(r, tm // d, stride=d), :].astype(BF16)

    emit(head_norm(seg(SEG_QA), ga_ref, gq_ref[...]) * scale, q_refs)
    ka = head_norm(seg(SEG_KA), ga_ref, gk_ref[...])
    ka_ref[...] = ka
    emit(ka, k_refs)
    va = seg(SEG_VA)
    va_ref[...] = va
    emit(va, v_refs)
    qkv_ref[...] = seg(SEG_QKV)
    gate_ref[...] = seg(SEG_GATE)
    tail = seg(SEG_TAIL)
    ab_ref[...] = tail[:, 0:LANES]
    qc = tail[:, 2 * N_HEADS_B:2 * N_HEADS_B + WIDTH_C]
    qc_ref[...] = (head_norm(qc, gcc_ref, gc_ref[...]) * scale).astype(BF16)


def _inproj(x, g1, w_in_p, gq, gk, gc, dils, keep_rows, scale):
    rows = x.shape[0]
    tm = ROW_TILE
    nt = rows // tm
    skip = nt - keep_rows // tm
    row = lambda w: pl.BlockSpec((tm, w), lambda i: (i, 0))
    tail = pl.BlockSpec((tm, WIDTH_A), lambda i: (jnp.maximum(i - skip, 0), 0))
    dil_specs = [pl.BlockSpec((tm // d, d * WIDTH_A), lambda i: (i, 0)) for d in dils]
    dil_shapes = [jax.ShapeDtypeStruct((rows // d, d * WIDTH_A), BF16) for d in dils]
    outs = [(QKV_B, F32), (WIDTH_B, F32), (WIDTH_C, BF16), (LANES, F32)]
    res = pl.pallas_call(
        functools.partial(_inproj_kernel, dils, scale),
        grid=(nt,),
        in_specs=[row(D_MODEL), _full((1, D_MODEL)), _full((N_IN, D_MODEL)), _full((1, WIDTH_A)),
                  _full((1, WIDTH_A)), _full((1, WIDTH_C)), _full((WIDTH_A, WIDTH_A)), _full((WIDTH_C, WIDTH_C))],
        out_specs=dil_specs * 3 + [tail, tail] + [row(w) for w, _ in outs],
        out_shape=dil_shapes * 3 + [jax.ShapeDtypeStruct((keep_rows, WIDTH_A), F32)] * 2
        + [jax.ShapeDtypeStruct((rows, w), dt) for w, dt in outs],
        scratch_shapes=[pltpu.VMEM((WIDTH_A // LANES, tm, LANES), F32)],
        compiler_params=_params("arbitrary"),
        name="inproj",
    )(x, g1, w_in_p, gq, gk, gc, _group_ones(WIDTH_A), _group_ones(WIDTH_C))
    nd = len(dils)
    return res[0:nd], res[nd:2 * nd], res[2 * nd:3 * nd], res[3 * nd:]


def _alibi_slopes():
    return np.exp2(-8.0 * np.arange(1, N_HEADS_A + 1, dtype=np.float64) / N_HEADS_A)


def _band_bias(dil):
    qi = np.arange(NBAND)[:, None]
    kj = np.arange(2 * NBAND)[None, :]
    delta = qi + NBAND - kj
    in_band = (delta >= 0) & (delta <= NBAND)
    bias = -_alibi_slopes()[:, None, None] * (delta * dil)[None].astype(np.float64)
    general = np.where(in_band[None], bias, NEG)
    first = np.where((in_band & (kj >= NBAND))[None], bias, NEG)
    return jnp.asarray(np.stack([general, first]) * LOG2E, F32)


def _attn_band_kernel(q_ref, kp_ref, kc_ref, vp_ref, vc_ref, bias_ref, o_ref, lse_ref):
    lane = lax.broadcasted_iota(jnp.int32, (NBAND, LANES), 1)
    low = lane < HEAD_DIM
    ones = jnp.ones((2 * NBAND, LANES), BF16)
    for blk in range(q_ref.shape[0] // NBAND):
        first = (pl.program_id(1) == 0).astype(jnp.int32) if blk == 0 else 0
        rows = slice(blk * NBAND, (blk + 1) * NBAND)
        lse = jnp.zeros((NBAND, LANES), F32)
        for pair in range(N_HEADS_A // 2):
            cols = slice(pair * LANES, (pair + 1) * LANES)
            q = q_ref[rows, cols]
            if blk == 0:
                k = jnp.concatenate([kp_ref[:, cols], kc_ref[rows, cols]], axis=0)
                v = jnp.concatenate([vp_ref[:, cols], vc_ref[rows, cols]], axis=0)
            else:
                k = kc_ref[(blk - 1) * NBAND:(blk + 1) * NBAND, cols]
                v = vc_ref[(blk - 1) * NBAND:(blk + 1) * NBAND, cols]
            v_ext = jnp.concatenate([v, ones], axis=1)
            outs = []
            for half in range(2):
                h = 2 * pair + half
                qm = jnp.where(low if half == 0 else ~low, q, jnp.zeros((), BF16))
                s = _dot_nt(qm, k) + bias_ref[first, h]
                m = jnp.max(s, axis=-1, keepdims=True)
                p = jnp.exp2(s - m).astype(BF16)
                r = _dot(p, v_ext)
                outs.append(r[:, :LANES])
                lse = jnp.where(lane == h, m, lse)
                lse = jnp.where(lane == N_HEADS_A + h, r[:, LANES:], lse)
            o_ref[rows, cols] = jnp.where(low, outs[0], outs[1]).astype(BF16)
        lse_ref[rows, :] = lse


def _attn_band(q_d, k_d, v_d, dil):
    rows = q_d.shape[0]
    blocks = min(BAND_BLOCKS, rows // NBAND)
    step = blocks * NBAND
    cur = lambda w: pl.BlockSpec((step, w), lambda r, n: (n, r))
    prev = lambda w: pl.BlockSpec((NBAND, w), lambda r, n: (jnp.maximum(n * blocks - 1, 0), r))
    return pl.pallas_call(
        _attn_band_kernel,
        grid=(dil, rows // step),
        in_specs=[cur(WIDTH_A), prev(WIDTH_A), cur(WIDTH_A), prev(WIDTH_A), cur(WIDTH_A),
                  _full((2, N_HEADS_A, NBAND, 2 * NBAND))],
        out_specs=[cur(WIDTH_A), cur(LANES)],
        out_shape=[jax.ShapeDtypeStruct((rows, dil * WIDTH_A), BF16), jax.ShapeDtypeStruct((rows, dil * LANES), F32)],
        compiler_params=_params("parallel", "arbitrary"),
        name=f"attn_band_d{dil}",
    )(q_d, k_d, k_d, v_d, v_d, _band_bias(dil))


def _memkv_kernel(mem_ref, g_ref, w_ref, gk_ref, ones_ref, mk_ref, mv_ref):
    x = mem_ref[...]
    ms = jnp.mean(x * x, axis=-1, keepdims=True)
    xn = (x * lax.rsqrt(ms + EPS) * g_ref[...]).astype(BF16)
    zk = _dot(xn, w_ref[:, 0:WIDTH_C])
    ss = _dot_split(zk * zk, ones_ref[...], SQ_PARTS) * (1.0 / HEAD_DIM)
    mk_ref[...] = zk * lax.rsqrt(ss + EPS) * gk_ref[...]
    mv_ref[...] = _dot(xn, w_ref[:, WIDTH_C:2 * WIDTH_C])


def _memkv(mem, g, w, gk):
    n = mem.shape[0]
    return pl.pallas_call(
        _memkv_kernel,
        out_shape=[jax.ShapeDtypeStruct((n, WIDTH_C), F32)] * 2,
        compiler_params=pltpu.CompilerParams(vmem_limit_bytes=VMEM_LIMIT),
        name="memkv",
    )(mem, g, w, gk, _group_ones(WIDTH_C))


def _head_spread():
    e = np.zeros((LANES, WIDTH_A), np.float32)
    for h in range(N_HEADS_A):
        e[h, h * HEAD_DIM:(h + 1) * HEAD_DIM] = 1.0
    return jnp.asarray(e, BF16)


def _combine_mem_kernel(dils, *refs):
    nd = len(dils)
    o_refs, l_refs = refs[0:nd], refs[nd:2 * nd]
    qc_ref, mk_ref, mv_ref, e_ref, oa_ref, oc_ref, o_scr, l_scr = refs[2 * nd:]
    tm = oa_ref.shape[0]
    os_, ls = [], []
    for p, d in enumerate(dils):
        if d == 1:
            os_.append(o_refs[p][...].astype(F32))
            ls.append(l_refs[p][...])
            continue
        n_tiles = WIDTH_A // LANES
        for r in range(d):
            for c in range(n_tiles):
                col = r * WIDTH_A + c * LANES
                o_scr[p * n_tiles + c, pl.ds(r, tm // d, stride=d), :] = o_refs[p][:, col:col + LANES].astype(F32)
            l_scr[p, pl.ds(r, tm // d, stride=d), :] = l_refs[p][:, r * LANES:(r + 1) * LANES]
        os_.append(jnp.concatenate([o_scr[p * n_tiles + c] for c in range(n_tiles)], axis=1))
        ls.append(l_scr[p])
    head_lane = lax.broadcasted_iota(jnp.int32, (tm, LANES), 1) < N_HEADS_A
    m = functools.reduce(jnp.maximum, ls)
    es = [jnp.exp2(l - m) for l in ls]
    dens = [pltpu.roll(l, LANES - N_HEADS_A, 1) for l in ls]
    tot = functools.reduce(lambda a, b: a + b, [d * e for d, e in zip(dens, es)])
    tot = jnp.where(head_lane, tot, 1.0)
    acc = None
    for o, e in zip(os_, es):
        term = o * _dot_split(e / tot, e_ref[...], 1)
        acc = term if acc is None else acc + term
    oa_ref[...] = acc.astype(BF16)

    lane = lax.broadcasted_iota(jnp.int32, (tm, LANES), 1)
    low = lane < HEAD_DIM
    ones = jnp.ones((N_MEM, LANES), BF16)
    for pair in range(N_HEADS_C // 2):
        cols = slice(pair * LANES, (pair + 1) * LANES)
        q = qc_ref[:, cols]
        mk = mk_ref[:, cols].astype(BF16)
        v_ext = jnp.concatenate([mv_ref[:, cols].astype(BF16), ones], axis=1)
        outs = []
        for half in range(2):
            qm = jnp.where(low if half == 0 else ~low, q, jnp.zeros((), BF16))
            s = _dot_nt(qm, mk)
            p = jnp.exp2(s - jnp.max(s, axis=-1, keepdims=True)).astype(BF16)
            r = _dot(p, v_ext)
            outs.append(r[:, :LANES] / r[:, LANES:])
        oc_ref[:, cols] = jnp.where(low, outs[0], outs[1]).astype(BF16)


def _combine_mem(os_, lses, dils, qc, mk, mv):
    t = qc.shape[0]
    tm = COMBINE_ROW_TILE
    nd = len(dils)
    row = lambda w: pl.BlockSpec((tm, w), lambda i: (i, 0))
    o_specs = [pl.BlockSpec((tm // d, d * WIDTH_A), lambda i: (i, 0)) for d in dils]
    l_specs = [pl.BlockSpec((tm // d, d * LANES), lambda i: (i, 0)) for d in dils]
    return pl.pallas_call(
        functools.partial(_combine_mem_kernel, dils),
        grid=(t // tm,),
        in_specs=o_specs + l_specs + [row(WIDTH_C), _full((N_MEM, WIDTH_C)), _full((N_MEM, WIDTH_C)),
                                     _full((LANES, WIDTH_A))],
        out_specs=[row(WIDTH_A), row(WIDTH_C)],
        out_shape=[jax.ShapeDtypeStruct((t, WIDTH_A), BF16), jax.ShapeDtypeStruct((t, WIDTH_C), BF16)],
        scratch_shapes=[pltpu.VMEM((nd * WIDTH_A // LANES, tm, LANES), F32), pltpu.VMEM((nd, tm, LANES), F32)],
        compiler_params=_params("parallel"),
        name="combine_mem",
    )(*os_, *lses, qc, mk, mv, _head_spread())


def _head_lane_consts():
    e_g = np.zeros((LANES, WIDTH_B), np.float32)
    e_b = np.zeros((LANES, WIDTH_B), np.float32)
    for h in range(N_HEADS_B):
        e_g[h, h * HEAD_DIM:(h + 1) * HEAD_DIM] = 1.0
        e_b[N_HEADS_B + h, h * HEAD_DIM:(h + 1) * HEAD_DIM] = 1.0
    i = np.arange(CHUNK)
    ltri = (i[None, :] <= i[:, None]).astype(np.float32)
    return jnp.asarray(e_g, BF16), jnp.asarray(e_b, BF16), jnp.asarray(ltri, BF16), jnp.ones((CHUNK, CHUNK), BF16)


def _block_diag(x, mask):
    return jnp.where(mask, jnp.concatenate([x] * N_HEADS_B, axis=0), jnp.zeros((), x.dtype))


def _gdn_prep_kernel(units, x_ref, halo_ref, ab_ref, cw_ref, alog_ref, dtb_ref, ones_ref, eg_ref, eb_ref, ltri_ref, one64_ref,
                     m_ref, attn_ref, qd_ref, kd_ref, vb_ref, kbe_ref, egl_ref, xe_ref, m_scr):
    tm = x_ref.shape[0]
    nc = tm // CHUNK
    halo = jnp.where(pl.program_id(0) > 0, halo_ref[...], 0.0)
    xe_ref[0:8, :] = halo
    xe_ref[8:8 + tm, :] = x_ref[...]
    cw = cw_ref[...]
    y = cw[3:4] * xe_ref[8:8 + tm, :] + cw[2:3] * xe_ref[7:7 + tm, :] + cw[1:2] * xe_ref[6:6 + tm, :] \
        + cw[0:1] * xe_ref[5:5 + tm, :]
    y = _silu(y)

    def l2n(z):
        return z * lax.rsqrt(_dot_split(z * z, ones_ref[...], SQ_PARTS) + EPS)

    q = l2n(y[:, 0:WIDTH_B]) * (HEAD_DIM ** -0.5)
    k = l2n(y[:, WIDTH_B:2 * WIDTH_B])
    v = y[:, 2 * WIDTH_B:3 * WIDTH_B]
    ab = ab_ref[...]
    g = -jnp.exp(alog_ref[...]) * _softplus(ab + dtb_ref[...])
    gx = _dot_split(g, eg_ref[...], 2)
    bx = _dot_split(_sigmoid(ab), eb_ref[...], 2)

    row = lax.broadcasted_iota(jnp.int32, (CHUNK, WIDTH_B), 0)
    col = lax.broadcasted_iota(jnp.int32, (CHUNK, WIDTH_B), 1) % CHUNK
    diag = row == col
    r4 = lax.broadcasted_iota(jnp.int32, (WIDTH_B, WIDTH_B), 0) // HEAD_DIM
    c4 = lax.broadcasted_iota(jnp.int32, (WIDTH_B, WIDTH_B), 1) // HEAD_DIM
    bd_mask = r4 == c4
    for c in range(tm // CHUNK):
        sl = slice(c * CHUNK, (c + 1) * CHUNK)
        gc = _dot_split(gx[sl], ltri_ref[...], 2, left=True)
        gl = gc[CHUNK - 1:CHUNK, :]
        gcj = _dot_split(jnp.where(diag, gc, 0.0), one64_ref[...], 2, left=True)
        dec = jnp.exp(jnp.where(row >= col, gc - gcj, NEG))
        kc, qc, vc, bc = k[sl], q[sl], v[sl], bx[sl]
        kb = kc * bc
        lhs = jnp.concatenate([kb, qc], axis=0).astype(BF16)
        kkqk = _dot_nt(lhs, _block_diag(kc.astype(BF16), bd_mask))
        m_c = jnp.where(row > col, kkqk[0:CHUNK] * dec, 0.0)
        for lt in range(WIDTH_B // LANES):
            m_scr[pl.ds(lt * nc + c, CHUNK, stride=nc * WIDTH_B // LANES), :] = m_c[:, lt * LANES:(lt + 1) * LANES]
        attn_ref[sl, :] = (kkqk[CHUNK:2 * CHUNK] * dec).astype(BF16)
        egc = jnp.exp(gc)
        qd_ref[sl, :] = (qc * egc).astype(BF16)
        kd_ref[sl, :] = (kc * jnp.exp(gl - gc)).astype(BF16)
        vb_ref[sl, :] = (vc * bc).astype(BF16)
        kbe_ref[sl, :] = (kb * egc).astype(BF16)
        egl_ref[c:c + 1, :] = jnp.exp(gl)
        _run_share(units, c, nc, spread=False)
    m_ref[...] = m_scr[...].reshape(m_ref.shape)


def _gdn_prep(qkv, ab, conv_w, alog_row, dtb_row, rider=None):
    t = qkv.shape[0]
    tm = ROW_TILE
    nc = tm // CHUNK
    assert nc == 8
    n_lt = WIDTH_B // LANES
    row = lambda w: pl.BlockSpec((tm, w), lambda i: (i, 0))
    m_spec = pl.BlockSpec((CHUNK, n_lt, nc, LANES), lambda i: (0, 0, i, 0))
    e_g, e_b, ltri, one64 = _head_lane_consts()
    outs = [BF16, BF16, BF16, BF16, BF16]
    return _call_with_rider(
        _gdn_prep_kernel, rider,
        grid=(t // tm,),
        in_specs=[row(QKV_B), pl.BlockSpec((8, QKV_B), lambda i: (jnp.maximum(i * (tm // 8) - 1, 0), 0)), row(LANES),
                  _full((4, QKV_B)), _full((1, LANES)), _full((1, LANES)), _full((WIDTH_B, WIDTH_B)),
                  _full((LANES, WIDTH_B)), _full((LANES, WIDTH_B)), _full((CHUNK, CHUNK)), _full((CHUNK, CHUNK))],
        out_specs=[m_spec] + [row(WIDTH_B)] * 5 + [pl.BlockSpec((nc, WIDTH_B), lambda i: (i, 0))],
        out_shape=[jax.ShapeDtypeStruct((CHUNK, n_lt, t // CHUNK, LANES), F32)]
        + [jax.ShapeDtypeStruct((t, WIDTH_B), dt) for dt in outs]
        + [jax.ShapeDtypeStruct((t // CHUNK, WIDTH_B), F32)],
        scratch_shapes=[pltpu.VMEM((tm + 8, QKV_B), F32), pltpu.VMEM((CHUNK * n_lt * nc, LANES), F32)],
        args=(qkv, qkv, ab, conv_w, alog_row, dtb_row, _group_ones(WIDTH_B), e_g, e_b, ltri, one64),
        sem="parallel", name="gdn_prep")


def _tinv_kernel(m_ref, out_ref, n_ref, mrow_ref, nrow_ref):
    nslots = n_ref.shape[2]
    heads_per_tile = LANES // CHUNK
    blk = pl.program_id(0)

    @pl.when(blk == 0)
    def _():
        n_ref[...] = jnp.zeros_like(n_ref)

    for ii in range(8):
        i = blk * 8 + ii
        for g in range(nslots // N_HEADS_B):
            for lt in range(WIDTH_B // LANES):
                xt = m_ref[ii, lt, g * LANES:(g + 1) * LANES, :].T
                for h2 in range(heads_per_tile):
                    slot = g * N_HEADS_B + lt * heads_per_tile + h2
                    mrow_ref[ii, pl.ds(slot, CHUNK, stride=nslots), :] = xt[h2 * CHUNK:(h2 + 1) * CHUNK, :]
        for kb in range(8):
            k0 = 8 * kb
            init = tuple(mrow_ref[ii, (k0 + kk) * nslots:(k0 + kk + 1) * nslots, :] for kk in range(8))

            def m_at(j, ii=ii):
                return mrow_ref[ii, pl.ds(pl.multiple_of(j * nslots, nslots), nslots), :]

            def body(jj, acc, k0=k0):
                j = k0 + 2 * jj
                m0, m1 = m_at(j), m_at(j + 1)
                return tuple(acc[kk] + m0 * n_ref[j, k0 + kk] + m1 * n_ref[j + 1, k0 + kk] for kk in range(8))

            n_cols = jnp.maximum(i - k0, 0)
            acc = lax.fori_loop(0, n_cols // 2, body, init)
            odd = n_cols % 2 == 1
            j_last = jnp.where(odd, i - 1, 0)
            m_last = jnp.where(odd, m_at(j_last), 0.0)
            acc = tuple(acc[kk] + m_last * n_ref[j_last, k0 + kk] for kk in range(8))
            for kk in range(8):
                val = jnp.where(k0 + kk < i, -acc[kk], 0.0)
                n_ref[i, k0 + kk] = val
                nrow_ref[(k0 + kk) * nslots:(k0 + kk + 1) * nslots, :] = val
        for g in range(nslots // N_HEADS_B):
            for lt in range(WIDTH_B // LANES):
                slot0 = g * N_HEADS_B + lt * heads_per_tile
                parts = [nrow_ref[pl.ds(slot0 + h2, CHUNK, stride=nslots), :] for h2 in range(heads_per_tile)]
                out_ref[ii, lt, g * LANES:(g + 1) * LANES, :] = jnp.concatenate(parts, axis=0).T


def _tinv(m4):
    _, n_lt, nchunk, _ = m4.shape
    assert nchunk % LANES == 0
    nslots = (nchunk // LANES) * N_HEADS_B
    blk = pl.BlockSpec((8, n_lt, nchunk, LANES), lambda i: (i, 0, 0, 0))
    return pl.pallas_call(
        _tinv_kernel,
        grid=(CHUNK // 8,),
        in_specs=[blk],
        out_specs=blk,
        out_shape=jax.ShapeDtypeStruct(m4.shape, F32),
        scratch_shapes=[pltpu.VMEM((CHUNK, CHUNK, nslots, LANES), F32), pltpu.VMEM((8, CHUNK * nslots, LANES), F32),
                        pltpu.VMEM((CHUNK * nslots, LANES), F32)],
        compiler_params=_params("arbitrary"),
        name="gdn_tinv",
    )(m4)


def _gdn_scan_kernel(units, n_ref, vb_ref, kbe_ref, attn_ref, qd_ref, kd_ref, egl_ref, gate_ref, gn_ref, ones_ref,
                     ob_ref, s_out_ref, s_ref, n_scr):
    tm = vb_ref.shape[0]
    nc = tm // CHUNK
    n_lt = WIDTH_B // LANES

    @pl.when(pl.program_id(0) == 0)
    def _():
        s_ref[...] = jnp.zeros_like(s_ref)

    n_scr[...] = n_ref[...].reshape(n_scr.shape)

    row = lax.broadcasted_iota(jnp.int32, (CHUNK, WIDTH_B), 0)
    col = lax.broadcasted_iota(jnp.int32, (CHUNK, WIDTH_B), 1) % CHUNK
    eye = (row == col).astype(F32)
    r4 = lax.broadcasted_iota(jnp.int32, (WIDTH_B, WIDTH_B), 0) // HEAD_DIM
    c4 = lax.broadcasted_iota(jnp.int32, (WIDTH_B, WIDTH_B), 1) // HEAD_DIM
    bd_mask = r4 == c4
    s = s_ref[...]
    for c in range(tm // CHUNK):
        sl = slice(c * CHUNK, (c + 1) * CHUNK)
        n_c = jnp.concatenate([n_scr[pl.ds(lt * nc + c, CHUNK, stride=nc * n_lt), :] for lt in range(n_lt)], axis=1)
        tinv = (n_c + eye).astype(BF16)
        u = _dot(tinv, _block_diag(vb_ref[sl, :], bd_mask))
        w = _dot(tinv, _block_diag(kbe_ref[sl, :], bd_mask))
        lhs = jnp.concatenate([w.astype(BF16), qd_ref[sl, :]], axis=0)
        ws = _dot(lhs, s.astype(BF16))
        v_new = (u - ws[0:CHUNK]).astype(BF16)
        o = ws[CHUNK:2 * CHUNK] + _dot(attn_ref[sl, :], _block_diag(v_new, bd_mask))
        s = s * egl_ref[c:c + 1, :] + jnp.where(bd_mask, _dot_tn(kd_ref[sl, :], v_new), 0.0)
        ms = _dot_split(o * o, ones_ref[...], SQ_PARTS) * (1.0 / HEAD_DIM)
        ob_ref[sl, :] = (o * lax.rsqrt(ms + EPS) * gn_ref[...] * _silu(gate_ref[sl, :])).astype(BF16)
        _run_share(units, c, nc, spread=True)
    s_ref[...] = s
    s_out_ref[...] = s


def _gdn_scan(n4, vb, kbe, attn, qd, kd, egl, gate, gn_row, rider=None):
    t = vb.shape[0]
    tm = ROW_TILE
    nc = tm // CHUNK
    n_lt = WIDTH_B // LANES
    row = lambda w: pl.BlockSpec((tm, w), lambda i: (i, 0))
    return _call_with_rider(
        _gdn_scan_kernel, rider,
        grid=(t // tm,),
        in_specs=[pl.BlockSpec((CHUNK, n_lt, nc, LANES), lambda i: (0, 0, i, 0))] + [row(WIDTH_B)] * 5
        + [pl.BlockSpec((nc, WIDTH_B), lambda i: (i, 0)), row(WIDTH_B), _full((1, WIDTH_B)), _full((WIDTH_B, WIDTH_B))],
        out_specs=[row(WIDTH_B), _full((WIDTH_B, WIDTH_B))],
        out_shape=[jax.ShapeDtypeStruct((t, WIDTH_B), BF16), jax.ShapeDtypeStruct((WIDTH_B, WIDTH_B), F32)],
        scratch_shapes=[pltpu.VMEM((WIDTH_B, WIDTH_B), F32), pltpu.VMEM((CHUNK * n_lt * nc, LANES), F32)],
        args=(n4, vb, kbe, attn, qd, kd, egl, gate, gn_row, _group_ones(WIDTH_B)),
        sem="arbitrary", name="gdn_scan")


def _gdn_prompt(qkv, ab, gate, conv_w, alog_row, dtb_row, gn_row, riders=(None, None)):
    (m4, attn, qd, kd, vb, kbe, egl), ride0 = _gdn_prep(qkv, ab, conv_w, alog_row, dtb_row, riders[0])
    (ob, s_bd), ride1 = _gdn_scan(_tinv(m4), vb, kbe, attn, qd, kd, egl, gate, gn_row, riders[1])
    s_fin = jnp.stack([s_bd[h * HEAD_DIM:(h + 1) * HEAD_DIM, h * HEAD_DIM:(h + 1) * HEAD_DIM] for h in range(N_HEADS_B)])
    return ob, s_fin, (ride0, ride1)


def _ffn_kernel(shift, carry_rows, x_ref, oa_ref, ob_ref, oc_ref, wo_ref, g_ref, wup_ref, cw_ref, wdn_ref, c0_ref,
                y_ref, last_ref, carry_ref, ext_ref, hid_ref):
    tm = x_ref.shape[0]

    @pl.when(pl.program_id(0) == 0)
    def _():
        carry_ref[...] = c0_ref[...]

    mix = _dot(oa_ref[...], wo_ref[0:WIDTH_A, :])
    mix = mix + _dot(ob_ref[...], wo_ref[WIDTH_A:WIDTH_A + WIDTH_B, :])
    mix = mix + _dot(oc_ref[...], wo_ref[WIDTH_A + WIDTH_B:, :])
    x = x_ref[...] + mix
    ms = jnp.mean(x * x, axis=-1, keepdims=True)
    xn = (x * lax.rsqrt(ms + EPS) * g_ref[...]).astype(BF16)
    tf = FF_TILE
    base = carry_rows
    for j in range(D_FF // tf):
        halves = []
        for half in range(2):
            cols = slice(half * D_FF + j * tf, half * D_FF + (j + 1) * tf)
            up = _dot(xn, wup_ref[:, cols])
            ext_ref[0:base, :] = carry_ref[:, cols]
            ext_ref[base:base + tm, :] = up
            carry_ref[:, cols] = ext_ref[tm:tm + base, :]
            cw = cw_ref[:, cols]
            halves.append(cw[0:1] * ext_ref[base - 2 * shift:base - 2 * shift + tm, :]
                          + cw[1:2] * ext_ref[base - shift:base - shift + tm, :] + cw[2:3] * up)
        hid_ref[:, j * tf:(j + 1) * tf] = (_silu(halves[0]) * halves[1]).astype(BF16)
    y_ref[...] = x + _dot(hid_ref[...], wdn_ref[...])
    last_ref[...] = carry_ref[...]


def _ffn(x, oa, ob, oc, w_out, g, w_up, conv_w, w_down, carry0, shift, tm):
    rows = x.shape[0]
    carry_rows = carry0.shape[0]
    row = lambda w: pl.BlockSpec((tm, w), lambda i: (i, 0))
    once = lambda shape: pl.BlockSpec(shape, lambda i: (0,) * len(shape), pipeline_mode=pl.Buffered(1))
    return pl.pallas_call(
        functools.partial(_ffn_kernel, shift, carry_rows),
        grid=(rows // tm,),
        in_specs=[row(D_MODEL), row(WIDTH_A), row(WIDTH_B), row(WIDTH_C), once((D_MODEL, D_MODEL)),
                  once((1, D_MODEL)), once((D_MODEL, 2 * D_FF)), once((3, 2 * D_FF)),
                  once((D_FF, D_MODEL)), once((carry_rows, 2 * D_FF))],
        out_specs=[row(D_MODEL), _full((carry_rows, 2 * D_FF))],
        out_shape=[jax.ShapeDtypeStruct((rows, D_MODEL), F32), jax.ShapeDtypeStruct((carry_rows, 2 * D_FF), F32)],
        scratch_shapes=[pltpu.VMEM((carry_rows, 2 * D_FF), F32), pltpu.VMEM((tm + carry_rows, FF_TILE), F32),
                        pltpu.VMEM((tm, D_FF), BF16)],
        compiler_params=_params("arbitrary"),
        name=f"ffn_shift{shift}",
    )(x, oa, ob, oc, w_out, g, w_up, conv_w, w_down, carry0)


PAIR_ROWS = 16
NEW_ROWS = 16


def _sample_bias(wbuf, s_len):
    def table(key_pos):
        q_pos = wbuf + np.arange(s_len)
        dist = q_pos[:, None] - key_pos[None, :]
        mult = np.zeros(dist.shape)
        for window, dil in PATTERNS:
            mult += (dist >= 0) & (dist <= window) & (dist % dil == 0)
        logm = np.log(np.maximum(mult, 1))
        bias = -_alibi_slopes()[:, None, None] * np.maximum(dist, 0)[None] + logm[None]
        bias = np.where(mult[None] > 0, bias, NEG)
        out = np.zeros((N_HEADS_A, PAIR_ROWS // 2, key_pos.shape[0]))
        out[:, :s_len] = bias
        return out.reshape(N_HEADS_A // 2, PAIR_ROWS, -1)
    main = table(np.arange(wbuf))
    new = table(np.concatenate([wbuf + np.arange(s_len), np.full(NEW_ROWS - s_len, 10 ** 6)]))
    new[:, :, s_len:] = NEG
    return jnp.asarray(main, F32), jnp.asarray(new, F32)


def _sample_attn_units(q_ref, kn_ref, vn_ref, kt_ref, vt_ref, bm_ref, bn_ref, qc_ref, mkt_ref, mvt_ref, oa_ref, oc_ref):
    def times_values(p, vt):
        return _dot_nt(p.astype(BF16), vt.astype(BF16))

    def window(b):
        scores = []
        for pair in range(N_HEADS_A // 2):
            rows = slice(pair * LANES, (pair + 1) * LANES)
            q = q_ref[b, pair]
            scores.append((_dot(q, kt_ref[b, rows, :].astype(BF16)) + bm_ref[pair],
                           _dot_nt(q, kn_ref[b, :, rows].astype(BF16)) + bn_ref[pair]))
        for pair, (s_main, s_new) in enumerate(scores):
            rows = slice(pair * LANES, (pair + 1) * LANES)
            m = jnp.maximum(jnp.max(s_main, axis=-1, keepdims=True), jnp.max(s_new, axis=-1, keepdims=True))
            p_main = jnp.exp(s_main - m)
            p_new = jnp.exp(s_new - m)
            den = jnp.sum(p_main, axis=-1, keepdims=True) + jnp.sum(p_new, axis=-1, keepdims=True)
            num = times_values(p_main, vt_ref[b, rows, :])
            num = num + _dot(p_new.astype(BF16), vn_ref[b, :, rows].astype(BF16))
            oa_ref[b, pair] = num / den

    def memory(b):
        for pair in range(N_HEADS_C // 2):
            rows = slice(pair * LANES, (pair + 1) * LANES)
            q = qc_ref[b, pair]
            s = _dot(q, mkt_ref[b, rows, :].astype(BF16))
            p = jnp.exp(s - jnp.max(s, axis=-1, keepdims=True))
            oc_ref[b, pair] = times_values(p, mvt_ref[b, rows, :]) / jnp.sum(p, axis=-1, keepdims=True)

    units = []
    for b in range(q_ref.shape[0]):
        units += [functools.partial(window, b), functools.partial(memory, b)]
    return units


def _run_share(units, stage, n_stages, spread):
    lo, hi = (stage * len(units) // n_stages, (stage + 1) * len(units) // n_stages) if spread else \
        ((0, len(units)) if stage == n_stages - 1 else (0, 0))
    for u in units[lo:hi]:
        u()


def _pair_rows(q, s_len):
    nb, _, nh, hd = q.shape
    q = q.reshape(nb, s_len, nh // 2, 2, hd).transpose(0, 2, 3, 1, 4)
    q = jnp.pad(q, ((0, 0), (0, 0), (0, 0), (0, PAIR_ROWS // 2 - s_len), (0, 0)))
    own = jnp.eye(2, dtype=q.dtype)[None, None, :, None, :, None]
    return (q[:, :, :, :, None, :] * own).reshape(nb, nh // 2, PAIR_ROWS, 2 * hd)


def _unpair_rows(o, s_len):
    nb, npair = o.shape[:2]
    o = o.reshape(nb, npair, 2, PAIR_ROWS // 2, 2, HEAD_DIM)
    o = jnp.stack([o[:, :, half, :s_len, half, :] for half in range(2)], axis=2)
    return o.transpose(0, 3, 1, 2, 4).reshape(nb, s_len, npair * 2 * HEAD_DIM)


def _sample_attn_riders(qa, ka, va, cache_k, cache_v, qc, mem_k, mem_v, n_parts, steps):
    nb, s_len = qa.shape[:2]
    wbuf = cache_k.shape[1]
    n_mem = mem_k.shape[1]
    assert s_len <= PAIR_ROWS // 2 and nb % (n_parts * steps) == 0
    g = nb // (n_parts * steps)
    bm, bn = _sample_bias(wbuf, s_len)
    transposed = lambda a: a.transpose(0, 2, 3, 1).reshape(nb, a.shape[2] * a.shape[3], a.shape[1])
    pad_new = lambda a: jnp.pad(a, ((0, 0), (0, NEW_ROWS - s_len), (0, 0)))
    args = [_pair_rows(qa, s_len), pad_new(ka), pad_new(va), transposed(cache_k), transposed(cache_v), bm, bn,
            _pair_rows(qc, s_len), transposed(mem_k), transposed(mem_v)]
    riders = []
    for part in range(n_parts):
        per_b = lambda shape, part=part: pl.BlockSpec((g,) + shape, lambda i: (i + part * steps,) + (0,) * len(shape))
        out_b = lambda shape: pl.BlockSpec((g,) + shape, lambda i: (i,) + (0,) * len(shape))
        riders.append(dict(
            units=_sample_attn_units,
            in_specs=[per_b((N_HEADS_A // 2, PAIR_ROWS, LANES)), per_b((NEW_ROWS, WIDTH_A)), per_b((NEW_ROWS, WIDTH_A)),
                      per_b((WIDTH_A, wbuf)), per_b((WIDTH_A, wbuf)), _full(bm.shape), _full(bn.shape),
                      per_b((N_HEADS_C // 2, PAIR_ROWS, LANES)), per_b((WIDTH_C, n_mem)), per_b((WIDTH_C, n_mem))],
            out_specs=[out_b((N_HEADS_A // 2, PAIR_ROWS, LANES)), out_b((N_HEADS_C // 2, PAIR_ROWS, LANES))],
            out_shape=[jax.ShapeDtypeStruct((g * steps, N_HEADS_A // 2, PAIR_ROWS, LANES), F32),
                       jax.ShapeDtypeStruct((g * steps, N_HEADS_C // 2, PAIR_ROWS, LANES), F32)],
            args=args))

    def finish(rider_outs):
        oa = jnp.concatenate([o[0] for o in rider_outs], axis=0)
        oc = jnp.concatenate([o[1] for o in rider_outs], axis=0)
        return _unpair_rows(oa, s_len).astype(BF16), _unpair_rows(oc, s_len).astype(BF16)

    return riders, finish


def _sample_gdn_kernel(xe_ref, cw_ref, a_ref, b_ref, alog_ref, dtb_ref, gate_ref, gn_ref, s_ref, o_ref, s_out_ref,
                       q_scr, k_scr, v_scr):
    s_len = o_ref.shape[0]
    s_out_ref[...] = s_ref[...]
    for t in range(s_len):
        for part, scr in enumerate((q_scr, k_scr, v_scr)):
            y = sum(cw_ref[j, part] * xe_ref[t + j, part] for j in range(4))
            y = _silu(y)
            if part < 2:
                y = y * lax.rsqrt(jnp.sum(y * y, axis=0, keepdims=True) + EPS)
            if part == 0:
                y = y * (HEAD_DIM ** -0.5)
            scr[...] = y
        decay = jnp.exp(-jnp.exp(alog_ref[...]) * _softplus(a_ref[t] + dtb_ref[...]))
        beta = _sigmoid(b_ref[t])

        def read_body(dk, r):
            return r + k_scr[pl.ds(dk, 1), :] * s_out_ref[dk]

        r = lax.fori_loop(0, HEAD_DIM, read_body, jnp.zeros(v_scr.shape, F32), unroll=8)
        v_new = beta * (v_scr[...] - decay * r)

        def write_body(dk, o):
            s_new = decay * s_out_ref[dk] + k_scr[pl.ds(dk, 1), :] * v_new
            s_out_ref[dk] = s_new
            return o + q_scr[pl.ds(dk, 1), :] * s_new

        o = lax.fori_loop(0, HEAD_DIM, write_body, jnp.zeros(v_scr.shape, F32), unroll=8)
        ms = jnp.mean(o * o, axis=0, keepdims=True)
        o_ref[t] = o * lax.rsqrt(ms + EPS) * gn_ref[...] * _silu(gate_ref[t])


def _sample_gdn(xe, conv_w, a_b, b_b, a_log, dt_bias, gate, out_norm, state):
    nb, ext, _ = xe.shape
    s_len = ext - 3
    nh, hd = N_HEADS_B, HEAD_DIM
    xe_t = xe.reshape(nb, ext, 3, nh, hd).transpose(1, 2, 3, 4, 0)
    cw_t = jnp.broadcast_to(conv_w.reshape(4, 3, nh, hd)[..., None], (4, 3, nh, hd, nb))
    a_t = a_b.transpose(1, 2, 0).reshape(s_len, nh, 1, nb)
    b_t = b_b.transpose(1, 2, 0).reshape(s_len, nh, 1, nb)
    alog_t = jnp.broadcast_to(a_log.reshape(nh, 1, 1), (nh, 1, nb))
    dtb_t = jnp.broadcast_to(dt_bias.reshape(nh, 1, 1), (nh, 1, nb))
    gate_t = gate.reshape(nb, s_len, nh, hd).transpose(1, 2, 3, 0)
    gn_t = jnp.broadcast_to(out_norm.reshape(1, hd, 1), (1, hd, nb))
    s_t = state.transpose(1, 2, 3, 0)
    o_t, s_new = pl.pallas_call(
        _sample_gdn_kernel,
        grid=(nh,),
        in_specs=[pl.BlockSpec((ext, 3, None, hd, nb), lambda h: (0, 0, h, 0, 0)),
                  pl.BlockSpec((4, 3, None, hd, nb), lambda h: (0, 0, h, 0, 0)),
                  pl.BlockSpec((s_len, None, 1, nb), lambda h: (0, h, 0, 0)),
                  pl.BlockSpec((s_len, None, 1, nb), lambda h: (0, h, 0, 0)),
                  pl.BlockSpec((None, 1, nb), lambda h: (h, 0, 0)),
                  pl.BlockSpec((None, 1, nb), lambda h: (h, 0, 0)),
                  pl.BlockSpec((s_len, None, hd, nb), lambda h: (0, h, 0, 0)),
                  pl.BlockSpec((None, hd, nb), lambda h: (0, 0, 0)),
                  pl.BlockSpec((None, hd, hd, nb), lambda h: (h, 0, 0, 0))],
        out_specs=[pl.BlockSpec((s_len, None, hd, nb), lambda h: (0, h, 0, 0)),
                   pl.BlockSpec((None, hd, hd, nb), lambda h: (h, 0, 0, 0))],
        out_shape=[jax.ShapeDtypeStruct((s_len, nh, hd, nb), F32), jax.ShapeDtypeStruct((nh, hd, hd, nb), F32)],
        scratch_shapes=[pltpu.VMEM((hd, nb), F32)] * 3,
        compiler_params=_params("parallel"),
        name="sample_gdn",
    )(xe_t, cw_t, a_t, b_t, alog_t, dtb_t, gate_t, gn_t, s_t)
    ob = o_t.transpose(3, 0, 1, 2).reshape(nb * s_len, nh * hd).astype(BF16)
    return ob, s_new.transpose(3, 0, 1, 2)


def _lane_row(v, width=LANES):
    return jnp.zeros((1, width), F32).at[0, :v.shape[0]].set(v)


def kernel(x_prompt, x_sample, cache_win_k, cache_win_v, state_gdn, state_gdn_conv, state_ffn_conv, cache_mem_k,
           cache_mem_v, mem_prompt, norm1_g, w_in, q_norm_a, k_norm_a, conv_b_w, a_log_b, dt_bias_b, out_norm_b,
           mem_norm_g, w_mem_kv, q_norm_c, k_norm_c, w_out, norm2_g, w_up, conv_ffn_w, w_down):
    depth = norm1_g.shape[0]
    assert depth == 1 and x_prompt.shape[0] == 1
    l = 0
    t_p = x_prompt.shape[1]
    nb, s_len = x_sample.shape[:2]
    xp = x_prompt.reshape(t_p, D_MODEL)
    xs = x_sample.reshape(nb * s_len, D_MODEL)

    w_in_p = w_in[l].T.astype(BF16)
    w_out_b = w_out[l].astype(BF16)
    w_up_b = w_up[l].astype(BF16)
    w_down_b = w_down[l].astype(BF16)
    g1 = norm1_g[l].reshape(1, D_MODEL)
    g2 = norm2_g[l].reshape(1, D_MODEL)
    gq = jnp.tile(q_norm_a[l], N_HEADS_A).reshape(1, WIDTH_A)
    gk = jnp.tile(k_norm_a[l], N_HEADS_A).reshape(1, WIDTH_A)
    gqc = jnp.tile(q_norm_c[l], N_HEADS_C).reshape(1, WIDTH_C)
    gkc = jnp.tile(k_norm_c[l], N_HEADS_C).reshape(1, WIDTH_C)
    gn_row = jnp.tile(out_norm_b[l], N_HEADS_B).reshape(1, WIDTH_B)

    rows_s = nb * s_len
    (qa_s,), _, _, (ka_s, va_s, qkv_s, gate_s, qc_s, ab_s) = _inproj(xs, g1, w_in_p, gq, gk, gqc, (1,), rows_s,
                                                                         HEAD_DIM ** -0.5)
    b3 = lambda a: a.reshape(nb, s_len, a.shape[-1])
    heads = lambda a: a.reshape(nb, s_len, -1, HEAD_DIM)
    riders, finish_sample_attn = _sample_attn_riders(
        heads(qa_s), b3(ka_s), b3(va_s), cache_win_k[l], cache_win_v[l], heads(qc_s), cache_mem_k[l], cache_mem_v[l],
        n_parts=2, steps=t_p // ROW_TILE)

    n_keep = min(MAX_WINDOW, t_p)
    q_ds, k_ds, v_ds, (ka, va, qkv, gate, qc, ab) = _inproj(xp, g1, w_in_p, gq, gk, gqc, DILATIONS, n_keep,
                                                            HEAD_DIM ** -0.5 * LOG2E)
    parts = [_attn_band(q_d, k_d, v_d, dil) for q_d, k_d, v_d, dil in zip(q_ds, k_ds, v_ds, DILATIONS)]
    mk, mv = _memkv(mem_prompt[0], mem_norm_g[l].reshape(1, D_MODEL), w_mem_kv[l].astype(BF16), gkc)
    oa, oc = _combine_mem([p[0] for p in parts], [p[1] for p in parts], DILATIONS, qc, mk, mv)
    ob, gdn_p, rider_outs = _gdn_prompt(qkv, ab, gate, conv_b_w[l], _lane_row(a_log_b[l]), _lane_row(dt_bias_b[l]),
                                        gn_row, riders)
    y_p, last_p = _ffn(xp, oa, ob, oc, w_out_b, g2, w_up_b, conv_ffn_w[l], w_down_b, jnp.zeros((8, 2 * D_FF), F32), 1,
                       FFN_ROW_TILE)
    win_k_p = ka.reshape(1, 1, n_keep, N_HEADS_A, HEAD_DIM)
    win_v_p = va.reshape(1, 1, n_keep, N_HEADS_A, HEAD_DIM)
    gconv_p = qkv[t_p - 3:].reshape(1, 1, 3, QKV_B)
    fconv_p = last_p[6:8].reshape(1, 1, 2, 2 * D_FF)

    oa_s, oc_s = finish_sample_attn(rider_outs)
    xe_s = jnp.concatenate([state_gdn_conv[l], b3(qkv_s)], axis=1)
    ab3 = b3(ab_s)
    ob_s, gdn_s = _sample_gdn(xe_s, conv_b_w[l], ab3[..., 0:N_HEADS_B], ab3[..., N_HEADS_B:2 * N_HEADS_B], a_log_b[l],
                              dt_bias_b[l], b3(gate_s), out_norm_b[l], state_gdn[l])
    time_major = lambda a: a.reshape(nb, s_len, a.shape[-1]).transpose(1, 0, 2).reshape(rows_s, a.shape[-1])
    carry_s = state_ffn_conv[l].transpose(1, 0, 2).reshape(2 * nb, 2 * D_FF)
    y_t, last_s = _ffn(time_major(xs), time_major(oa_s), time_major(ob_s), time_major(oc_s), w_out_b, g2, w_up_b,
                       conv_ffn_w[l], w_down_b, carry_s, nb, rows_s)
    y_s = y_t.reshape(s_len, nb, D_MODEL).transpose(1, 0, 2)
    fconv_s = last_s.reshape(2, nb, 2 * D_FF).transpose(1, 0, 2)[None]

    return (y_p.reshape(1, t_p, D_MODEL), y_s,
            win_k_p, win_v_p,
            ka_s.reshape(1, nb, s_len, N_HEADS_A, HEAD_DIM), va_s.reshape(1, nb, s_len, N_HEADS_A, HEAD_DIM),
            gdn_p[None, None], gdn_s[None],
            gconv_p, xe_s[:, -3:][None],
            fconv_p, fconv_s,
            mk.reshape(1, 1, N_MEM, N_HEADS_C, HEAD_DIM), mv.reshape(1, 1, N_MEM, N_HEADS_C, HEAD_DIM))
```

```python
import functools

import numpy as np
import jax
import jax.numpy as jnp
from jax import lax
from jax.experimental import pallas as pl
from jax.experimental.pallas import tpu as pltpu

F32 = jnp.float32
BF16 = jnp.bfloat16
EPS = 1e-6
NEG = -1e30
LOG2E = 1.4426950408889634
SQ_PARTS = 1

D_MODEL = 1024
HEAD_DIM = 64
N_HEADS_A = 8
PATTERNS = ((128, 1), (512, 4), (2048, 16))
DILATIONS = tuple(d for _, d in PATTERNS)
MAX_WINDOW = 2048
N_HEADS_B = 4
N_HEADS_C = 4
N_MEM = 256
D_FF = 2816
WIDTH_A = N_HEADS_A * HEAD_DIM
WIDTH_B = N_HEADS_B * HEAD_DIM
WIDTH_C = N_HEADS_C * HEAD_DIM
QKV_B = 3 * WIDTH_B
CHUNK = 64
NBAND = 128
BAND_BLOCKS = 16
COMBINE_ROW_TILE = 1024
ROW_TILE = 512
FFN_ROW_TILE = 512
FF_TILE = 256
LANES = 128
VMEM_LIMIT = 56 * 1024 * 1024

SEG_QA, SEG_KA, SEG_VA = (0, 512), (512, 1024), (1024, 1536)
SEG_QKV, SEG_GATE, SEG_TAIL = (1536, 2304), (2304, 2560), (2560, 2824)
N_IN = 2824


def _params(*sem):
    return pltpu.CompilerParams(dimension_semantics=sem, vmem_limit_bytes=VMEM_LIMIT)


def _dot(a, b):
    return jnp.dot(a, b, preferred_element_type=F32)


def _dot_nt(a, b):
    return lax.dot_general(a, b, (((1,), (1,)), ((), ())), preferred_element_type=F32)


def _dot_tn(a, b):
    return lax.dot_general(a, b, (((0,), (0,)), ((), ())), preferred_element_type=F32)


def _dot_split(a, b, parts, left=False):
    acc = None
    rem = a
    for _ in range(parts):
        piece = rem.astype(BF16)
        term = _dot(b, piece) if left else _dot(piece, b)
        acc = term if acc is None else acc + term
        rem = rem - piece.astype(F32)
    return acc


def _sigmoid(x):
    return 1.0 / (1.0 + jnp.exp(-x))


def _silu(x):
    return x * _sigmoid(x)


def _softplus(x):
    return jnp.maximum(x, 0.0) + jnp.log1p(jnp.exp(-jnp.abs(x)))


def _group_ones(width, group=HEAD_DIM):
    i = np.arange(width)
    return jnp.asarray((i[:, None] // group) == (i[None, :] // group), BF16)


def _full(shape):
    nd = len(shape)
    return pl.BlockSpec(shape, lambda *_: (0,) * nd)


def _call_with_rider(host_kernel, rider, *, grid, in_specs, out_specs, out_shape, scratch_shapes, args, sem, name):
    if rider is None:
        rider = dict(units=None, in_specs=[], out_specs=[], out_shape=[], args=[])
    n_hi, n_ri, n_ho, n_ro = len(in_specs), len(rider["in_specs"]), len(out_specs), len(rider["out_specs"])

    def body(*refs):
        host_in, refs = refs[:n_hi], refs[n_hi:]
        rider_in, refs = refs[:n_ri], refs[n_ri:]
        host_out, refs = refs[:n_ho], refs[n_ho:]
        rider_out, scratch = refs[:n_ro], refs[n_ro:]
        units = rider["units"](*rider_in, *rider_out) if rider["units"] is not None else []
        host_kernel(units, *host_in, *host_out, *scratch)

    outs = pl.pallas_call(
        body, grid=grid, in_specs=list(in_specs) + rider["in_specs"], out_specs=list(out_specs) + rider["out_specs"],
        out_shape=list(out_shape) + rider["out_shape"], scratch_shapes=scratch_shapes,
        compiler_params=_params(sem), name=name,
    )(*args, *rider["args"])
    return outs[:n_ho], outs[n_ho:]


def _inproj_kernel(dils, scale, x_ref, g1_ref, w_ref, gq_ref, gk_ref, gc_ref, ga_ref, gcc_ref, *refs):
    nd = len(dils)
    q_refs, k_refs, v_refs = refs[0:nd], refs[nd:2 * nd], refs[2 * nd:3 * nd]
    ka_ref, va_ref, qkv_ref, gate_ref, qc_ref, ab_ref, scr_ref = refs[3 * nd:]
    tm = x_ref.shape[0]
    x = x_ref[...]
    ms = jnp.mean(x * x, axis=-1, keepdims=True)
    xn = (x * lax.rsqrt(ms + EPS) * g1_ref[...]).astype(BF16)

    def seg(s):
        return _dot_nt(xn, w_ref[s[0]:s[1], :])

    def head_norm(z, ones_ref, gain):
        ss = _dot_split(z * z, ones_ref[...], SQ_PARTS) * (1.0 / HEAD_DIM)
        return z * lax.rsqrt(ss + EPS) * gain

    def emit(z, out_refs):
        n_tiles = WIDTH_A // LANES
        for c in range(n_tiles):
            scr_ref[c] = z[:, c * LANES:(c + 1) * LANES]
        for d, ref in zip(dils, out_refs):
            if d == 1:
                ref[...] = z.astype(BF16)
            else:
                for r in range(d):
                    for c in range(n_tiles):
                        col = r * WIDTH_A + c * LANES
                        ref[:, col:col + LANES] = scr_ref[c, pl.ds(r, tm // d, stride=d), :].astype(BF16)

    emit(head_norm(seg(SEG_QA), ga_ref, gq_ref[...]) * scale, q_refs)
    ka = head_norm(seg(SEG_KA), ga_ref, gk_ref[...])
    ka_ref[...] = ka
    emit(ka, k_refs)
    va = seg(SEG_VA)
    va_ref[...] = va
    emit(va, v_refs)
    qkv_ref[...] = seg(SEG_QKV)
    gate_ref[...] = seg(SEG_GATE)
    tail = seg(SEG_TAIL)
    ab_ref[...] = tail[:, 0:LANES]
    qc = tail[:, 2 * N_HEADS_B:2 * N_HEADS_B + WIDTH_C]
    qc_ref[...] = (head_norm(qc, gcc_ref, gc_ref[...]) * scale).astype(BF16)


def _inproj(x, g1, w_in_p, gq, gk, gc, dils, keep_rows, scale):
    rows = x.shape[0]
    tm = ROW_TILE
    nt = rows // tm
    skip = nt - keep_rows // tm
    row = lambda w: pl.BlockSpec((tm, w), lambda i: (i, 0))
    tail = pl.BlockSpec((tm, WIDTH_A), lambda i: (jnp.maximum(i - skip, 0), 0))
    dil_specs = [pl.BlockSpec((tm // d, d * WIDTH_A), lambda i: (i, 0)) for d in dils]
    dil_shapes = [jax.ShapeDtypeStruct((rows // d, d * WIDTH_A), BF16) for d in dils]
    outs = [(QKV_B, F32), (WIDTH_B, F32), (WIDTH_C, BF16), (LANES, F32)]
    res = pl.pallas_call(
        functools.partial(_inproj_kernel, dils, scale),
        grid=(nt,),
        in_specs=[row(D_MODEL), _full((1, D_MODEL)), _full((N_IN, D_MODEL)), _full((1, WIDTH_A)),
                  _full((1, WIDTH_A)), _full((1, WIDTH_C)), _full((WIDTH_A, WIDTH_A)), _full((WIDTH_C, WIDTH_C))],
        out_specs=dil_specs * 3 + [tail, tail] + [row(w) for w, _ in outs],
        out_shape=dil_shapes * 3 + [jax.ShapeDtypeStruct((keep_rows, WIDTH_A), F32)] * 2
        + [jax.ShapeDtypeStruct((rows, w), dt) for w, dt in outs],
        scratch_shapes=[pltpu.VMEM((WIDTH_A // LANES, tm, LANES), F32)],
        compiler_params=_params("arbitrary"),
        name="inproj",
    )(x, g1, w_in_p, gq, gk, gc, _group_ones(WIDTH_A), _group_ones(WIDTH_C))
    nd = len(dils)
    return res[0:nd], res[nd:2 * nd], res[2 * nd:3 * nd], res[3 * nd:]


def _alibi_slopes():
    return np.exp2(-8.0 * np.arange(1, N_HEADS_A + 1, dtype=np.float64) / N_HEADS_A)


def _band_bias(dil):
    qi = np.arange(NBAND)[:, None]
    kj = np.arange(2 * NBAND)[None, :]
    delta = qi + NBAND - kj
    in_band = (delta >= 0) & (delta <= NBAND)
    bias = -_alibi_slopes()[:, None, None] * (delta * dil)[None].astype(np.float64)
    general = np.where(in_band[None], bias, NEG)
    first = np.where((in_band & (kj >= NBAND))[None], bias, NEG)
    return jnp.asarray(np.stack([general, first]) * LOG2E, F32)


def _attn_band_kernel(q_ref, kp_ref, kc_ref, vp_ref, vc_ref, bias_ref, o_ref, lse_ref):
    lane = lax.broadcasted_iota(jnp.int32, (NBAND, LANES), 1)
    low = lane < HEAD_DIM
    ones = jnp.ones((2 * NBAND, LANES), BF16)
    for blk in range(q_ref.shape[0] // NBAND):
        first = (pl.program_id(1) == 0).astype(jnp.int32) if blk == 0 else 0
        rows = slice(blk * NBAND, (blk + 1) * NBAND)
        lse = jnp.zeros((NBAND, LANES), F32)
        for pair in range(N_HEADS_A // 2):
            cols = slice(pair * LANES, (pair + 1) * LANES)
            q = q_ref[rows, cols]
            if blk == 0:
                k = jnp.concatenate([kp_ref[:, cols], kc_ref[rows, cols]], axis=0)
                v = jnp.concatenate([vp_ref[:, cols], vc_ref[rows, cols]], axis=0)
            else:
                k = kc_ref[(blk - 1) * NBAND:(blk + 1) * NBAND, cols]
                v = vc_ref[(blk - 1) * NBAND:(blk + 1) * NBAND, cols]
            v_ext = jnp.concatenate([v, ones], axis=1)
            outs = []
            for half in range(2):
                h = 2 * pair + half
                qm = jnp.where(low if half == 0 else ~low, q, jnp.zeros((), BF16))
                s = _dot_nt(qm, k) + bias_ref[first, h]
                m = jnp.max(s, axis=-1, keepdims=True)
                p = jnp.exp2(s - m).astype(BF16)
                r = _dot(p, v_ext)
                outs.append(r[:, :LANES])
                lse = jnp.where(lane == h, m, lse)
                lse = jnp.where(lane == N_HEADS_A + h, r[:, LANES:], lse)
            o_ref[rows, cols] = jnp.where(low, outs[0], outs[1]).astype(BF16)
        lse_ref[rows, :] = lse


def _attn_band(q_d, k_d, v_d, dil):
    rows = q_d.shape[0]
    blocks = min(BAND_BLOCKS, rows // NBAND)
    step = blocks * NBAND
    cur = lambda w: pl.BlockSpec((step, w), lambda r, n: (n, r))
    prev = lambda w: pl.BlockSpec((NBAND, w), lambda r, n: (jnp.maximum(n * blocks - 1, 0), r))
    return pl.pallas_call(
        _attn_band_kernel,
        grid=(dil, rows // step),
        in_specs=[cur(WIDTH_A), prev(WIDTH_A), cur(WIDTH_A), prev(WIDTH_A), cur(WIDTH_A),
                  _full((2, N_HEADS_A, NBAND, 2 * NBAND))],
        out_specs=[cur(WIDTH_A), cur(LANES)],
        out_shape=[jax.ShapeDtypeStruct((rows, dil * WIDTH_A), BF16), jax.ShapeDtypeStruct((rows, dil * LANES), F32)],
        compiler_params=_params("parallel", "arbitrary"),
        name=f"attn_band_d{dil}",
    )(q_d, k_d, k_d, v_d, v_d, _band_bias(dil))


def _memkv_kernel(mem_ref, g_ref, w_ref, gk_ref, ones_ref, mk_ref, mv_ref):
    x = mem_ref[...]
    ms = jnp.mean(x * x, axis=-1, keepdims=True)
    xn = (x * lax.rsqrt(ms + EPS) * g_ref[...]).astype(BF16)
    zk = _dot(xn, w_ref[:, 0:WIDTH_C])
    ss = _dot_split(zk * zk, ones_ref[...], SQ_PARTS) * (1.0 / HEAD_DIM)
    mk_ref[...] = zk * lax.rsqrt(ss + EPS) * gk_ref[...]
    mv_ref[...] = _dot(xn, w_ref[:, WIDTH_C:2 * WIDTH_C])


def _memkv(mem, g, w, gk):
    n = mem.shape[0]
    return pl.pallas_call(
        _memkv_kernel,
        out_shape=[jax.ShapeDtypeStruct((n, WIDTH_C), F32)] * 2,
        compiler_params=pltpu.CompilerParams(vmem_limit_bytes=VMEM_LIMIT),
        name="memkv",
    )(mem, g, w, gk, _group_ones(WIDTH_C))


def _head_spread():
    e = np.zeros((LANES, WIDTH_A), np.float32)
    for h in range(N_HEADS_A):
        e[h, h * HEAD_DIM:(h + 1) * HEAD_DIM] = 1.0
    return jnp.asarray(e, BF16)


def _combine_mem_kernel(dils, *refs):
    nd = len(dils)
    o_refs, l_refs = refs[0:nd], refs[nd:2 * nd]
    qc_ref, mk_ref, mv_ref, e_ref, oa_ref, oc_ref, o_scr, l_scr = refs[2 * nd:]
    tm = oa_ref.shape[0]
    os_, ls = [], []
    for p, d in enumerate(dils):
        if d == 1:
            os_.append(o_refs[p][...].astype(F32))
            ls.append(l_refs[p][...])
            continue
        n_tiles = WIDTH_A // LANES
        for r in range(d):
            for c in range(n_tiles):
                col = r * WIDTH_A + c * LANES
                o_scr[p * n_tiles + c, pl.ds(r, tm // d, stride=d), :] = o_refs[p][:, col:col + LANES].astype(F32)
            l_scr[p, pl.ds(r, tm // d, stride=d), :] = l_refs[p][:, r * LANES:(r + 1) * LANES]
        os_.append(jnp.concatenate([o_scr[p * n_tiles + c] for c in range(n_tiles)], axis=1))
        ls.append(l_scr[p])
    head_lane = lax.broadcasted_iota(jnp.int32, (tm, LANES), 1) < N_HEADS_A
    m = functools.reduce(jnp.maximum, ls)
    es = [jnp.exp2(l - m) for l in ls]
    dens = [pltpu.roll(l, LANES - N_HEADS_A, 1) for l in ls]
    tot = functools.reduce(lambda a, b: a + b, [d * e for d, e in zip(dens, es)])
    tot = jnp.where(head_lane, tot, 1.0)
    acc = None
    for o, e in zip(os_, es):
        term = o * _dot_split(e / tot, e_ref[...], 1)
        acc = term if acc is None else acc + term
    oa_ref[...] = acc.astype(BF16)

    lane = lax.broadcasted_iota(jnp.int32, (tm, LANES), 1)
    low = lane < HEAD_DIM
    ones = jnp.ones((N_MEM, LANES), BF16)
    for pair in range(N_HEADS_C // 2):
        cols = slice(pair * LANES, (pair + 1) * LANES)
        q = qc_ref[:, cols]
        mk = mk_ref[:, cols].astype(BF16)
        v_ext = jnp.concatenate([mv_ref[:, cols].astype(BF16), ones], axis=1)
        outs = []
        for half in range(2):
            qm = jnp.where(low if half == 0 else ~low, q, jnp.zeros((), BF16))
            s = _dot_nt(qm, mk)
            p = jnp.exp2(s - jnp.max(s, axis=-1, keepdims=True)).astype(BF16)
            r = _dot(p, v_ext)
            outs.append(r[:, :LANES] / r[:, LANES:])
        oc_ref[:, cols] = jnp.where(low, outs[0], outs[1]).astype(BF16)


def _combine_mem(os_, lses, dils, qc, mk, mv):
    t = qc.shape[0]
    tm = COMBINE_ROW_TILE
    nd = len(dils)
    row = lambda w: pl.BlockSpec((tm, w), lambda i: (i, 0))
    o_specs = [pl.BlockSpec((tm // d, d * WIDTH_A), lambda i: (i, 0)) for d in dils]
    l_specs = [pl.BlockSpec((tm // d, d * LANES), lambda i: (i, 0)) for d in dils]
    return pl.pallas_call(
        functools.partial(_combine_mem_kernel, dils),
        grid=(t // tm,),
        in_specs=o_specs + l_specs + [row(WIDTH_C), _full((N_MEM, WIDTH_C)), _full((N_MEM, WIDTH_C)),
                                     _full((LANES, WIDTH_A))],
        out_specs=[row(WIDTH_A), row(WIDTH_C)],
        out_shape=[jax.ShapeDtypeStruct((t, WIDTH_A), BF16), jax.ShapeDtypeStruct((t, WIDTH_C), BF16)],
        scratch_shapes=[pltpu.VMEM((nd * WIDTH_A // LANES, tm, LANES), F32), pltpu.VMEM((nd, tm, LANES), F32)],
        compiler_params=_params("parallel"),
        name="combine_mem",
    )(*os_, *lses, qc, mk, mv, _head_spread())


def _head_lane_consts():
    e_g = np.zeros((LANES, WIDTH_B), np.float32)
    e_b = np.zeros((LANES, WIDTH_B), np.float32)
    for h in range(N_HEADS_B):
        e_g[h, h * HEAD_DIM:(h + 1) * HEAD_DIM] = 1.0
        e_b[N_HEADS_B + h, h * HEAD_DIM:(h + 1) * HEAD_DIM] = 1.0
    i = np.arange(CHUNK)
    ltri = (i[None, :] <= i[:, None]).astype(np.float32)
    return jnp.asarray(e_g, BF16), jnp.asarray(e_b, BF16), jnp.asarray(ltri, BF16), jnp.ones((CHUNK, CHUNK), BF16)


def _block_diag(x, mask):
    return jnp.where(mask, jnp.concatenate([x] * N_HEADS_B, axis=0), jnp.zeros((), x.dtype))


def _gdn_prep_kernel(units, x_ref, halo_ref, ab_ref, cw_ref, alog_ref, dtb_ref, ones_ref, eg_ref, eb_ref, ltri_ref, one64_ref,
                     m_ref, attn_ref, qd_ref, kd_ref, vb_ref, kbe_ref, egl_ref, xe_ref, m_scr):
    tm = x_ref.shape[0]
    nc = tm // CHUNK
    halo = jnp.where(pl.program_id(0) > 0, halo_ref[...], 0.0)
    xe_ref[0:8, :] = halo
    xe_ref[8:8 + tm, :] = x_ref[...]
    cw = cw_ref[...]
    y = cw[3:4] * xe_ref[8:8 + tm, :] + cw[2:3] * xe_ref[7:7 + tm, :] + cw[1:2] * xe_ref[6:6 + tm, :] \
        + cw[0:1] * xe_ref[5:5 + tm, :]
    y = _silu(y)

    def l2n(z):
        return z * lax.rsqrt(_dot_split(z * z, ones_ref[...], SQ_PARTS) + EPS)

    q = l2n(y[:, 0:WIDTH_B]) * (HEAD_DIM ** -0.5)
    k = l2n(y[:, WIDTH_B:2 * WIDTH_B])
    v = y[:, 2 * WIDTH_B:3 * WIDTH_B]
    ab = ab_ref[...]
    g = -jnp.exp(alog_ref[...]) * _softplus(ab + dtb_ref[...])
    gx = _dot_split(g, eg_ref[...], 2)
    bx = _dot_split(_sigmoid(ab), eb_ref[...], 2)

    row = lax.broadcasted_iota(jnp.int32, (CHUNK, WIDTH_B), 0)
    col = lax.broadcasted_iota(jnp.int32, (CHUNK, WIDTH_B), 1) % CHUNK
    diag = row == col
    r4 = lax.broadcasted_iota(jnp.int32, (WIDTH_B, WIDTH_B), 0) // HEAD_DIM
    c4 = lax.broadcasted_iota(jnp.int32, (WIDTH_B, WIDTH_B), 1) // HEAD_DIM
    bd_mask = r4 == c4
    for c in range(tm // CHUNK):
        sl = slice(c * CHUNK, (c + 1) * CHUNK)
        gc = _dot_split(gx[sl], ltri_ref[...], 2, left=True)
        gl = gc[CHUNK - 1:CHUNK, :]
        gcj = _dot_split(jnp.where(diag, gc, 0.0), one64_ref[...], 2, left=True)
        dec = jnp.exp(jnp.where(row >= col, gc - gcj, NEG))
        kc, qc, vc, bc = k[sl], q[sl], v[sl], bx[sl]
        kb = kc * bc
        lhs = jnp.concatenate([kb, qc], axis=0).astype(BF16)
        kkqk = _dot_nt(lhs, _block_diag(kc.astype(BF16), bd_mask))
        m_c = jnp.where(row > col, kkqk[0:CHUNK] * dec, 0.0)
        for lt in range(WIDTH_B // LANES):
            m_scr[pl.ds(lt * nc + c, CHUNK, stride=nc * WIDTH_B // LANES), :] = m_c[:, lt * LANES:(lt + 1) * LANES]
        attn_ref[sl, :] = (kkqk[CHUNK:2 * CHUNK] * dec).astype(BF16)
        egc = jnp.exp(gc)
        qd_ref[sl, :] = (qc * egc).astype(BF16)
        kd_ref[sl, :] = (kc * jnp.exp(gl - gc)).astype(BF16)
        vb_ref[sl, :] = (vc * bc).astype(BF16)
        kbe_ref[sl, :] = (kb * egc).astype(BF16)
        egl_ref[c:c + 1, :] = jnp.exp(gl)
        _run_share(units, c, nc, spread=False)
    m_ref[...] = m_scr[...].reshape(m_ref.shape)


def _gdn_prep(qkv, ab, conv_w, alog_row, dtb_row, rider=None):
    t = qkv.shape[0]
    tm = ROW_TILE
    nc = tm // CHUNK
    assert nc == 8
    n_lt = WIDTH_B // LANES
    row = lambda w: pl.BlockSpec((tm, w), lambda i: (i, 0))
    m_spec = pl.BlockSpec((CHUNK, n_lt, nc, LANES), lambda i: (0, 0, i, 0))
    e_g, e_b, ltri, one64 = _head_lane_consts()
    outs = [BF16, BF16, BF16, BF16, BF16]
    return _call_with_rider(
        _gdn_prep_kernel, rider,
        grid=(t // tm,),
        in_specs=[row(QKV_B), pl.BlockSpec((8, QKV_B), lambda i: (jnp.maximum(i * (tm // 8) - 1, 0), 0)), row(LANES),
                  _full((4, QKV_B)), _full((1, LANES)), _full((1, LANES)), _full((WIDTH_B, WIDTH_B)),
                  _full((LANES, WIDTH_B)), _full((LANES, WIDTH_B)), _full((CHUNK, CHUNK)), _full((CHUNK, CHUNK))],
        out_specs=[m_spec] + [row(WIDTH_B)] * 5 + [pl.BlockSpec((nc, WIDTH_B), lambda i: (i, 0))],
        out_shape=[jax.ShapeDtypeStruct((CHUNK, n_lt, t // CHUNK, LANES), F32)]
        + [jax.ShapeDtypeStruct((t, WIDTH_B), dt) for dt in outs]
        + [jax.ShapeDtypeStruct((t // CHUNK, WIDTH_B), F32)],
        scratch_shapes=[pltpu.VMEM((tm + 8, QKV_B), F32), pltpu.VMEM((CHUNK * n_lt * nc, LANES), F32)],
        args=(qkv, qkv, ab, conv_w, alog_row, dtb_row, _group_ones(WIDTH_B), e_g, e_b, ltri, one64),
        sem="parallel", name="gdn_prep")


def _tinv_kernel(m_ref, out_ref, n_ref, mrow_ref, nrow_ref):
    nslots = n_ref.shape[2]
    heads_per_tile = LANES // CHUNK
    blk = pl.program_id(0)

    @pl.when(blk == 0)
    def _():
        n_ref[...] = jnp.zeros_like(n_ref)

    for ii in range(8):
        i = blk * 8 + ii
        for g in range(nslots // N_HEADS_B):
            for lt in range(WIDTH_B // LANES):
                xt = m_ref[ii, lt, g * LANES:(g + 1) * LANES, :].T
                for h2 in range(heads_per_tile):
                    slot = g * N_HEADS_B + lt * heads_per_tile + h2
                    mrow_ref[ii, pl.ds(slot, CHUNK, stride=nslots), :] = xt[h2 * CHUNK:(h2 + 1) * CHUNK, :]
        for kb in range(8):
            k0 = 8 * kb
            init = tuple(mrow_ref[ii, (k0 + kk) * nslots:(k0 + kk + 1) * nslots, :] for kk in range(8))

            def m_at(j, ii=ii):
                return mrow_ref[ii, pl.ds(pl.multiple_of(j * nslots, nslots), nslots), :]

            def body(jj, acc, k0=k0):
                j = k0 + 2 * jj
                m0, m1 = m_at(j), m_at(j + 1)
                return tuple(acc[kk] + m0 * n_ref[j, k0 + kk] + m1 * n_ref[j + 1, k0 + kk] for kk in range(8))

            n_cols = jnp.maximum(i - k0, 0)
            acc = lax.fori_loop(0, n_cols // 2, body, init)
            odd = n_cols % 2 == 1
            j_last = jnp.where(odd, i - 1, 0)
            m_last = jnp.where(odd, m_at(j_last), 0.0)
            acc = tuple(acc[kk] + m_last * n_ref[j_last, k0 + kk] for kk in range(8))
            for kk in range(8):
                val = jnp.where(k0 + kk < i, -acc[kk], 0.0)
                n_ref[i, k0 + kk] = val
                nrow_ref[(k0 + kk) * nslots:(k0 + kk + 1) * nslots, :] = val
        for g in range(nslots // N_HEADS_B):
            for lt in range(WIDTH_B // LANES):
                slot0 = g * N_HEADS_B + lt * heads_per_tile
                parts = [nrow_ref[pl.ds(slot0 + h2, CHUNK, stride=nslots), :] for h2 in range(heads_per_tile)]
                out_ref[ii, lt, g * LANES:(g + 1) * LANES, :] = jnp.concatenate(parts, axis=0).T


def _tinv(m4):
    _, n_lt, nchunk, _ = m4.shape
    assert nchunk % LANES == 0
    nslots = (nchunk // LANES) * N_HEADS_B
    blk = pl.BlockSpec((8, n_lt, nchunk, LANES), lambda i: (i, 0, 0, 0))
    return pl.pallas_call(
        _tinv_kernel,
        grid=(CHUNK // 8,),
        in_specs=[blk],
        out_specs=blk,
        out_shape=jax.ShapeDtypeStruct(m4.shape, F32),
        scratch_shapes=[pltpu.VMEM((CHUNK, CHUNK, nslots, LANES), F32), pltpu.VMEM((8, CHUNK * nslots, LANES), F32),
                        pltpu.VMEM((CHUNK * nslots, LANES), F32)],
        compiler_params=_params("arbitrary"),
        name="gdn_tinv",
    )(m4)


def _gdn_scan_kernel(units, n_ref, vb_ref, kbe_ref, attn_ref, qd_ref, kd_ref, egl_ref, gate_ref, gn_ref, ones_ref,
                     ob_ref, s_out_ref, s_ref, n_scr):
    tm = vb_ref.shape[0]
    nc = tm // CHUNK
    n_lt = WIDTH_B // LANES

    @pl.when(pl.program_id(0) == 0)
    def _():
        s_ref[...] = jnp.zeros_like(s_ref)

    n_scr[...] = n_ref[...].reshape(n_scr.shape)

    row = lax.broadcasted_iota(jnp.int32, (CHUNK, WIDTH_B), 0)
    col = lax.broadcasted_iota(jnp.int32, (CHUNK, WIDTH_B), 1) % CHUNK
    eye = (row == col).astype(F32)
    r4 = lax.broadcasted_iota(jnp.int32, (WIDTH_B, WIDTH_B), 0) // HEAD_DIM
    c4 = lax.broadcasted_iota(jnp.int32, (WIDTH_B, WIDTH_B), 1) // HEAD_DIM
    bd_mask = r4 == c4
    s = s_ref[...]
    for c in range(tm // CHUNK):
        sl = slice(c * CHUNK, (c + 1) * CHUNK)
        n_c = jnp.concatenate([n_scr[pl.ds(lt * nc + c, CHUNK, stride=nc * n_lt), :] for lt in range(n_lt)], axis=1)
        tinv = (n_c + eye).astype(BF16)
        u = _dot(tinv, _block_diag(vb_ref[sl, :], bd_mask))
        w = _dot(tinv, _block_diag(kbe_ref[sl, :], bd_mask))
        lhs = jnp.concatenate([w.astype(BF16), qd_ref[sl, :]], axis=0)
        ws = _dot(lhs, s.astype(BF16))
        v_new = (u - ws[0:CHUNK]).astype(BF16)
        o = ws[CHUNK:2 * CHUNK] + _dot(attn_ref[sl, :], _block_diag(v_new, bd_mask))
        s = s * egl_ref[c:c + 1, :] + jnp.where(bd_mask, _dot_tn(kd_ref[sl, :], v_new), 0.0)
        ms = _dot_split(o * o, ones_ref[...], SQ_PARTS) * (1.0 / HEAD_DIM)
        ob_ref[sl, :] = (o * lax.rsqrt(ms + EPS) * gn_ref[...] * _silu(gate_ref[sl, :])).astype(BF16)
        _run_share(units, c, nc, spread=True)
    s_ref[...] = s
    s_out_ref[...] = s


def _gdn_scan(n4, vb, kbe, attn, qd, kd, egl, gate, gn_row, rider=None):
    t = vb.shape[0]
    tm = ROW_TILE
    nc = tm // CHUNK
    n_lt = WIDTH_B // LANES
    row = lambda w: pl.BlockSpec((tm, w), lambda i: (i, 0))
    return _call_with_rider(
        _gdn_scan_kernel, rider,
        grid=(t // tm,),
        in_specs=[pl.BlockSpec((CHUNK, n_lt, nc, LANES), lambda i: (0, 0, i, 0))] + [row(WIDTH_B)] * 5
        + [pl.BlockSpec((nc, WIDTH_B), lambda i: (i, 0)), row(WIDTH_B), _full((1, WIDTH_B)), _full((WIDTH_B, WIDTH_B))],
        out_specs=[row(WIDTH_B), _full((WIDTH_B, WIDTH_B))],
        out_shape=[jax.ShapeDtypeStruct((t, WIDTH_B), BF16), jax.ShapeDtypeStruct((WIDTH_B, WIDTH_B), F32)],
        scratch_shapes=[pltpu.VMEM((WIDTH_B, WIDTH_B), F32), pltpu.VMEM((CHUNK * n_lt * nc, LANES), F32)],
        args=(n4, vb, kbe, attn, qd, kd, egl, gate, gn_row, _group_ones(WIDTH_B)),
        sem="arbitrary", name="gdn_scan")


def _gdn_prompt(qkv, ab, gate, conv_w, alog_row, dtb_row, gn_row, riders=(None, None)):
    (m4, attn, qd, kd, vb, kbe, egl), ride0 = _gdn_prep(qkv, ab, conv_w, alog_row, dtb_row, riders[0])
    (ob, s_bd), ride1 = _gdn_scan(_tinv(m4), vb, kbe, attn, qd, kd, egl, gate, gn_row, riders[1])
    s_fin = jnp.stack([s_bd[h * HEAD_DIM:(h + 1) * HEAD_DIM, h * HEAD_DIM:(h + 1) * HEAD_DIM] for h in range(N_HEADS_B)])
    return ob, s_fin, (ride0, ride1)


def _ffn_kernel(shift, carry_rows, x_ref, oa_ref, ob_ref, oc_ref, wo_ref, g_ref, wup_ref, cw_ref, wdn_ref, c0_ref,
                y_ref, last_ref, carry_ref, ext_ref, hid_ref):
    tm = x_ref.shape[0]

    @pl.when(pl.program_id(0) == 0)
    def _():
        carry_ref[...] = c0_ref[...]

    mix = _dot(oa_ref[...], wo_ref[0:WIDTH_A, :])
    mix = mix + _dot(ob_ref[...], wo_ref[WIDTH_A:WIDTH_A + WIDTH_B, :])
    mix = mix + _dot(oc_ref[...], wo_ref[WIDTH_A + WIDTH_B:, :])
    x = x_ref[...] + mix
    ms = jnp.mean(x * x, axis=-1, keepdims=True)
    xn = (x * lax.rsqrt(ms + EPS) * g_ref[...]).astype(BF16)
    tf = FF_TILE
    base = carry_rows
    for j in range(D_FF // tf):
        halves = []
        for half in range(2):
            cols = slice(half * D_FF + j * tf, half * D_FF + (j + 1) * tf)
            up = _dot(xn, wup_ref[:, cols])
            ext = ext_ref.at[(j % 2) * 2 + half]
            ext[0:base, :] = carry_ref[:, cols]
            ext[base:base + tm, :] = up
            carry_ref[:, cols] = ext[tm:tm + base, :]
            cw = cw_ref[:, cols]
            halves.append(cw[0:1] * ext[base - 2 * shift:base - 2 * shift + tm, :]
                          + cw[1:2] * ext[base - shift:base - shift + tm, :] + cw[2:3] * up)
        hid_ref[:, j * tf:(j + 1) * tf] = (_silu(halves[0]) * halves[1]).astype(BF16)
    y_ref[...] = x + _dot(hid_ref[...], wdn_ref[...])
    last_ref[...] = carry_ref[...]


def _ffn(x, oa, ob, oc, w_out, g, w_up, conv_w, w_down, carry0, shift, tm):
    rows = x.shape[0]
    carry_rows = carry0.shape[0]
    row = lambda w: pl.BlockSpec((tm, w), lambda i: (i, 0))
    once = lambda shape: pl.BlockSpec(shape, lambda i: (0,) * len(shape), pipeline_mode=pl.Buffered(1))
    return pl.pallas_call(
        functools.partial(_ffn_kernel, shift, carry_rows),
        grid=(rows // tm,),
        in_specs=[row(D_MODEL), row(WIDTH_A), row(WIDTH_B), row(WIDTH_C), once((D_MODEL, D_MODEL)),
                  once((1, D_MODEL)), once((D_MODEL, 2 * D_FF)), once((3, 2 * D_FF)),
                  once((D_FF, D_MODEL)), once((carry_rows, 2 * D_FF))],
        out_specs=[row(D_MODEL), _full((carry_rows, 2 * D_FF))],
        out_shape=[jax.ShapeDtypeStruct((rows, D_MODEL), F32), jax.ShapeDtypeStruct((carry_rows, 2 * D_FF), F32)],
        scratch_shapes=[pltpu.VMEM((carry_rows, 2 * D_FF), F32), pltpu.VMEM((4, tm + carry_rows, FF_TILE), F32),
                        pltpu.VMEM((tm, D_FF), BF16)],
        compiler_params=_params("arbitrary"),
        name=f"ffn_shift{shift}",
    )(x, oa, ob, oc, w_out, g, w_up, conv_w, w_down, carry0)


PAIR_ROWS = 16
NEW_ROWS = 16


def _sample_bias(wbuf, s_len):
    def table(key_pos):
        q_pos = wbuf + np.arange(s_len)
        dist = q_pos[:, None] - key_pos[None, :]
        mult = np.zeros(dist.shape)
        for window, dil in PATTERNS:
            mult += (dist >= 0) & (dist <= window) & (dist % dil == 0)
        logm = np.log(np.maximum(mult, 1))
        bias = -_alibi_slopes()[:, None, None] * np.maximum(dist, 0)[None] + logm[None]
        bias = np.where(mult[None] > 0, bias, NEG)
        out = np.zeros((N_HEADS_A, PAIR_ROWS // 2, key_pos.shape[0]))
        out[:, :s_len] = bias
        return out.reshape(N_HEADS_A // 2, PAIR_ROWS, -1)
    main = table(np.arange(wbuf))
    new = table(np.concatenate([wbuf + np.arange(s_len), np.full(NEW_ROWS - s_len, 10 ** 6)]))
    new[:, :, s_len:] = NEG
    return jnp.asarray(main, F32), jnp.asarray(new, F32)


def _sample_attn_units(q_ref, kn_ref, vn_ref, kt_ref, vt_ref, bm_ref, bn_ref, qc_ref, mkt_ref, mvt_ref, oa_ref, oc_ref):
    def times_values(p, vt):
        return _dot_nt(p.astype(BF16), vt.astype(BF16))

    def window(b):
        scores = []
        for pair in range(N_HEADS_A // 2):
            rows = slice(pair * LANES, (pair + 1) * LANES)
            q = q_ref[b, pair]
            scores.append((_dot(q, kt_ref[b, rows, :].astype(BF16)) + bm_ref[pair],
                           _dot_nt(q, kn_ref[b, :, rows].astype(BF16)) + bn_ref[pair]))
        for pair, (s_main, s_new) in enumerate(scores):
            rows = slice(pair * LANES, (pair + 1) * LANES)
            m = jnp.maximum(jnp.max(s_main, axis=-1, keepdims=True), jnp.max(s_new, axis=-1, keepdims=True))
            p_main = jnp.exp(s_main - m)
            p_new = jnp.exp(s_new - m)
            den = jnp.sum(p_main, axis=-1, keepdims=True) + jnp.sum(p_new, axis=-1, keepdims=True)
            num = times_values(p_main, vt_ref[b, rows, :])
            num = num + _dot(p_new.astype(BF16), vn_ref[b, :, rows].astype(BF16))
            oa_ref[b, pair] = num / den

    def memory(b):
        for pair in range(N_HEADS_C // 2):
            rows = slice(pair * LANES, (pair + 1) * LANES)
            q = qc_ref[b, pair]
            s = _dot(q, mkt_ref[b, rows, :].astype(BF16))
            p = jnp.exp(s - jnp.max(s, axis=-1, keepdims=True))
            oc_ref[b, pair] = times_values(p, mvt_ref[b, rows, :]) / jnp.sum(p, axis=-1, keepdims=True)

    units = []
    for b in range(q_ref.shape[0]):
        units += [functools.partial(window, b), functools.partial(memory, b)]
    return units


def _run_share(units, stage, n_stages, spread):
    lo, hi = (stage * len(units) // n_stages, (stage + 1) * len(units) // n_stages) if spread else \
        ((0, len(units)) if stage == n_stages - 1 else (0, 0))
    for u in units[lo:hi]:
        u()


def _pair_rows(q, s_len):
    nb, _, nh, hd = q.shape
    q = q.reshape(nb, s_len, nh // 2, 2, hd).transpose(0, 2, 3, 1, 4)
    q = jnp.pad(q, ((0, 0), (0, 0), (0, 0), (0, PAIR_ROWS // 2 - s_len), (0, 0)))
    own = jnp.eye(2, dtype=q.dtype)[None, None, :, None, :, None]
    return (q[:, :, :, :, None, :] * own).reshape(nb, nh // 2, PAIR_ROWS, 2 * hd)


def _unpair_rows(o, s_len):
    nb, npair = o.shape[:2]
    o = o.reshape(nb, npair, 2, PAIR_ROWS // 2, 2, HEAD_DIM)
    o = jnp.stack([o[:, :, half, :s_len, half, :] for half in range(2)], axis=2)
    return o.transpose(0, 3, 1, 2, 4).reshape(nb, s_len, npair * 2 * HEAD_DIM)


def _sample_attn_riders(qa, ka, va, cache_k, cache_v, qc, mem_k, mem_v, n_parts, steps):
    nb, s_len = qa.shape[:2]
    wbuf = cache_k.shape[1]
    n_mem = mem_k.shape[1]
    assert s_len <= PAIR_ROWS // 2 and nb % (n_parts * steps) == 0
    g = nb // (n_parts * steps)
    bm, bn = _sample_bias(wbuf, s_len)
    transposed = lambda a: a.transpose(0, 2, 3, 1).reshape(nb, a.shape[2] * a.shape[3], a.shape[1])
    pad_new = lambda a: jnp.pad(a, ((0, 0), (0, NEW_ROWS - s_len), (0, 0)))
    args = [_pair_rows(qa, s_len), pad_new(ka), pad_new(va), transposed(cache_k), transposed(cache_v), bm, bn,
            _pair_rows(qc, s_len), transposed(mem_k), transposed(mem_v)]
    riders = []
    for part in range(n_parts):
        per_b = lambda shape, part=part: pl.BlockSpec((g,) + shape, lambda i: (i + part * steps,) + (0,) * len(shape))
        out_b = lambda shape: pl.BlockSpec((g,) + shape, lambda i: (i,) + (0,) * len(shape))
        riders.append(dict(
            units=_sample_attn_units,
            in_specs=[per_b((N_HEADS_A // 2, PAIR_ROWS, LANES)), per_b((NEW_ROWS, WIDTH_A)), per_b((NEW_ROWS, WIDTH_A)),
                      per_b((WIDTH_A, wbuf)), per_b((WIDTH_A, wbuf)), _full(bm.shape), _full(bn.shape),
                      per_b((N_HEADS_C // 2, PAIR_ROWS, LANES)), per_b((WIDTH_C, n_mem)), per_b((WIDTH_C, n_mem))],
            out_specs=[out_b((N_HEADS_A // 2, PAIR_ROWS, LANES)), out_b((N_HEADS_C // 2, PAIR_ROWS, LANES))],
            out_shape=[jax.ShapeDtypeStruct((g * steps, N_HEADS_A // 2, PAIR_ROWS, LANES), F32),
                       jax.ShapeDtypeStruct((g * steps, N_HEADS_C // 2, PAIR_ROWS, LANES), F32)],
            args=args))

    def finish(rider_outs):
        oa = jnp.concatenate([o[0] for o in rider_outs], axis=0)
        oc = jnp.concatenate([o[1] for o in rider_outs], axis=0)
        return _unpair_rows(oa, s_len).astype(BF16), _unpair_rows(oc, s_len).astype(BF16)

    return riders, finish


def _sample_gdn_kernel(xe_ref, cw_ref, a_ref, b_ref, alog_ref, dtb_ref, gate_ref, gn_ref, s_ref, o_ref, s_out_ref,
                       q_scr, k_scr, v_scr):
    s_len = o_ref.shape[0]
    s_out_ref[...] = s_ref[...]
    for t in range(s_len):
        for part, scr in enumerate((q_scr, k_scr, v_scr)):
            y = sum(cw_ref[j, part] * xe_ref[t + j, part] for j in range(4))
            y = _silu(y)
            if part < 2:
                y = y * lax.rsqrt(jnp.sum(y * y, axis=0, keepdims=True) + EPS)
            if part == 0:
                y = y * (HEAD_DIM ** -0.5)
            scr[...] = y
        decay = jnp.exp(-jnp.exp(alog_ref[...]) * _softplus(a_ref[t] + dtb_ref[...]))
        beta = _sigmoid(b_ref[t])

        def read_body(dk, r):
            return r + k_scr[pl.ds(dk, 1), :] * s_out_ref[dk]

        r = lax.fori_loop(0, HEAD_DIM, read_body, jnp.zeros(v_scr.shape, F32), unroll=8)
        v_new = beta * (v_scr[...] - decay * r)

        def write_body(dk, o):
            s_new = decay * s_out_ref[dk] + k_scr[pl.ds(dk, 1), :] * v_new
            s_out_ref[dk] = s_new
            return o + q_scr[pl.ds(dk, 1), :] * s_new

        o = lax.fori_loop(0, HEAD_DIM, write_body, jnp.zeros(v_scr.shape, F32), unroll=8)
        ms = jnp.mean(o * o, axis=0, keepdims=True)
        o_ref[t] = o * lax.rsqrt(ms + EPS) * gn_ref[...] * _silu(gate_ref[t])


def _sample_gdn(xe, conv_w, a_b, b_b, a_log, dt_bias, gate, out_norm, state):
    nb, ext, _ = xe.shape
    s_len = ext - 3
    nh, hd = N_HEADS_B, HEAD_DIM
    xe_t = xe.reshape(nb, ext, 3, nh, hd).transpose(1, 2, 3, 4, 0)
    cw_t = jnp.broadcast_to(conv_w.reshape(4, 3, nh, hd)[..., None], (4, 3, nh, hd, nb))
    a_t = a_b.transpose(1, 2, 0).reshape(s_len, nh, 1, nb)
    b_t = b_b.transpose(1, 2, 0).reshape(s_len, nh, 1, nb)
    alog_t = jnp.broadcast_to(a_log.reshape(nh, 1, 1), (nh, 1, nb))
    dtb_t = jnp.broadcast_to(dt_bias.reshape(nh, 1, 1), (nh, 1, nb))
    gate_t = gate.reshape(nb, s_len, nh, hd).transpose(1, 2, 3, 0)
    gn_t = jnp.broadcast_to(out_norm.reshape(1, hd, 1), (1, hd, nb))
    s_t = state.transpose(1, 2, 3, 0)
    o_t, s_new = pl.pallas_call(
        _sample_gdn_kernel,
        grid=(nh,),
        in_specs=[pl.BlockSpec((ext, 3, None, hd, nb), lambda h: (0, 0, h, 0, 0)),
                  pl.BlockSpec((4, 3, None, hd, nb), lambda h: (0, 0, h, 0, 0)),
                  pl.BlockSpec((s_len, None, 1, nb), lambda h: (0, h, 0, 0)),
                  pl.BlockSpec((s_len, None, 1, nb), lambda h: (0, h, 0, 0)),
                  pl.BlockSpec((None, 1, nb), lambda h: (h, 0, 0)),
                  pl.BlockSpec((None, 1, nb), lambda h: (h, 0, 0)),
                  pl.BlockSpec((s_len, None, hd, nb), lambda h: (0, h, 0, 0)),
                  pl.BlockSpec((None, hd, nb), lambda h: (0, 0, 0)),
                  pl.BlockSpec((None, hd, hd, nb), lambda h: (h, 0, 0, 0))],
        out_specs=[pl.BlockSpec((s_len, None, hd, nb), lambda h: (0, h, 0, 0)),
                   pl.BlockSpec((None, hd, hd, nb), lambda h: (h, 0, 0, 0))],
        out_shape=[jax.ShapeDtypeStruct((s_len, nh, hd, nb), F32), jax.ShapeDtypeStruct((nh, hd, hd, nb), F32)],
        scratch_shapes=[pltpu.VMEM((hd, nb), F32)] * 3,
        compiler_params=_params("parallel"),
        name="sample_gdn",
    )(xe_t, cw_t, a_t, b_t, alog_t, dtb_t, gate_t, gn_t, s_t)
    ob = o_t.transpose(3, 0, 1, 2).reshape(nb * s_len, nh * hd).astype(BF16)
    return ob, s_new.transpose(3, 0, 1, 2)


def _lane_row(v, width=LANES):
    return jnp.zeros((1, width), F32).at[0, :v.shape[0]].set(v)


def kernel(x_prompt, x_sample, cache_win_k, cache_win_v, state_gdn, state_gdn_conv, state_ffn_conv, cache_mem_k,
           cache_mem_v, mem_prompt, norm1_g, w_in, q_norm_a, k_norm_a, conv_b_w, a_log_b, dt_bias_b, out_norm_b,
           mem_norm_g, w_mem_kv, q_norm_c, k_norm_c, w_out, norm2_g, w_up, conv_ffn_w, w_down):
    depth = norm1_g.shape[0]
    assert depth == 1 and x_prompt.shape[0] == 1
    l = 0
    t_p = x_prompt.shape[1]
    nb, s_len = x_sample.shape[:2]
    xp = x_prompt.reshape(t_p, D_MODEL)
    xs = x_sample.reshape(nb * s_len, D_MODEL)

    w_in_p = w_in[l].T.astype(BF16)
    w_out_b = w_out[l].astype(BF16)
    w_up_b = w_up[l].astype(BF16)
    w_down_b = w_down[l].astype(BF16)
    g1 = norm1_g[l].reshape(1, D_MODEL)
    g2 = norm2_g[l].reshape(1, D_MODEL)
    gq = jnp.tile(q_norm_a[l], N_HEADS_A).reshape(1, WIDTH_A)
    gk = jnp.tile(k_norm_a[l], N_HEADS_A).reshape(1, WIDTH_A)
    gqc = jnp.tile(q_norm_c[l], N_HEADS_C).reshape(1, WIDTH_C)
    gkc = jnp.tile(k_norm_c[l], N_HEADS_C).reshape(1, WIDTH_C)
    gn_row = jnp.tile(out_norm_b[l], N_HEADS_B).reshape(1, WIDTH_B)

    rows_s = nb * s_len
    (qa_s,), _, _, (ka_s, va_s, qkv_s, gate_s, qc_s, ab_s) = _inproj(xs, g1, w_in_p, gq, gk, gqc, (1,), rows_s,
                                                                         HEAD_DIM ** -0.5)
    b3 = lambda a: a.reshape(nb, s_len, a.shape[-1])
    heads = lambda a: a.reshape(nb, s_len, -1, HEAD_DIM)
    riders, finish_sample_attn = _sample_attn_riders(
        heads(qa_s), b3(ka_s), b3(va_s), cache_win_k[l], cache_win_v[l], heads(qc_s), cache_mem_k[l], cache_mem_v[l],
        n_parts=2, steps=t_p // ROW_TILE)

    n_keep = min(MAX_WINDOW, t_p)
    q_ds, k_ds, v_ds, (ka, va, qkv, gate, qc, ab) = _inproj(xp, g1, w_in_p, gq, gk, gqc, DILATIONS, n_keep,
                                                            HEAD_DIM ** -0.5 * LOG2E)
    parts = [_attn_band(q_d, k_d, v_d, dil) for q_d, k_d, v_d, dil in zip(q_ds, k_ds, v_ds, DILATIONS)]
    mk, mv = _memkv(mem_prompt[0], mem_norm_g[l].reshape(1, D_MODEL), w_mem_kv[l].astype(BF16), gkc)
    oa, oc = _combine_mem([p[0] for p in parts], [p[1] for p in parts], DILATIONS, qc, mk, mv)
    ob, gdn_p, rider_outs = _gdn_prompt(qkv, ab, gate, conv_b_w[l], _lane_row(a_log_b[l]), _lane_row(dt_bias_b[l]),
                                        gn_row, riders)
    y_p, last_p = _ffn(xp, oa, ob, oc, w_out_b, g2, w_up_b, conv_ffn_w[l], w_down_b, jnp.zeros((8, 2 * D_FF), F32), 1,
                       FFN_ROW_TILE)
    win_k_p = ka.reshape(1, 1, n_keep, N_HEADS_A, HEAD_DIM)
    win_v_p = va.reshape(1, 1, n_keep, N_HEADS_A, HEAD_DIM)
    gconv_p = qkv[t_p - 3:].reshape(1, 1, 3, QKV_B)
    fconv_p = last_p[6:8].reshape(1, 1, 2, 2 * D_FF)

    oa_s, oc_s = finish_sample_attn(rider_outs)
    xe_s = jnp.concatenate([state_gdn_conv[l], b3(qkv_s)], axis=1)
    ab3 = b3(ab_s)
    ob_s, gdn_s = _sample_gdn(xe_s, conv_b_w[l], ab3[..., 0:N_HEADS_B], ab3[..., N_HEADS_B:2 * N_HEADS_B], a_log_b[l],
                              dt_bias_b[l], b3(gate_s), out_norm_b[l], state_gdn[l])
    time_major = lambda a: a.reshape(nb, s_len, a.shape[-1]).transpose(1, 0, 2).reshape(rows_s, a.shape[-1])
    carry_s = state_ffn_conv[l].transpose(1, 0, 2).reshape(2 * nb, 2 * D_FF)
    y_t, last_s = _ffn(time_major(xs), time_major(oa_s), time_major(ob_s), time_major(oc_s), w_out_b, g2, w_up_b,
                       conv_ffn_w[l], w_down_b, carry_s, nb, rows_s)
    y_s = y_t.reshape(s_len, nb, D_MODEL).transpose(1, 0, 2)
    fconv_s = last_s.reshape(2, nb, 2 * D_FF).transpose(1, 0, 2)[None]

    return (y_p.reshape(1, t_p, D_MODEL), y_s,
            win_k_p, win_v_p,
            ka_s.reshape(1, nb, s_len, N_HEADS_A, HEAD_DIM), va_s.reshape(1, nb, s_len, N_HEADS_A, HEAD_DIM),
            gdn_p[None, None], gdn_s[None],
            gconv_p, xe_s[:, -3:][None],
            fconv_p, fconv_s,
            mk.reshape(1, 1, N_MEM, N_HEADS_C, HEAD_DIM), mv.reshape(1, 1, N_MEM, N_HEADS_C, HEAD_DIM))
```

```python
import functools

import numpy as np
import jax
import jax.numpy as jnp
from jax import lax
from jax.experimental import pallas as pl
from jax.experimental.pallas import tpu as pltpu

F32 = jnp.float32
BF16 = jnp.bfloat16
EPS = 1e-6
NEG = -1e30
LOG2E = 1.4426950408889634
SQ_PARTS = 1

D_MODEL = 1024
HEAD_DIM = 64
N_HEADS_A = 8
PATTERNS = ((128, 1), (512, 4), (2048, 16))
DILATIONS = tuple(d for _, d in PATTERNS)
MAX_WINDOW = 2048
N_HEADS_B = 4
N_HEADS_C = 4
N_MEM = 256
D_FF = 2816
WIDTH_A = N_HEADS_A * HEAD_DIM
WIDTH_B = N_HEADS_B * HEAD_DIM
WIDTH_C = N_HEADS_C * HEAD_DIM
QKV_B = 3 * WIDTH_B
CHUNK = 64
NBAND = 128
BAND_BLOCKS = 16
COMBINE_ROW_TILE = 1024
ROW_TILE = 512
FFN_ROW_TILE = 512
FF_TILE = 512
LANES = 128
VMEM_LIMIT = 56 * 1024 * 1024

SEG_QA, SEG_KA, SEG_VA = (0, 512), (512, 1024), (1024, 1536)
SEG_QKV, SEG_GATE, SEG_TAIL = (1536, 2304), (2304, 2560), (2560, 2824)
N_IN = 2824


def _params(*sem):
    return pltpu.CompilerParams(dimension_semantics=sem, vmem_limit_bytes=VMEM_LIMIT)


def _dot(a, b):
    return jnp.dot(a, b, preferred_element_type=F32)


def _dot_nt(a, b):
    return lax.dot_general(a, b, (((1,), (1,)), ((), ())), preferred_element_type=F32)


def _dot_tn(a, b):
    return lax.dot_general(a, b, (((0,), (0,)), ((), ())), preferred_element_type=F32)


def _dot_split(a, b, parts, left=False):
    acc = None
    rem = a
    for _ in range(parts):
        piece = rem.astype(BF16)
        term = _dot(b, piece) if left else _dot(piece, b)
        acc = term if acc is None else acc + term
        rem = rem - piece.astype(F32)
    return acc


def _sigmoid(x):
    return 1.0 / (1.0 + jnp.exp(-x))


def _silu(x):
    return x * _sigmoid(x)


def _softplus(x):
    return jnp.maximum(x, 0.0) + jnp.log1p(jnp.exp(-jnp.abs(x)))


def _group_ones(width, group=HEAD_DIM):
    i = np.arange(width)
    return jnp.asarray((i[:, None] // group) == (i[None, :] // group), BF16)


def _full(shape):
    nd = len(shape)
    return pl.BlockSpec(shape, lambda *_: (0,) * nd)


def _call_with_rider(host_kernel, rider, *, grid, in_specs, out_specs, out_shape, scratch_shapes, args, sem, name):
    if rider is None:
        rider = dict(units=None, in_specs=[], out_specs=[], out_shape=[], args=[])
    n_hi, n_ri, n_ho, n_ro = len(in_specs), len(rider["in_specs"]), len(out_specs), len(rider["out_specs"])

    def body(*refs):
        host_in, refs = refs[:n_hi], refs[n_hi:]
        rider_in, refs = refs[:n_ri], refs[n_ri:]
        host_out, refs = refs[:n_ho], refs[n_ho:]
        rider_out, scratch = refs[:n_ro], refs[n_ro:]
        units = rider["units"](*rider_in, *rider_out) if rider["units"] is not None else []
        host_kernel(units, *host_in, *host_out, *scratch)

    outs = pl.pallas_call(
        body, grid=grid, in_specs=list(in_specs) + rider["in_specs"], out_specs=list(out_specs) + rider["out_specs"],
        out_shape=list(out_shape) + rider["out_shape"], scratch_shapes=scratch_shapes,
        compiler_params=_params(sem), name=name,
    )(*args, *rider["args"])
    return outs[:n_ho], outs[n_ho:]


def _inproj_kernel(dils, scale, x_ref, g1_ref, w_ref, gq_ref, gk_ref, gc_ref, ga_ref, gcc_ref, *refs):
    nd = len(dils)
    q_refs, k_refs, v_refs = refs[0:nd], refs[nd:2 * nd], refs[2 * nd:3 * nd]
    ka_ref, va_ref, qkv_ref, gate_ref, qc_ref, ab_ref, scr_ref = refs[3 * nd:]
    tm = x_ref.shape[0]
    x = x_ref[...]
    ms = jnp.mean(x * x, axis=-1, keepdims=True)
    xn = (x * lax.rsqrt(ms + EPS) * g1_ref[...]).astype(BF16)

    def seg(s):
        return _dot_nt(xn, w_ref[s[0]:s[1], :])

    def head_norm(z, ones_ref, gain):
        ss = _dot_split(z * z, ones_ref[...], SQ_PARTS) * (1.0 / HEAD_DIM)
        return z * lax.rsqrt(ss + EPS) * gain

    def emit(z, out_refs):
        n_tiles = WIDTH_A // LANES
        for c in range(n_tiles):
            scr_ref[c] = z[:, c * LANES:(c + 1) * LANES]
        for d, ref in zip(dils, out_refs):
            if d == 1:
                ref[...] = z.astype(BF16)
            else:
                for r in range(d):
                    for c in range(n_tiles):
                        col = r * WIDTH_A + c * LANES
                        ref[:, col:col + LANES] = scr_ref[c, pl.ds(r, tm // d, stride=d), :].astype(BF16)

    emit(head_norm(seg(SEG_QA), ga_ref, gq_ref[...]) * scale, q_refs)
    ka = head_norm(seg(SEG_KA), ga_ref, gk_ref[...])
    ka_ref[...] = ka
    emit(ka, k_refs)
    va = seg(SEG_VA)
    va_ref[...] = va
    emit(va, v_refs)
    qkv_ref[...] = seg(SEG_QKV)
    gate_ref[...] = seg(SEG_GATE)
    tail = seg(SEG_TAIL)
    ab_ref[...] = tail[:, 0:LANES]
    qc = tail[:, 2 * N_HEADS_B:2 * N_HEADS_B + WIDTH_C]
    qc_ref[...] = (head_norm(qc, gcc_ref, gc_ref[...]) * scale).astype(BF16)


def _inproj(x, g1, w_in_p, gq, gk, gc, dils, keep_rows, scale):
    rows = x.shape[0]
    tm = ROW_TILE
    nt = rows // tm
    skip = nt - keep_rows // tm
    row = lambda w: pl.BlockSpec((tm, w), lambda i: (i, 0))
    tail = pl.BlockSpec((tm, WIDTH_A), lambda i: (jnp.maximum(i - skip, 0), 0))
    dil_specs = [pl.BlockSpec((tm // d, d * WIDTH_A), lambda i: (i, 0)) for d in dils]
    dil_shapes = [jax.ShapeDtypeStruct((rows // d, d * WIDTH_A), BF16) for d in dils]
    outs = [(QKV_B, F32), (WIDTH_B, F32), (WIDTH_C, BF16), (LANES, F32)]
    res = pl.pallas_call(
        functools.partial(_inproj_kernel, dils, scale),
        grid=(nt,),
        in_specs=[row(D_MODEL), _full((1, D_MODEL)), _full((N_IN, D_MODEL)), _full((1, WIDTH_A)),
                  _full((1, WIDTH_A)), _full((1, WIDTH_C)), _full((WIDTH_A, WIDTH_A)), _full((WIDTH_C, WIDTH_C))],
        out_specs=dil_specs * 3 + [tail, tail] + [row(w) for w, _ in outs],
        out_shape=dil_shapes * 3 + [jax.ShapeDtypeStruct((keep_rows, WIDTH_A), F32)] * 2
        + [jax.ShapeDtypeStruct((rows, w), dt) for w, dt in outs],
        scratch_shapes=[pltpu.VMEM((WIDTH_A // LANES, tm, LANES), F32)],
        compiler_params=_params("arbitrary"),
        name="inproj",
    )(x, g1, w_in_p, gq, gk, gc, _group_ones(WIDTH_A), _group_ones(WIDTH_C))
    nd = len(dils)
    return res[0:nd], res[nd:2 * nd], res[2 * nd:3 * nd], res[3 * nd:]


def _alibi_slopes():
    return np.exp2(-8.0 * np.arange(1, N_HEADS_A + 1, dtype=np.float64) / N_HEADS_A)


def _band_bias(dil):
    qi = np.arange(NBAND)[:, None]
    kj = np.arange(2 * NBAND)[None, :]
    delta = qi + NBAND - kj
    in_band = (delta >= 0) & (delta <= NBAND)
    bias = -_alibi_slopes()[:, None, None] * (delta * dil)[None].astype(np.float64)
    general = np.where(in_band[None], bias, NEG)
    first = np.where((in_band & (kj >= NBAND))[None], bias, NEG)
    return jnp.asarray(np.stack([general, first]) * LOG2E, F32)


def _attn_band_kernel(q_ref, kp_ref, kc_ref, vp_ref, vc_ref, bias_ref, o_ref, lse_ref):
    lane = lax.broadcasted_iota(jnp.int32, (NBAND, LANES), 1)
    low = lane < HEAD_DIM
    ones = jnp.ones((2 * NBAND, LANES), BF16)
    for blk in range(q_ref.shape[0] // NBAND):
        first = (pl.program_id(1) == 0).astype(jnp.int32) if blk == 0 else 0
        rows = slice(blk * NBAND, (blk + 1) * NBAND)
        lse = jnp.zeros((NBAND, LANES), F32)
        for pair in range(N_HEADS_A // 2):
            cols = slice(pair * LANES, (pair + 1) * LANES)
            q = q_ref[rows, cols]
            if blk == 0:
                k = jnp.concatenate([kp_ref[:, cols], kc_ref[rows, cols]], axis=0)
                v = jnp.concatenate([vp_ref[:, cols], vc_ref[rows, cols]], axis=0)
            else:
                k = kc_ref[(blk - 1) * NBAND:(blk + 1) * NBAND, cols]
                v = vc_ref[(blk - 1) * NBAND:(blk + 1) * NBAND, cols]
            v_ext = jnp.concatenate([v, ones], axis=1)
            outs = []
            for half in range(2):
                h = 2 * pair + half
                qm = jnp.where(low if half == 0 else ~low, q, jnp.zeros((), BF16))
                s = _dot_nt(qm, k) + bias_ref[first, h]
                m = jnp.max(s, axis=-1, keepdims=True)
                p = jnp.exp2(s - m).astype(BF16)
                r = _dot(p, v_ext)
                outs.append(r[:, :LANES])
                lse = jnp.where(lane == h, m, lse)
                lse = jnp.where(lane == N_HEADS_A + h, r[:, LANES:], lse)
            o_ref[rows, cols] = jnp.where(low, outs[0], outs[1]).astype(BF16)
        lse_ref[rows, :] = lse


def _attn_band(q_d, k_d, v_d, dil):
    rows = q_d.shape[0]
    blocks = min(BAND_BLOCKS, rows // NBAND)
    step = blocks * NBAND
    cur = lambda w: pl.BlockSpec((step, w), lambda r, n: (n, r))
    prev = lambda w: pl.BlockSpec((NBAND, w), lambda r, n: (jnp.maximum(n * blocks - 1, 0), r))
    return pl.pallas_call(
        _attn_band_kernel,
        grid=(dil, rows // step),
        in_specs=[cur(WIDTH_A), prev(WIDTH_A), cur(WIDTH_A), prev(WIDTH_A), cur(WIDTH_A),
                  _full((2, N_HEADS_A, NBAND, 2 * NBAND))],
        out_specs=[cur(WIDTH_A), cur(LANES)],
        out_shape=[jax.ShapeDtypeStruct((rows, dil * WIDTH_A), BF16), jax.ShapeDtypeStruct((rows, dil * LANES), F32)],
        compiler_params=_params("parallel", "arbitrary"),
        name=f"attn_band_d{dil}",
    )(q_d, k_d, k_d, v_d, v_d, _band_bias(dil))


def _memkv_kernel(mem_ref, g_ref, w_ref, gk_ref, ones_ref, mk_ref, mv_ref):
    x = mem_ref[...]
    ms = jnp.mean(x * x, axis=-1, keepdims=True)
    xn = (x * lax.rsqrt(ms + EPS) * g_ref[...]).astype(BF16)
    zk = _dot(xn, w_ref[:, 0:WIDTH_C])
    ss = _dot_split(zk * zk, ones_ref[...], SQ_PARTS) * (1.0 / HEAD_DIM)
    mk_ref[...] = zk * lax.rsqrt(ss + EPS) * gk_ref[...]
    mv_ref[...] = _dot(xn, w_ref[:, WIDTH_C:2 * WIDTH_C])


def _memkv(mem, g, w, gk):
    n = mem.shape[0]
    return pl.pallas_call(
        _memkv_kernel,
        out_shape=[jax.ShapeDtypeStruct((n, WIDTH_C), F32)] * 2,
        compiler_params=pltpu.CompilerParams(vmem_limit_bytes=VMEM_LIMIT),
        name="memkv",
    )(mem, g, w, gk, _group_ones(WIDTH_C))


def _head_spread():
    e = np.zeros((LANES, WIDTH_A), np.float32)
    for h in range(N_HEADS_A):
        e[h, h * HEAD_DIM:(h + 1) * HEAD_DIM] = 1.0
    return jnp.asarray(e, BF16)


def _combine_mem_kernel(dils, *refs):
    nd = len(dils)
    o_refs, l_refs = refs[0:nd], refs[nd:2 * nd]
    qc_ref, mk_ref, mv_ref, e_ref, oa_ref, oc_ref, o_scr, l_scr = refs[2 * nd:]
    tm = oa_ref.shape[0]
    os_, ls = [], []
    for p, d in enumerate(dils):
        if d == 1:
            os_.append(o_refs[p][...].astype(F32))
            ls.append(l_refs[p][...])
            continue
        n_tiles = WIDTH_A // LANES
        for r in range(d):
            for c in range(n_tiles):
                col = r * WIDTH_A + c * LANES
                o_scr[p * n_tiles + c, pl.ds(r, tm // d, stride=d), :] = o_refs[p][:, col:col + LANES].astype(F32)
            l_scr[p, pl.ds(r, tm // d, stride=d), :] = l_refs[p][:, r * LANES:(r + 1) * LANES]
        os_.append(jnp.concatenate([o_scr[p * n_tiles + c] for c in range(n_tiles)], axis=1))
        ls.append(l_scr[p])
    head_lane = lax.broadcasted_iota(jnp.int32, (tm, LANES), 1) < N_HEADS_A
    m = functools.reduce(jnp.maximum, ls)
    es = [jnp.exp2(l - m) for l in ls]
    dens = [pltpu.roll(l, LANES - N_HEADS_A, 1) for l in ls]
    tot = functools.reduce(lambda a, b: a + b, [d * e for d, e in zip(dens, es)])
    tot = jnp.where(head_lane, tot, 1.0)
    acc = None
    for o, e in zip(os_, es):
        term = o * _dot_split(e / tot, e_ref[...], 1)
        acc = term if acc is None else acc + term
    oa_ref[...] = acc.astype(BF16)

    lane = lax.broadcasted_iota(jnp.int32, (tm, LANES), 1)
    low = lane < HEAD_DIM
    ones = jnp.ones((N_MEM, LANES), BF16)
    for pair in range(N_HEADS_C // 2):
        cols = slice(pair * LANES, (pair + 1) * LANES)
        q = qc_ref[:, cols]
        mk = mk_ref[:, cols].astype(BF16)
        v_ext = jnp.concatenate([mv_ref[:, cols].astype(BF16), ones], axis=1)
        outs = []
        for half in range(2):
            qm = jnp.where(low if half == 0 else ~low, q, jnp.zeros((), BF16))
            s = _dot_nt(qm, mk)
            p = jnp.exp2(s - jnp.max(s, axis=-1, keepdims=True)).astype(BF16)
            r = _dot(p, v_ext)
            outs.append(r[:, :LANES] / r[:, LANES:])
        oc_ref[:, cols] = jnp.where(low, outs[0], outs[1]).astype(BF16)


def _combine_mem(os_, lses, dils, qc, mk, mv):
    t = qc.shape[0]
    tm = COMBINE_ROW_TILE
    nd = len(dils)
    row = lambda w: pl.BlockSpec((tm, w), lambda i: (i, 0))
    o_specs = [pl.BlockSpec((tm // d, d * WIDTH_A), lambda i: (i, 0)) for d in dils]
    l_specs = [pl.BlockSpec((tm // d, d * LANES), lambda i: (i, 0)) for d in dils]
    return pl.pallas_call(
        functools.partial(_combine_mem_kernel, dils),
        grid=(t // tm,),
        in_specs=o_specs + l_specs + [row(WIDTH_C), _full((N_MEM, WIDTH_C)), _full((N_MEM, WIDTH_C)),
                                     _full((LANES, WIDTH_A))],
        out_specs=[row(WIDTH_A), row(WIDTH_C)],
        out_shape=[jax.ShapeDtypeStruct((t, WIDTH_A), BF16), jax.ShapeDtypeStruct((t, WIDTH_C), BF16)],
        scratch_shapes=[pltpu.VMEM((nd * WIDTH_A // LANES, tm, LANES), F32), pltpu.VMEM((nd, tm, LANES), F32)],
        compiler_params=_params("parallel"),
        name="combine_mem",
    )(*os_, *lses, qc, mk, mv, _head_spread())


def _head_lane_consts():
    e_g = np.zeros((LANES, WIDTH_B), np.float32)
    e_b = np.zeros((LANES, WIDTH_B), np.float32)
    for h in range(N_HEADS_B):
        e_g[h, h * HEAD_DIM:(h + 1) * HEAD_DIM] = 1.0
        e_b[N_HEADS_B + h, h * HEAD_DIM:(h + 1) * HEAD_DIM] = 1.0
    i = np.arange(CHUNK)
    ltri = (i[None, :] <= i[:, None]).astype(np.float32)
    return jnp.asarray(e_g, BF16), jnp.asarray(e_b, BF16), jnp.asarray(ltri, BF16), jnp.ones((CHUNK, CHUNK), BF16)


def _block_diag(x, mask):
    return jnp.where(mask, jnp.concatenate([x] * N_HEADS_B, axis=0), jnp.zeros((), x.dtype))


def _gdn_prep_kernel(units, x_ref, halo_ref, ab_ref, cw_ref, alog_ref, dtb_ref, ones_ref, eg_ref, eb_ref, ltri_ref, one64_ref,
                     m_ref, attn_ref, qd_ref, kd_ref, vb_ref, kbe_ref, egl_ref, xe_ref, m_scr):
    tm = x_ref.shape[0]
    nc = tm // CHUNK
    halo = jnp.where(pl.program_id(0) > 0, halo_ref[...], 0.0)
    xe_ref[0:8, :] = halo
    xe_ref[8:8 + tm, :] = x_ref[...]
    cw = cw_ref[...]
    y = cw[3:4] * xe_ref[8:8 + tm, :] + cw[2:3] * xe_ref[7:7 + tm, :] + cw[1:2] * xe_ref[6:6 + tm, :] \
        + cw[0:1] * xe_ref[5:5 + tm, :]
    y = _silu(y)

    def l2n(z):
        return z * lax.rsqrt(_dot_split(z * z, ones_ref[...], SQ_PARTS) + EPS)

    q = l2n(y[:, 0:WIDTH_B]) * (HEAD_DIM ** -0.5)
    k = l2n(y[:, WIDTH_B:2 * WIDTH_B])
    v = y[:, 2 * WIDTH_B:3 * WIDTH_B]
    ab = ab_ref[...]
    g = -jnp.exp(alog_ref[...]) * _softplus(ab + dtb_ref[...])
    gx = _dot_split(g, eg_ref[...], 2)
    bx = _dot_split(_sigmoid(ab), eb_ref[...], 2)

    row = lax.broadcasted_iota(jnp.int32, (CHUNK, WIDTH_B), 0)
    col = lax.broadcasted_iota(jnp.int32, (CHUNK, WIDTH_B), 1) % CHUNK
    diag = row == col
    r4 = lax.broadcasted_iota(jnp.int32, (WIDTH_B, WIDTH_B), 0) // HEAD_DIM
    c4 = lax.broadcasted_iota(jnp.int32, (WIDTH_B, WIDTH_B), 1) // HEAD_DIM
    bd_mask = r4 == c4
    for c in range(tm // CHUNK):
        sl = slice(c * CHUNK, (c + 1) * CHUNK)
        gc = _dot_split(gx[sl], ltri_ref[...], 2, left=True)
        gl = gc[CHUNK - 1:CHUNK, :]
        gcj = _dot_split(jnp.where(diag, gc, 0.0), one64_ref[...], 2, left=True)
        dec = jnp.exp(jnp.where(row >= col, gc - gcj, NEG))
        kc, qc, vc, bc = k[sl], q[sl], v[sl], bx[sl]
        kb = kc * bc
        lhs = jnp.concatenate([kb, qc], axis=0).astype(BF16)
        kkqk = _dot_nt(lhs, _block_diag(kc.astype(BF16), bd_mask))
        m_c = jnp.where(row > col, kkqk[0:CHUNK] * dec, 0.0)
        for lt in range(WIDTH_B // LANES):
            m_scr[pl.ds(lt * nc + c, CHUNK, stride=nc * WIDTH_B // LANES), :] = m_c[:, lt * LANES:(lt + 1) * LANES]
        attn_ref[sl, :] = (kkqk[CHUNK:2 * CHUNK] * dec).astype(BF16)
        egc = jnp.exp(gc)
        qd_ref[sl, :] = (qc * egc).astype(BF16)
        kd_ref[sl, :] = (kc * jnp.exp(gl - gc)).astype(BF16)
        vb_ref[sl, :] = (vc * bc).astype(BF16)
        kbe_ref[sl, :] = (kb * egc).astype(BF16)
        egl_ref[c:c + 1, :] = jnp.exp(gl)
        _run_share(units, c, nc, spread=False)
    m_ref[...] = m_scr[...].reshape(m_ref.shape)


def _gdn_prep(qkv, ab, conv_w, alog_row, dtb_row, rider=None):
    t = qkv.shape[0]
    tm = ROW_TILE
    nc = tm // CHUNK
    assert nc == 8
    n_lt = WIDTH_B // LANES
    row = lambda w: pl.BlockSpec((tm, w), lambda i: (i, 0))
    m_spec = pl.BlockSpec((CHUNK, n_lt, nc, LANES), lambda i: (0, 0, i, 0))
    e_g, e_b, ltri, one64 = _head_lane_consts()
    outs = [BF16, BF16, BF16, BF16, BF16]
    return _call_with_rider(
        _gdn_prep_kernel, rider,
        grid=(t // tm,),
        in_specs=[row(QKV_B), pl.BlockSpec((8, QKV_B), lambda i: (jnp.maximum(i * (tm // 8) - 1, 0), 0)), row(LANES),
                  _full((4, QKV_B)), _full((1, LANES)), _full((1, LANES)), _full((WIDTH_B, WIDTH_B)),
                  _full((LANES, WIDTH_B)), _full((LANES, WIDTH_B)), _full((CHUNK, CHUNK)), _full((CHUNK, CHUNK))],
        out_specs=[m_spec] + [row(WIDTH_B)] * 5 + [pl.BlockSpec((nc, WIDTH_B), lambda i: (i, 0))],
        out_shape=[jax.ShapeDtypeStruct((CHUNK, n_lt, t // CHUNK, LANES), F32)]
        + [jax.ShapeDtypeStruct((t, WIDTH_B), dt) for dt in outs]
        + [jax.ShapeDtypeStruct((t // CHUNK, WIDTH_B), F32)],
        scratch_shapes=[pltpu.VMEM((tm + 8, QKV_B), F32), pltpu.VMEM((CHUNK * n_lt * nc, LANES), F32)],
        args=(qkv, qkv, ab, conv_w, alog_row, dtb_row, _group_ones(WIDTH_B), e_g, e_b, ltri, one64),
        sem="parallel", name="gdn_prep")


def _tinv_kernel(m_ref, out_ref, n_ref, mrow_ref, nrow_ref):
    nslots = n_ref.shape[2]
    heads_per_tile = LANES // CHUNK
    blk = pl.program_id(0)

    @pl.when(blk == 0)
    def _():
        n_ref[...] = jnp.zeros_like(n_ref)

    for ii in range(8):
        i = blk * 8 + ii
        for g in range(nslots // N_HEADS_B):
            for lt in range(WIDTH_B // LANES):
                xt = m_ref[ii, lt, g * LANES:(g + 1) * LANES, :].T
                for h2 in range(heads_per_tile):
                    slot = g * N_HEADS_B + lt * heads_per_tile + h2
                    mrow_ref[ii, pl.ds(slot, CHUNK, stride=nslots), :] = xt[h2 * CHUNK:(h2 + 1) * CHUNK, :]
        for kb in range(8):
            k0 = 8 * kb
            init = tuple(mrow_ref[ii, (k0 + kk) * nslots:(k0 + kk + 1) * nslots, :] for kk in range(8))

            def m_at(j, ii=ii):
                return mrow_ref[ii, pl.ds(pl.multiple_of(j * nslots, nslots), nslots), :]

            def body(jj, acc, k0=k0):
                j = k0 + 2 * jj
                m0, m1 = m_at(j), m_at(j + 1)
                return tuple(acc[kk] + m0 * n_ref[j, k0 + kk] + m1 * n_ref[j + 1, k0 + kk] for kk in range(8))

            n_cols = jnp.maximum(i - k0, 0)
            acc = lax.fori_loop(0, n_cols // 2, body, init)
            odd = n_cols % 2 == 1
            j_last = jnp.where(odd, i - 1, 0)
            m_last = jnp.where(odd, m_at(j_last), 0.0)
            acc = tuple(acc[kk] + m_last * n_ref[j_last, k0 + kk] for kk in range(8))
            for kk in range(8):
                val = jnp.where(k0 + kk < i, -acc[kk], 0.0)
                n_ref[i, k0 + kk] = val
                nrow_ref[(k0 + kk) * nslots:(k0 + kk + 1) * nslots, :] = val
        for g in range(nslots // N_HEADS_B):
            for lt in range(WIDTH_B // LANES):
                slot0 = g * N_HEADS_B + lt * heads_per_tile
                parts = [nrow_ref[pl.ds(slot0 + h2, CHUNK, stride=nslots), :] for h2 in range(heads_per_tile)]
                out_ref[ii, lt, g * LANES:(g + 1) * LANES, :] = jnp.concatenate(parts, axis=0).T


def _tinv(m4):
    _, n_lt, nchunk, _ = m4.shape
    assert nchunk % LANES == 0
    nslots = (nchunk // LANES) * N_HEADS_B
    blk = pl.BlockSpec((8, n_lt, nchunk, LANES), lambda i: (i, 0, 0, 0))
    return pl.pallas_call(
        _tinv_kernel,
        grid=(CHUNK // 8,),
        in_specs=[blk],
        out_specs=blk,
        out_shape=jax.ShapeDtypeStruct(m4.shape, F32),
        scratch_shapes=[pltpu.VMEM((CHUNK, CHUNK, nslots, LANES), F32), pltpu.VMEM((8, CHUNK * nslots, LANES), F32),
                        pltpu.VMEM((CHUNK * nslots, LANES), F32)],
        compiler_params=_params("arbitrary"),
        name="gdn_tinv",
    )(m4)


def _gdn_scan_kernel(units, n_ref, vb_ref, kbe_ref, attn_ref, qd_ref, kd_ref, egl_ref, gate_ref, gn_ref, ones_ref,
                     ob_ref, s_out_ref, s_ref, n_scr):
    tm = vb_ref.shape[0]
    nc = tm // CHUNK
    n_lt = WIDTH_B // LANES

    @pl.when(pl.program_id(0) == 0)
    def _():
        s_ref[...] = jnp.zeros_like(s_ref)

    n_scr[...] = n_ref[...].reshape(n_scr.shape)

    row = lax.broadcasted_iota(jnp.int32, (CHUNK, WIDTH_B), 0)
    col = lax.broadcasted_iota(jnp.int32, (CHUNK, WIDTH_B), 1) % CHUNK
    eye = (row == col).astype(F32)
    r4 = lax.broadcasted_iota(jnp.int32, (WIDTH_B, WIDTH_B), 0) // HEAD_DIM
    c4 = lax.broadcasted_iota(jnp.int32, (WIDTH_B, WIDTH_B), 1) // HEAD_DIM
    bd_mask = r4 == c4
    s = s_ref[...]
    for c in range(tm // CHUNK):
        sl = slice(c * CHUNK, (c + 1) * CHUNK)
        n_c = jnp.concatenate([n_scr[pl.ds(lt * nc + c, CHUNK, stride=nc * n_lt), :] for lt in range(n_lt)], axis=1)
        tinv = (n_c + eye).astype(BF16)
        u = _dot(tinv, _block_diag(vb_ref[sl, :], bd_mask))
        w = _dot(tinv, _block_diag(kbe_ref[sl, :], bd_mask))
        lhs = jnp.concatenate([w.astype(BF16), qd_ref[sl, :]], axis=0)
        ws = _dot(lhs, s.astype(BF16))
        v_new = (u - ws[0:CHUNK]).astype(BF16)
        o = ws[CHUNK:2 * CHUNK] + _dot(attn_ref[sl, :], _block_diag(v_new, bd_mask))
        s = s * egl_ref[c:c + 1, :] + jnp.where(bd_mask, _dot_tn(kd_ref[sl, :], v_new), 0.0)
        ms = _dot_split(o * o, ones_ref[...], SQ_PARTS) * (1.0 / HEAD_DIM)
        ob_ref[sl, :] = (o * lax.rsqrt(ms + EPS) * gn_ref[...] * _silu(gate_ref[sl, :])).astype(BF16)
        _run_share(units, c, nc, spread=True)
    s_ref[...] = s
    s_out_ref[...] = s


def _gdn_scan(n4, vb, kbe, attn, qd, kd, egl, gate, gn_row, rider=None):
    t = vb.shape[0]
    tm = ROW_TILE
    nc = tm // CHUNK
    n_lt = WIDTH_B // LANES
    row = lambda w: pl.BlockSpec((tm, w), lambda i: (i, 0))
    return _call_with_rider(
        _gdn_scan_kernel, rider,
        grid=(t // tm,),
        in_specs=[pl.BlockSpec((CHUNK, n_lt, nc, LANES), lambda i: (0, 0, i, 0))] + [row(WIDTH_B)] * 5
        + [pl.BlockSpec((nc, WIDTH_B), lambda i: (i, 0)), row(WIDTH_B), _full((1, WIDTH_B)), _full((WIDTH_B, WIDTH_B))],
        out_specs=[row(WIDTH_B), _full((WIDTH_B, WIDTH_B))],
        out_shape=[jax.ShapeDtypeStruct((t, WIDTH_B), BF16), jax.ShapeDtypeStruct((WIDTH_B, WIDTH_B), F32)],
        scratch_shapes=[pltpu.VMEM((WIDTH_B, WIDTH_B), F32), pltpu.VMEM((CHUNK * n_lt * nc, LANES), F32)],
        args=(n4, vb, kbe, attn, qd, kd, egl, gate, gn_row, _group_ones(WIDTH_B)),
        sem="arbitrary", name="gdn_scan")


def _gdn_prompt(qkv, ab, gate, conv_w, alog_row, dtb_row, gn_row, riders=(None, None)):
    (m4, attn, qd, kd, vb, kbe, egl), ride0 = _gdn_prep(qkv, ab, conv_w, alog_row, dtb_row, riders[0])
    (ob, s_bd), ride1 = _gdn_scan(_tinv(m4), vb, kbe, attn, qd, kd, egl, gate, gn_row, riders[1])
    s_fin = jnp.stack([s_bd[h * HEAD_DIM:(h + 1) * HEAD_DIM, h * HEAD_DIM:(h + 1) * HEAD_DIM] for h in range(N_HEADS_B)])
    return ob, s_fin, (ride0, ride1)


def _ffn_kernel(shift, carry_rows, x_ref, oa_ref, ob_ref, oc_ref, wo_ref, g_ref, wup_ref, cw_ref, wdn_ref, c0_ref,
                y_ref, last_ref, carry_ref, ext_ref, hid_ref):
    tm = x_ref.shape[0]

    @pl.when(pl.program_id(0) == 0)
    def _():
        carry_ref[...] = c0_ref[...]

    mix = _dot(oa_ref[...], wo_ref[0:WIDTH_A, :])
    mix = mix + _dot(ob_ref[...], wo_ref[WIDTH_A:WIDTH_A + WIDTH_B, :])
    mix = mix + _dot(oc_ref[...], wo_ref[WIDTH_A + WIDTH_B:, :])
    x = x_ref[...] + mix
    ms = jnp.mean(x * x, axis=-1, keepdims=True)
    xn = (x * lax.rsqrt(ms + EPS) * g_ref[...]).astype(BF16)
    tf = FF_TILE
    base = carry_rows
    for j, lo in enumerate(range(0, D_FF, tf)):
        hi = min(lo + tf, D_FF)
        halves = []
        for half in range(2):
            cols = slice(half * D_FF + lo, half * D_FF + hi)
            up = _dot(xn, wup_ref[:, cols])
            ext = ext_ref.at[(j % 2) * 2 + half]
            ext[0:base, 0:hi - lo] = carry_ref[:, cols]
            ext[base:base + tm, 0:hi - lo] = up
            carry_ref[:, cols] = ext[tm:tm + base, 0:hi - lo]
            cw = cw_ref[:, cols]
            halves.append(cw[0:1] * ext[base - 2 * shift:base - 2 * shift + tm, 0:hi - lo]
                          + cw[1:2] * ext[base - shift:base - shift + tm, 0:hi - lo] + cw[2:3] * up)
        hid_ref[:, lo:hi] = (_silu(halves[0]) * halves[1]).astype(BF16)
    y_ref[...] = x + _dot(hid_ref[...], wdn_ref[...])
    last_ref[...] = carry_ref[...]


def _ffn(x, oa, ob, oc, w_out, g, w_up, conv_w, w_down, carry0, shift, tm):
    rows = x.shape[0]
    carry_rows = carry0.shape[0]
    row = lambda w: pl.BlockSpec((tm, w), lambda i: (i, 0))
    once = lambda shape: pl.BlockSpec(shape, lambda i: (0,) * len(shape), pipeline_mode=pl.Buffered(1))
    return pl.pallas_call(
        functools.partial(_ffn_kernel, shift, carry_rows),
        grid=(rows // tm,),
        in_specs=[row(D_MODEL), row(WIDTH_A), row(WIDTH_B), row(WIDTH_C), once((D_MODEL, D_MODEL)),
                  once((1, D_MODEL)), once((D_MODEL, 2 * D_FF)), once((3, 2 * D_FF)),
                  once((D_FF, D_MODEL)), once((carry_rows, 2 * D_FF))],
        out_specs=[row(D_MODEL), _full((carry_rows, 2 * D_FF))],
        out_shape=[jax.ShapeDtypeStruct((rows, D_MODEL), F32), jax.ShapeDtypeStruct((carry_rows, 2 * D_FF), F32)],
        scratch_shapes=[pltpu.VMEM((carry_rows, 2 * D_FF), F32), pltpu.VMEM((4, tm + carry_rows, FF_TILE), F32),
                        pltpu.VMEM((tm, D_FF), BF16)],
        compiler_params=_params("arbitrary"),
        name=f"ffn_shift{shift}",
    )(x, oa, ob, oc, w_out, g, w_up, conv_w, w_down, carry0)


PAIR_ROWS = 16
NEW_ROWS = 16


def _sample_bias(wbuf, s_len):
    def table(key_pos):
        q_pos = wbuf + np.arange(s_len)
        dist = q_pos[:, None] - key_pos[None, :]
        mult = np.zeros(dist.shape)
        for window, dil in PATTERNS:
            mult += (dist >= 0) & (dist <= window) & (dist % dil == 0)
        logm = np.log(np.maximum(mult, 1))
        bias = -_alibi_slopes()[:, None, None] * np.maximum(dist, 0)[None] + logm[None]
        bias = np.where(mult[None] > 0, bias, NEG)
        out = np.zeros((N_HEADS_A, PAIR_ROWS // 2, key_pos.shape[0]))
        out[:, :s_len] = bias
        return out.reshape(N_HEADS_A // 2, PAIR_ROWS, -1)
    main = table(np.arange(wbuf))
    new = table(np.concatenate([wbuf + np.arange(s_len), np.full(NEW_ROWS - s_len, 10 ** 6)]))
    new[:, :, s_len:] = NEG
    return jnp.asarray(main, F32), jnp.asarray(new, F32)


def _sample_attn_units(q_ref, kn_ref, vn_ref, kt_ref, vt_ref, bm_ref, bn_ref, qc_ref, mkt_ref, mvt_ref, oa_ref, oc_ref):
    def times_values(p, vt):
        return _dot_nt(p.astype(BF16), vt.astype(BF16))

    def window(b):
        scores = []
        for pair in range(N_HEADS_A // 2):
            rows = slice(pair * LANES, (pair + 1) * LANES)
            q = q_ref[b, pair]
            scores.append((_dot(q, kt_ref[b, rows, :].astype(BF16)) + bm_ref[pair],
                           _dot_nt(q, kn_ref[b, :, rows].astype(BF16)) + bn_ref[pair]))
        for pair, (s_main, s_new) in enumerate(scores):
            rows = slice(pair * LANES, (pair + 1) * LANES)
            m = jnp.maximum(jnp.max(s_main, axis=-1, keepdims=True), jnp.max(s_new, axis=-1, keepdims=True))
            p_main = jnp.exp(s_main - m)
            p_new = jnp.exp(s_new - m)
            den = jnp.sum(p_main, axis=-1, keepdims=True) + jnp.sum(p_new, axis=-1, keepdims=True)
            num = times_values(p_main, vt_ref[b, rows, :])
            num = num + _dot(p_new.astype(BF16), vn_ref[b, :, rows].astype(BF16))
            oa_ref[b, pair] = num / den

    def memory(b):
        for pair in range(N_HEADS_C // 2):
            rows = slice(pair * LANES, (pair + 1) * LANES)
            q = qc_ref[b, pair]
            s = _dot(q, mkt_ref[b, rows, :].astype(BF16))
            p = jnp.exp(s - jnp.max(s, axis=-1, keepdims=True))
            oc_ref[b, pair] = times_values(p, mvt_ref[b, rows, :]) / jnp.sum(p, axis=-1, keepdims=True)

    units = []
    for b in range(q_ref.shape[0]):
        units += [functools.partial(window, b), functools.partial(memory, b)]
    return units


def _run_share(units, stage, n_stages, spread):
    lo, hi = (stage * len(units) // n_stages, (stage + 1) * len(units) // n_stages) if spread else \
        ((0, len(units)) if stage == n_stages - 1 else (0, 0))
    for u in units[lo:hi]:
        u()


def _pair_rows(q, s_len):
    nb, _, nh, hd = q.shape
    q = q.reshape(nb, s_len, nh // 2, 2, hd).transpose(0, 2, 3, 1, 4)
    q = jnp.pad(q, ((0, 0), (0, 0), (0, 0), (0, PAIR_ROWS // 2 - s_len), (0, 0)))
    own = jnp.eye(2, dtype=q.dtype)[None, None, :, None, :, None]
    return (q[:, :, :, :, None, :] * own).reshape(nb, nh // 2, PAIR_ROWS, 2 * hd)


def _unpair_rows(o, s_len):
    nb, npair = o.shape[:2]
    o = o.reshape(nb, npair, 2, PAIR_ROWS // 2, 2, HEAD_DIM)
    o = jnp.stack([o[:, :, half, :s_len, half, :] for half in range(2)], axis=2)
    return o.transpose(0, 3, 1, 2, 4).reshape(nb, s_len, npair * 2 * HEAD_DIM)


def _sample_attn_riders(qa, ka, va, cache_k, cache_v, qc, mem_k, mem_v, n_parts, steps):
    nb, s_len = qa.shape[:2]
    wbuf = cache_k.shape[1]
    n_mem = mem_k.shape[1]
    assert s_len <= PAIR_ROWS // 2 and nb % (n_parts * steps) == 0
    g = nb // (n_parts * steps)
    bm, bn = _sample_bias(wbuf, s_len)
    transposed = lambda a: a.transpose(0, 2, 3, 1).reshape(nb, a.shape[2] * a.shape[3], a.shape[1])
    pad_new = lambda a: jnp.pad(a, ((0, 0), (0, NEW_ROWS - s_len), (0, 0)))
    args = [_pair_rows(qa, s_len), pad_new(ka), pad_new(va), transposed(cache_k), transposed(cache_v), bm, bn,
            _pair_rows(qc, s_len), transposed(mem_k), transposed(mem_v)]
    riders = []
    for part in range(n_parts):
        per_b = lambda shape, part=part: pl.BlockSpec((g,) + shape, lambda i: (i + part * steps,) + (0,) * len(shape))
        out_b = lambda shape: pl.BlockSpec((g,) + shape, lambda i: (i,) + (0,) * len(shape))
        riders.append(dict(
            units=_sample_attn_units,
            in_specs=[per_b((N_HEADS_A // 2, PAIR_ROWS, LANES)), per_b((NEW_ROWS, WIDTH_A)), per_b((NEW_ROWS, WIDTH_A)),
                      per_b((WIDTH_A, wbuf)), per_b((WIDTH_A, wbuf)), _full(bm.shape), _full(bn.shape),
                      per_b((N_HEADS_C // 2, PAIR_ROWS, LANES)), per_b((WIDTH_C, n_mem)), per_b((WIDTH_C, n_mem))],
            out_specs=[out_b((N_HEADS_A // 2, PAIR_ROWS, LANES)), out_b((N_HEADS_C // 2, PAIR_ROWS, LANES))],
            out_shape=[jax.ShapeDtypeStruct((g * steps, N_HEADS_A // 2, PAIR_ROWS, LANES), F32),
                       jax.ShapeDtypeStruct((g * steps, N_HEADS_C // 2, PAIR_ROWS, LANES), F32)],
            args=args))

    def finish(rider_outs):
        oa = jnp.concatenate([o[0] for o in rider_outs], axis=0)
        oc = jnp.concatenate([o[1] for o in rider_outs], axis=0)
        return _unpair_rows(oa, s_len).astype(BF16), _unpair_rows(oc, s_len).astype(BF16)

    return riders, finish


def _sample_gdn_kernel(xe_ref, cw_ref, a_ref, b_ref, alog_ref, dtb_ref, gate_ref, gn_ref, s_ref, o_ref, s_out_ref,
                       q_scr, k_scr, v_scr):
    s_len = o_ref.shape[0]
    s_out_ref[...] = s_ref[...]
    for t in range(s_len):
        for part, scr in enumerate((q_scr, k_scr, v_scr)):
            y = sum(cw_ref[j, part] * xe_ref[t + j, part] for j in range(4))
            y = _silu(y)
            if part < 2:
                y = y * lax.rsqrt(jnp.sum(y * y, axis=0, keepdims=True) + EPS)
            if part == 0:
                y = y * (HEAD_DIM ** -0.5)
            scr[...] = y
        decay = jnp.exp(-jnp.exp(alog_ref[...]) * _softplus(a_ref[t] + dtb_ref[...]))
        beta = _sigmoid(b_ref[t])

        def read_body(dk, r):
            return r + k_scr[pl.ds(dk, 1), :] * s_out_ref[dk]

        r = lax.fori_loop(0, HEAD_DIM, read_body, jnp.zeros(v_scr.shape, F32), unroll=8)
        v_new = beta * (v_scr[...] - decay * r)

        def write_body(dk, o):
            s_new = decay * s_out_ref[dk] + k_scr[pl.ds(dk, 1), :] * v_new
            s_out_ref[dk] = s_new
            return o + q_scr[pl.ds(dk, 1), :] * s_new

        o = lax.fori_loop(0, HEAD_DIM, write_body, jnp.zeros(v_scr.shape, F32), unroll=8)
        ms = jnp.mean(o * o, axis=0, keepdims=True)
        o_ref[t] = o * lax.rsqrt(ms + EPS) * gn_ref[...] * _silu(gate_ref[t])


def _sample_gdn(xe, conv_w, a_b, b_b, a_log, dt_bias, gate, out_norm, state):
    nb, ext, _ = xe.shape
    s_len = ext - 3
    nh, hd = N_HEADS_B, HEAD_DIM
    xe_t = xe.reshape(nb, ext, 3, nh, hd).transpose(1, 2, 3, 4, 0)
    cw_t = jnp.broadcast_to(conv_w.reshape(4, 3, nh, hd)[..., None], (4, 3, nh, hd, nb))
    a_t = a_b.transpose(1, 2, 0).reshape(s_len, nh, 1, nb)
    b_t = b_b.transpose(1, 2, 0).reshape(s_len, nh, 1, nb)
    alog_t = jnp.broadcast_to(a_log.reshape(nh, 1, 1), (nh, 1, nb))
    dtb_t = jnp.broadcast_to(dt_bias.reshape(nh, 1, 1), (nh, 1, nb))
    gate_t = gate.reshape(nb, s_len, nh, hd).transpose(1, 2, 3, 0)
    gn_t = jnp.broadcast_to(out_norm.reshape(1, hd, 1), (1, hd, nb))
    s_t = state.transpose(1, 2, 3, 0)
    o_t, s_new = pl.pallas_call(
        _sample_gdn_kernel,
        grid=(nh,),
        in_specs=[pl.BlockSpec((ext, 3, None, hd, nb), lambda h: (0, 0, h, 0, 0)),
                  pl.BlockSpec((4, 3, None, hd, nb), lambda h: (0, 0, h, 0, 0)),
                  pl.BlockSpec((s_len, None, 1, nb), lambda h: (0, h, 0, 0)),
                  pl.BlockSpec((s_len, None, 1, nb), lambda h: (0, h, 0, 0)),
                  pl.BlockSpec((None, 1, nb), lambda h: (h, 0, 0)),
                  pl.BlockSpec((None, 1, nb), lambda h: (h, 0, 0)),
                  pl.BlockSpec((s_len, None, hd, nb), lambda h: (0, h, 0, 0)),
                  pl.BlockSpec((None, hd, nb), lambda h: (0, 0, 0)),
                  pl.BlockSpec((None, hd, hd, nb), lambda h: (h, 0, 0, 0))],
        out_specs=[pl.BlockSpec((s_len, None, hd, nb), lambda h: (0, h, 0, 0)),
                   pl.BlockSpec((None, hd, hd, nb), lambda h: (h, 0, 0, 0))],
        out_shape=[jax.ShapeDtypeStruct((s_len, nh, hd, nb), F32), jax.ShapeDtypeStruct((nh, hd, hd, nb), F32)],
        scratch_shapes=[pltpu.VMEM((hd, nb), F32)] * 3,
        compiler_params=_params("parallel"),
        name="sample_gdn",
    )(xe_t, cw_t, a_t, b_t, alog_t, dtb_t, gate_t, gn_t, s_t)
    ob = o_t.transpose(3, 0, 1, 2).reshape(nb * s_len, nh * hd).astype(BF16)
    return ob, s_new.transpose(3, 0, 1, 2)


def _lane_row(v, width=LANES):
    return jnp.zeros((1, width), F32).at[0, :v.shape[0]].set(v)


def kernel(x_prompt, x_sample, cache_win_k, cache_win_v, state_gdn, state_gdn_conv, state_ffn_conv, cache_mem_k,
           cache_mem_v, mem_prompt, norm1_g, w_in, q_norm_a, k_norm_a, conv_b_w, a_log_b, dt_bias_b, out_norm_b,
           mem_norm_g, w_mem_kv, q_norm_c, k_norm_c, w_out, norm2_g, w_up, conv_ffn_w, w_down):
    depth = norm1_g.shape[0]
    assert depth == 1 and x_prompt.shape[0] == 1
    l = 0
    t_p = x_prompt.shape[1]
    nb, s_len = x_sample.shape[:2]
    xp = x_prompt.reshape(t_p, D_MODEL)
    xs = x_sample.reshape(nb * s_len, D_MODEL)

    w_in_p = w_in[l].T.astype(BF16)
    w_out_b = w_out[l].astype(BF16)
    w_up_b = w_up[l].astype(BF16)
    w_down_b = w_down[l].astype(BF16)
    g1 = norm1_g[l].reshape(1, D_MODEL)
    g2 = norm2_g[l].reshape(1, D_MODEL)
    gq = jnp.tile(q_norm_a[l], N_HEADS_A).reshape(1, WIDTH_A)
    gk = jnp.tile(k_norm_a[l], N_HEADS_A).reshape(1, WIDTH_A)
    gqc = jnp.tile(q_norm_c[l], N_HEADS_C).reshape(1, WIDTH_C)
    gkc = jnp.tile(k_norm_c[l], N_HEADS_C).reshape(1, WIDTH_C)
    gn_row = jnp.tile(out_norm_b[l], N_HEADS_B).reshape(1, WIDTH_B)

    rows_s = nb * s_len
    (qa_s,), _, _, (ka_s, va_s, qkv_s, gate_s, qc_s, ab_s) = _inproj(xs, g1, w_in_p, gq, gk, gqc, (1,), rows_s,
                                                                         HEAD_DIM ** -0.5)
    b3 = lambda a: a.reshape(nb, s_len, a.shape[-1])
    heads = lambda a: a.reshape(nb, s_len, -1, HEAD_DIM)
    riders, finish_sample_attn = _sample_attn_riders(
        heads(qa_s), b3(ka_s), b3(va_s), cache_win_k[l], cache_win_v[l], heads(qc_s), cache_mem_k[l], cache_mem_v[l],
        n_parts=2, steps=t_p // ROW_TILE)

    n_keep = min(MAX_WINDOW, t_p)
    q_ds, k_ds, v_ds, (ka, va, qkv, gate, qc, ab) = _inproj(xp, g1, w_in_p, gq, gk, gqc, DILATIONS, n_keep,
                                                            HEAD_DIM ** -0.5 * LOG2E)
    parts = [_attn_band(q_d, k_d, v_d, dil) for q_d, k_d, v_d, dil in zip(q_ds, k_ds, v_ds, DILATIONS)]
    mk, mv = _memkv(mem_prompt[0], mem_norm_g[l].reshape(1, D_MODEL), w_mem_kv[l].astype(BF16), gkc)
    oa, oc = _combine_mem([p[0] for p in parts], [p[1] for p in parts], DILATIONS, qc, mk, mv)
    ob, gdn_p, rider_outs = _gdn_prompt(qkv, ab, gate, conv_b_w[l], _lane_row(a_log_b[l]), _lane_row(dt_bias_b[l]),
                                        gn_row, riders)
    y_p, last_p = _ffn(xp, oa, ob, oc, w_out_b, g2, w_up_b, conv_ffn_w[l], w_down_b, jnp.zeros((8, 2 * D_FF), F32), 1,
                       FFN_ROW_TILE)
    win_k_p = ka.reshape(1, 1, n_keep, N_HEADS_A, HEAD_DIM)
    win_v_p = va.reshape(1, 1, n_keep, N_HEADS_A, HEAD_DIM)
    gconv_p = qkv[t_p - 3:].reshape(1, 1, 3, QKV_B)
    fconv_p = last_p[6:8].reshape(1, 1, 2, 2 * D_FF)

    oa_s, oc_s = finish_sample_attn(rider_outs)
    xe_s = jnp.concatenate([state_gdn_conv[l], b3(qkv_s)], axis=1)
    ab3 = b3(ab_s)
    ob_s, gdn_s = _sample_gdn(xe_s, conv_b_w[l], ab3[..., 0:N_HEADS_B], ab3[..., N_HEADS_B:2 * N_HEADS_B], a_log_b[l],
                              dt_bias_b[l], b3(gate_s), out_norm_b[l], state_gdn[l])
    time_major = lambda a: a.reshape(nb, s_len, a.shape[-1]).transpose(1, 0, 2).reshape(rows_s, a.shape[-1])
    carry_s = state_ffn_conv[l].transpose(1, 0, 2).reshape(2 * nb, 2 * D_FF)
    y_t, last_s = _ffn(time_major(xs), time_major(oa_s), time_major(ob_s), time_major(oc_s), w_out_b, g2, w_up_b,
                       conv_ffn_w[l], w_down_b, carry_s, nb, rows_s)
    y_s = y_t.reshape(s_len, nb, D_MODEL).transpose(1, 0, 2)
    fconv_s = last_s.reshape(2, nb, 2 * D_FF).transpose(1, 0, 2)[None]

    return (y_p.reshape(1, t_p, D_MODEL), y_s,
            win_k_p, win_v_p,
            ka_s.reshape(1, nb, s_len, N_HEADS_A, HEAD_DIM), va_s.reshape(1, nb, s_len, N_HEADS_A, HEAD_DIM),
            gdn_p[None, None], gdn_s[None],
            gconv_p, xe_s[:, -3:][None],
            fconv_p, fconv_s,
            mk.reshape(1, 1, N_MEM, N_HEADS_C, HEAD_DIM), mv.reshape(1, 1, N_MEM, N_HEADS_C, HEAD_DIM))
```
